```python
import math
import jax, jax.numpy as jnp
from jax import lax
import numpy as np

D_MODEL = 1024
BATCH = 16
SEQ = 4096
DEPTH = 1
DEC_BATCH = 8
DEC_SEQ = 2048
PAST_LEN = 128

HEAD_DIM = 64
H_DIFF = 4
H_DIL = 4
DIL_CONFIGS = ((128, 1), (512, 4), (2048, 16))
N_DIL = len(DIL_CONFIGS)
N_ATTN_HEADS = H_DIFF + N_DIL * H_DIL
DIFF_QK = H_DIFF * 2 * HEAD_DIM
DIFF_V = H_DIFF * 2 * HEAD_DIM
DIL_W = N_DIL * H_DIL * HEAD_DIM
N_BRANCH = 2
W_IN_COLS = 2 * DIFF_QK + DIFF_V + 3 * DIL_W + N_BRANCH * D_MODEL
SPLITS = (DIFF_QK, 2 * DIFF_QK, 2 * DIFF_QK + DIFF_V,
          2 * DIFF_QK + DIFF_V + DIL_W, 2 * DIFF_QK + DIFF_V + 2 * DIL_W,
          2 * DIFF_QK + DIFF_V + 3 * DIL_W)
Q_BLOCK = 128
N_BUCKETS = 32
MAX_DISTANCE = 128
N_GROUPS = 4
E_PER_GROUP = 4
N_EXPERTS = N_GROUPS * E_PER_GROUP
TOP_K = 2
D_EXPERT = 256
EPS = 1e-6
NEG_INF = -1e30

kernel_name = "hybrid_diff_dilated_hmoe_encoder"


def rmsnorm(x, g):
    xf = x.astype(jnp.float32)
    y = xf * lax.rsqrt(jnp.mean(xf * xf, axis=-1, keepdims=True) + EPS)
    return (y * g.astype(jnp.float32)).astype(x.dtype)


def t5_bucket(rel):
    half = N_BUCKETS // 2
    max_exact = half // 2
    ret = (rel > 0).astype(jnp.int32) * half
    n = jnp.abs(rel)
    nf = jnp.maximum(n, 1).astype(jnp.float32)
    large = max_exact + (jnp.log(nf / max_exact) / math.log(MAX_DISTANCE / max_exact)
                         * (half - max_exact)).astype(jnp.int32)
    large = jnp.minimum(large, half - 1)
    return ret + jnp.where(n < max_exact, n, large)


def diff_attention(q, k, v, bias_table, lam):
    B, S, H, _, dh = q.shape
    nqb = S // Q_BLOCK
    scale = dh ** -0.5
    qblocks = q.reshape(B, nqb, Q_BLOCK, H, 2, dh).transpose(1, 0, 2, 3, 4, 5)
    kf = k.astype(jnp.float32)
    vf = v.astype(jnp.float32)
    kpos = jnp.arange(S)

    def block(args):
        qb, start = args
        qpos = start + jnp.arange(Q_BLOCK)
        bias = bias_table[t5_bucket(kpos[None, :] - qpos[:, None])]
        bias = bias.transpose(2, 0, 1).astype(jnp.float32)
        s = jnp.einsum('bqhcd,bkhcd->cbhqk', qb.astype(jnp.float32), kf) * scale + bias
        p = jax.nn.softmax(s, axis=-1)
        a = p[0] - lam * p[1]
        return jnp.einsum('bhqk,bkhe->bqhe', a, vf)

    starts = jnp.arange(nqb) * Q_BLOCK
    o = lax.map(block, (qblocks, starts))
    return o.transpose(1, 0, 2, 3, 4).reshape(B, S, H, 2 * dh)


def dilated_group(q, k, v, bias_table, window, dilation):
    B, S, H, dh = q.shape
    L = S // dilation
    W = window // (2 * dilation)
    nb = -(-L // Q_BLOCK)
    Lp = nb * Q_BLOCK
    KW = Q_BLOCK + 2 * W
    scale = dh ** -0.5

    def residues(t):
        return t.astype(jnp.float32).reshape(B, L, dilation, H, dh).transpose(0, 2, 1, 3, 4)

    qr = jnp.pad(residues(q), ((0, 0), (0, 0), (0, Lp - L), (0, 0), (0, 0)))
    qr = qr.reshape(B, dilation, nb, Q_BLOCK, H, dh)
    pad_kv = ((0, 0), (0, 0), (W, W + Lp - L), (0, 0), (0, 0))
    kp = jnp.pad(residues(k), pad_kv)
    vp = jnp.pad(residues(v), pad_kv)
    idx = jnp.arange(nb)[:, None] * Q_BLOCK + jnp.arange(KW)[None, :]
    kblk = kp[:, :, idx]
    vblk = vp[:, :, idx]
    off = jnp.arange(KW)[None, :] - W - jnp.arange(Q_BLOCK)[:, None]
    band = jnp.abs(off) <= W
    jsub = idx - W
    kvalid = (jsub >= 0) & (jsub < L)
    mask = band[None, :, :] & kvalid[:, None, :]
    bias = bias_table[t5_bucket(off * dilation)].transpose(2, 0, 1).astype(jnp.float32)
    s = jnp.einsum('bmnqhd,bmnkhd->bmnhqk', qr, kblk) * scale + bias
    s = jnp.where(mask[None, None, :, None, :, :], s, NEG_INF)
    mx = jnp.max(s, axis=-1, keepdims=True)
    p = jnp.exp(s - mx)
    den = jnp.sum(p, axis=-1)
    o = jnp.einsum('bmnhqk,bmnkhd->bmnqhd', p, vblk) / den.transpose(0, 1, 2, 4, 3)[..., None]
    lse = (mx[..., 0] + jnp.log(den)).transpose(0, 1, 2, 4, 3)
    o = o.reshape(B, dilation, Lp, H, dh)[:, :, :L].transpose(0, 2, 1, 3, 4).reshape(B, S, H, dh)
    lse = lse.reshape(B, dilation, Lp, H)[:, :, :L].transpose(0, 2, 1, 3).reshape(B, S, H)
    return o, lse


def mixer(h, w_in, rel_bias, lam_q1, lam_k1, lam_q2, lam_k2, subln_g,
          w_branch_a, w_branch_b, w_out, lam_init):
    B, S, _ = h.shape
    proj = h @ w_in
    qa, ka, va, qb, kb, vb, gates = jnp.split(proj, SPLITS, axis=-1)
    qa = qa.reshape(B, S, H_DIFF, 2, HEAD_DIM)
    ka = ka.reshape(B, S, H_DIFF, 2, HEAD_DIM)
    va = va.reshape(B, S, H_DIFF, 2 * HEAD_DIM)
    lam = (jnp.exp(jnp.sum(lam_q1.astype(jnp.float32) * lam_k1.astype(jnp.float32)))
           - jnp.exp(jnp.sum(lam_q2.astype(jnp.float32) * lam_k2.astype(jnp.float32)))
           + lam_init)
    oa = diff_attention(qa, ka, va, rel_bias[:, :H_DIFF], lam)
    oa = rmsnorm(oa, subln_g) * (1.0 - lam_init)
    oa = oa.reshape(B, S, DIFF_V).astype(h.dtype)
    qb = qb.reshape(B, S, N_DIL, H_DIL, HEAD_DIM)
    kb = kb.reshape(B, S, N_DIL, H_DIL, HEAD_DIM)
    vb = vb.reshape(B, S, N_DIL, H_DIL, HEAD_DIM)
    outs, lses = [], []
    for g, (win, dil) in enumerate(DIL_CONFIGS):
        c0 = H_DIFF + g * H_DIL
        o_g, lse_g = dilated_group(qb[:, :, g], kb[:, :, g], vb[:, :, g],
                                   rel_bias[:, c0:c0 + H_DIL], win, dil)
        outs.append(o_g)
        lses.append(lse_g)
    alpha = jax.nn.softmax(jnp.stack(lses, axis=0), axis=0)[..., None]
    ob = jnp.sum(alpha * jnp.stack(outs, axis=0), axis=0)
    ob = ob.reshape(B, S, H_DIL * HEAD_DIM).astype(h.dtype)
    gts = jax.nn.sigmoid(gates.astype(jnp.float32)).reshape(B, S, N_BRANCH, D_MODEL)
    merged = (gts[:, :, 0] * (oa @ w_branch_a).astype(jnp.float32)
              + gts[:, :, 1] * (ob @ w_branch_b).astype(jnp.float32))
    return merged.astype(h.dtype) @ w_out


def hier_moe(h, w_group, w_router, w_exp_gate, w_exp_up, w_exp_down):
    B, S, D = h.shape
    t = h.reshape(B * S, D)
    gl = (t @ w_group).astype(jnp.float32)
    gp = jax.nn.softmax(gl, axis=-1)
    gi = jnp.argmax(gl, axis=-1)
    gw = jnp.take_along_axis(gp, gi[:, None], axis=-1)
    el = jnp.einsum('nd,gde->nge', t, w_router).astype(jnp.float32)
    esel = jnp.take_along_axis(el, gi[:, None, None], axis=1)[:, 0]
    tv, ti = lax.top_k(esel, TOP_K)
    tw = jax.nn.softmax(tv, axis=-1) * gw
    eid = gi[:, None] * E_PER_GROUP + ti
    dense_w = jnp.sum(jax.nn.one_hot(eid, N_EXPERTS, dtype=jnp.float32) * tw[..., None], axis=1)
    out = jnp.zeros((B * S, D), jnp.float32)
    for e in range(N_EXPERTS):
        he = jax.nn.silu(t @ w_exp_gate[e]) * (t @ w_exp_up[e])
        out = out + dense_w[:, e:e + 1] * (he @ w_exp_down[e]).astype(jnp.float32)
    return out.astype(h.dtype).reshape(B, S, D)


def encoder(x, rel_bias, norm_mix, w_in, lam_q1, lam_k1, lam_q2, lam_k2, subln_g,
            w_branch_a, w_branch_b, w_out, norm_ffn, w_group, w_router,
            w_exp_gate, w_exp_up, w_exp_down, norm_final):
    for l in range(DEPTH):
        lam_init = 0.8 - 0.6 * math.exp(-0.3 * l)
        h = rmsnorm(x, norm_mix[l])
        x = x + mixer(h, w_in[l], rel_bias, lam_q1[l], lam_k1[l], lam_q2[l], lam_k2[l],
                      subln_g[l], w_branch_a[l], w_branch_b[l], w_out[l], lam_init)
        h = rmsnorm(x, norm_ffn[l])
        x = x + hier_moe(h, w_group[l], w_router[l], w_exp_gate[l], w_exp_up[l], w_exp_down[l])
    return rmsnorm(x, norm_final)


def setup_inputs(seed: int = 0) -> dict:
    key = jax.random.key(seed)
    ks = jax.random.split(key, 24)
    f32 = jnp.float32

    def nrm(k, shape, scale):
        return jax.random.normal(k, shape, f32) * scale

    def gain(k, shape):
        return 1.0 + 0.02 * jax.random.normal(k, shape, f32)

    return {
        "x_prompt": nrm(ks[0], (BATCH, SEQ, D_MODEL), 1.0),
        "x_sample": nrm(ks[1], (DEC_BATCH, DEC_SEQ, D_MODEL), 1.0),
        "rel_bias": nrm(ks[2], (N_BUCKETS, N_ATTN_HEADS), 0.1),
        "norm_mix": gain(ks[3], (DEPTH, D_MODEL)),
        "w_in": nrm(ks[4], (DEPTH, D_MODEL, W_IN_COLS), D_MODEL ** -0.5),
        "lam_q1": nrm(ks[5], (DEPTH, HEAD_DIM), 0.1),
        "lam_k1": nrm(ks[6], (DEPTH, HEAD_DIM), 0.1),
        "lam_q2": nrm(ks[7], (DEPTH, HEAD_DIM), 0.1),
        "lam_k2": nrm(ks[8], (DEPTH, HEAD_DIM), 0.1),
        "subln_g": gain(ks[9], (DEPTH, 2 * HEAD_DIM)),
        "w_branch_a": nrm(ks[10], (DEPTH, DIFF_V, D_MODEL), DIFF_V ** -0.5),
        "w_branch_b": nrm(ks[11], (DEPTH, H_DIL * HEAD_DIM, D_MODEL), (H_DIL * HEAD_DIM) ** -0.5),
        "w_out": nrm(ks[12], (DEPTH, D_MODEL, D_MODEL), D_MODEL ** -0.5),
        "norm_ffn": gain(ks[13], (DEPTH, D_MODEL)),
        "w_group": nrm(ks[14], (DEPTH, D_MODEL, N_GROUPS), D_MODEL ** -0.5),
        "w_router": nrm(ks[15], (DEPTH, N_GROUPS, D_MODEL, E_PER_GROUP), D_MODEL ** -0.5),
        "w_exp_gate": nrm(ks[16], (DEPTH, N_EXPERTS, D_MODEL, D_EXPERT), D_MODEL ** -0.5),
        "w_exp_up": nrm(ks[17], (DEPTH, N_EXPERTS, D_MODEL, D_EXPERT), D_MODEL ** -0.5),
        "w_exp_down": nrm(ks[18], (DEPTH, N_EXPERTS, D_EXPERT, D_MODEL), D_EXPERT ** -0.5),
        "norm_final": gain(ks[19], (D_MODEL,)),
    }


def reference(x_prompt, x_sample, rel_bias, norm_mix, w_in, lam_q1, lam_k1, lam_q2, lam_k2,
              subln_g, w_branch_a, w_branch_b, w_out, norm_ffn, w_group, w_router,
              w_exp_gate, w_exp_up, w_exp_down, norm_final):
    y_prompt = encoder(x_prompt, rel_bias, norm_mix, w_in, lam_q1, lam_k1, lam_q2, lam_k2,
                       subln_g, w_branch_a, w_branch_b, w_out, norm_ffn, w_group, w_router,
                       w_exp_gate, w_exp_up, w_exp_down, norm_final)
    y_sample = encoder(x_sample, rel_bias, norm_mix, w_in, lam_q1, lam_k1, lam_q2, lam_k2,
                       subln_g, w_branch_a, w_branch_b, w_out, norm_ffn, w_group, w_router,
                       w_exp_gate, w_exp_up, w_exp_down, norm_final)
    return (y_prompt, y_sample)
```

```python
import functools
import math

import jax
import jax.numpy as jnp
from jax import lax
from jax.experimental import pallas as pl
from jax.experimental.pallas import tpu as pltpu

F32 = jnp.float32
BF16 = jnp.bfloat16

D_MODEL = 1024
HEAD_DIM = 64
H_DIFF = 4
H_DIL = 4
DIL_CONFIGS = ((128, 1), (512, 4), (2048, 16))
N_DIL = len(DIL_CONFIGS)
DIFF_QK = H_DIFF * 2 * HEAD_DIM
DIFF_V = H_DIFF * 2 * HEAD_DIM
DIL_W = N_DIL * H_DIL * HEAD_DIM
ATTN_COLS = 2 * DIFF_QK + DIFF_V + 3 * DIL_W
GATE_COLS = 2 * D_MODEL
Q_BLOCK = 128
N_BUCKETS = 32
MAX_DISTANCE = 128
N_GROUPS = 4
E_PER_GROUP = 4
N_EXPERTS = N_GROUPS * E_PER_GROUP
D_EXPERT = 256
EPS = 1e-6
NEG_INF = -1e30
LAM_INIT = 0.8 - 0.6 * math.exp(-0.3 * 0)
LOG2E = math.log2(math.e)
LN2 = math.log(2.0)
QK_SCALE = HEAD_DIM ** -0.5

LANES = 128
VMEM_LIMIT = 56 * 1024 * 1024

TM_PROJ = 512
TQ_DIFF = 512
TK_DIFF = 512
TM_MERGE = 256
TM_MOE = 512
ROUTE_LANES = 128
EXP_LANE0 = N_GROUPS


def _const_spec(shape):
    nd = len(shape)
    return pl.BlockSpec(shape, lambda *_: (0,) * nd, pipeline_mode=pl.Buffered(1))


def _params(sem):
    return pltpu.CompilerParams(dimension_semantics=sem, vmem_limit_bytes=VMEM_LIMIT)


def _rel_bucket(rel):
    half = N_BUCKETS // 2
    max_exact = half // 2
    ret = (rel > 0).astype(jnp.int32) * half
    n = jnp.abs(rel)
    nf = jnp.maximum(n, 1).astype(F32)
    large = max_exact + (jnp.log(nf / max_exact) / math.log(MAX_DISTANCE / max_exact)
                         * (half - max_exact)).astype(jnp.int32)
    large = jnp.minimum(large, half - 1)
    return ret + jnp.where(n < max_exact, n, large)


def _chunks(total, width):
    out, c = [], 0
    while c < total:
        w = min(width, total - c)
        out.append((c, w))
        c += w
    return out


def _norm_proj_kernel(x_ref, g_ref, wa_ref, wg_ref, oa_ref, og_ref):
    x = x_ref[...]
    ms = jnp.mean(x * x, axis=-1, keepdims=True)
    h = (x * lax.rsqrt(ms + EPS) * g_ref[...]).astype(BF16)
    for c0, cw in _chunks(ATTN_COLS, 512):
        oa_ref[:, c0:c0 + cw] = jnp.dot(h, wa_ref[:, c0:c0 + cw],
                                        preferred_element_type=F32).astype(BF16)
    for c0, cw in _chunks(GATE_COLS, 512):
        og_ref[:, c0:c0 + cw] = jnp.dot(h, wg_ref[:, c0:c0 + cw],
                                        preferred_element_type=F32).astype(BF16)


def _norm_proj(x2, g, w_attn, w_gate):
    n = x2.shape[0]
    tm = TM_PROJ
    return pl.pallas_call(
        _norm_proj_kernel,
        grid=(n // tm,),
        in_specs=[
            pl.BlockSpec((tm, D_MODEL), lambda i: (i, 0)),
            _const_spec((1, D_MODEL)),
            _const_spec((D_MODEL, ATTN_COLS)),
            _const_spec((D_MODEL, GATE_COLS)),
        ],
        out_specs=[
            pl.BlockSpec((tm, ATTN_COLS), lambda i: (i, 0)),
            pl.BlockSpec((tm, GATE_COLS), lambda i: (i, 0)),
        ],
        out_shape=[
            jax.ShapeDtypeStruct((n, ATTN_COLS), BF16),
            jax.ShapeDtypeStruct((n, GATE_COLS), BF16),
        ],
        compiler_params=_params(("parallel",)),
        name="norm_proj",
    )(x2, g, w_attn, w_gate)


def _diff_tile_range(tq, tk):
    u = math.gcd(tq, tk)
    lo_in = -((tk + MAX_DISTANCE - 1) // u)
    hi_in = (tq + MAX_DISTANCE - 1) // u
    return u, lo_in - 1, hi_in + 1


def _diff_bias_tiles(rel_bias, tq, tk):
    u, lo, hi = _diff_tile_range(tq, tk)
    i = jnp.arange(tq)[:, None]
    j = jnp.arange(tk)[None, :]
    tiles = []
    for du in range(lo, hi + 1):
        rel = du * u + j - i
        tiles.append(rel_bias[:, :H_DIFF][_rel_bucket(rel)])
    t = jnp.stack(tiles, axis=0).astype(F32) * LOG2E
    return t.transpose(3, 0, 1, 2)


def _diff_attn_kernel(lam_ref, q_ref, k_ref, v_ref, bias_ref, g_ref, o_ref, *, tq, tk, nkb):
    qi = pl.program_id(2)
    u, lo, hi = _diff_tile_range(tq, tk)
    q = q_ref[0]
    lane = lax.broadcasted_iota(jnp.int32, (tq, LANES), 1)
    first = lane < HEAD_DIM
    zero = jnp.zeros_like(q)
    q2 = jnp.concatenate([jnp.where(first, q, zero), jnp.where(first, zero, q)], axis=0)

    def body(kb, carry):
        m, l, acc = carry
        ks = pl.multiple_of(kb * tk, tk)
        k = k_ref[0, pl.ds(ks, tk), :]
        v = v_ref[0, pl.ds(ks, tk), :]
        s = lax.dot_general(q2, k, (((1,), (1,)), ((), ())), preferred_element_type=F32)
        t = jnp.clip(kb * (tk // u) - qi * (tq // u), lo, hi) - lo
        s = s.reshape(2, tq, tk) + bias_ref[0, t][None]
        m_new = jnp.maximum(m, jnp.max(s, axis=-1, keepdims=True))
        p = jnp.exp2(s - m_new)
        alpha = jnp.exp2(m - m_new)
        l = alpha * l + jnp.sum(p, axis=-1, keepdims=True)
        pv = jnp.dot(p.astype(BF16).reshape(2 * tq, tk), v, preferred_element_type=F32)
        acc = alpha * acc + pv.reshape(2, tq, LANES)
        return m_new, l, acc

    m0 = jnp.full((2, tq, 1), -jnp.inf, F32)
    l0 = jnp.zeros((2, tq, 1), F32)
    a0 = jnp.zeros((2, tq, LANES), F32)
    m, l, acc = lax.fori_loop(0, nkb, body, (m0, l0, a0))

    lp = lam_ref[...]
    lam = (jnp.exp(jnp.sum(lp[0:1] * lp[1:2], axis=-1, keepdims=True))
           - jnp.exp(jnp.sum(lp[2:3] * lp[3:4], axis=-1, keepdims=True)) + LAM_INIT)
    o = acc[0] / l[0] - lam * (acc[1] / l[1])
    ms = jnp.mean(o * o, axis=-1, keepdims=True)
    y = o * lax.rsqrt(ms + EPS) * g_ref[...] * (1.0 - LAM_INIT)
    o_ref[0] = y.astype(BF16)


def _diff_attn(proj3, bias_tiles, lam_params, subln_g):
    b, s, _ = proj3.shape
    tq, tk = min(TQ_DIFF, s), min(TK_DIFF, s)
    nt = bias_tiles.shape[1]
    kern = functools.partial(_diff_attn_kernel, tq=tq, tk=tk, nkb=s // tk)
    return pl.pallas_call(
        kern,
        grid=(b, H_DIFF, s // tq),
        in_specs=[
            _const_spec((4, HEAD_DIM)),
            pl.BlockSpec((1, tq, LANES), lambda bi, h, qi: (bi, qi, h)),
            pl.BlockSpec((1, s, LANES), lambda bi, h, qi: (bi, 0, H_DIFF + h)),
            pl.BlockSpec((1, s, LANES), lambda bi, h, qi: (bi, 0, 2 * H_DIFF + h)),
            pl.BlockSpec((1, nt, tq, tk), lambda bi, h, qi: (h, 0, 0, 0)),
            _const_spec((1, 2 * HEAD_DIM)),
        ],
        out_specs=pl.BlockSpec((1, tq, LANES), lambda bi, h, qi: (bi, qi, h)),
        out_shape=jax.ShapeDtypeStruct((b, s, DIFF_V), BF16),
        compiler_params=_params(("parallel", "parallel", "arbitrary")),
        name="diff_attn",
    )(lam_params, proj3, proj3, proj3, bias_tiles, subln_g)


def _dil_geometry(s, window, dilation):
    l = s // dilation
    w = window // (2 * dilation)
    kw = min(Q_BLOCK + 2 * w, l)
    nb = l // Q_BLOCK
    return l, w, kw, nb


def _dil_bias_tiles(rel_bias, g, s, window, dilation):
    l, w, kw, nb = _dil_geometry(s, window, dilation)
    i = jnp.arange(Q_BLOCK)[:, None]
    j = jnp.arange(kw)[None, :]
    c0 = H_DIFF + g * H_DIL
    tiles = []
    for n in (0, min(1, nb - 1), nb - 1):
        l0 = n * Q_BLOCK
        ws = min(max(l0 - w, 0), l - kw)
        off = (ws + j) - (l0 + i)
        band = jnp.abs(off) <= w
        bias = rel_bias[:, c0:c0 + H_DIL][_rel_bucket(off * dilation)].astype(F32) * LOG2E
        tiles.append(jnp.where(band[..., None], bias, NEG_INF).transpose(2, 0, 1))
    return jnp.stack(tiles, axis=0)


def _dil_attn_kernel(q_ref, k_ref, v_ref, bias_ref, o_ref, lse_ref, *, l, w, kw, nb):
    lane = lax.broadcasted_iota(jnp.int32, (Q_BLOCK, LANES), 1)
    first = lane < HEAD_DIM

    def block(n, carry):
        l0 = pl.multiple_of(n * Q_BLOCK, Q_BLOCK)
        ws = pl.multiple_of(jnp.clip(l0 - w, 0, l - kw), 64)
        t = jnp.where(n == 0, 0, jnp.where(n == nb - 1, 2, 1))
        for pair in range(H_DIL // 2):
            cs = slice(pair * LANES, (pair + 1) * LANES)
            qp = q_ref[0, pl.ds(l0, Q_BLOCK), cs]
            kp = k_ref[0, pl.ds(ws, kw), cs]
            vp = v_ref[0, pl.ds(ws, kw), cs]
            zero = jnp.zeros_like(qp)
            outs, lses = [], []
            for hh in range(2):
                qm = jnp.where(first, qp, zero) if hh == 0 else jnp.where(first, zero, qp)
                s = lax.dot_general(qm, kp, (((1,), (1,)), ((), ())), preferred_element_type=F32)
                s = s + bias_ref[t, pair * 2 + hh]
                mx = jnp.max(s, axis=-1, keepdims=True)
                p = jnp.exp2(s - mx)
                den = jnp.sum(p, axis=-1, keepdims=True)
                pv = jnp.dot(p.astype(BF16), vp, preferred_element_type=F32)
                outs.append(pv / den)
                lses.append(mx * LN2 + jnp.log(den))
            o_ref[0, pl.ds(l0, Q_BLOCK), cs] = jnp.where(first, outs[0], outs[1]).astype(BF16)
            lse_ref[0, pl.ds(l0, Q_BLOCK), cs] = jnp.where(first, lses[0], lses[1])
        return carry

    lax.fori_loop(0, nb, block, 0)


def _dil_attn(proj3, bias_tiles, g, window, dilation):
    b, s, _ = proj3.shape
    l, w, kw, nb = _dil_geometry(s, window, dilation)
    d = dilation
    cw = H_DIL * HEAD_DIM
    nblk = ATTN_COLS // cw
    pv = proj3.reshape(b, l, d * ATTN_COLS)
    qb0 = (2 * DIFF_QK + DIFF_V) // cw + g
    kb0 = qb0 + N_DIL
    vb0 = kb0 + N_DIL
    kern = functools.partial(_dil_attn_kernel, l=l, w=w, kw=kw, nb=nb)
    o, lse = pl.pallas_call(
        kern,
        grid=(b, d),
        in_specs=[
            pl.BlockSpec((1, l, cw), lambda bi, r: (bi, 0, r * nblk + qb0)),
            pl.BlockSpec((1, l, cw), lambda bi, r: (bi, 0, r * nblk + kb0)),
            pl.BlockSpec((1, l, cw), lambda bi, r: (bi, 0, r * nblk + vb0)),
            _const_spec((3, H_DIL, Q_BLOCK, kw)),
        ],
        out_specs=[
            pl.BlockSpec((1, l, cw), lambda bi, r: (bi, 0, r)),
            pl.BlockSpec((1, l, cw), lambda bi, r: (bi, 0, r)),
        ],
        out_shape=[
            jax.ShapeDtypeStruct((b, l, d * cw), BF16),
            jax.ShapeDtypeStruct((b, l, d * cw), F32),
        ],
        compiler_params=_params(("parallel", "parallel")),
        name=f"dil_attn_g{g}",
    )(pv, pv, pv, bias_tiles)
    return o.reshape(b * s, cw), lse.reshape(b * s, cw)


def _merge_route_kernel(x_ref, oa_ref, o0_ref, o1_ref, o2_ref, l0_ref, l1_ref, l2_ref, gt_ref,
                        wa_ref, wb_ref, wo_ref, nf_ref, wr_ref, x1_ref, h2_ref, dw_ref):
    l0, l1, l2 = l0_ref[...], l1_ref[...], l2_ref[...]
    mx = jnp.maximum(jnp.maximum(l0, l1), l2)
    e0, e1, e2 = jnp.exp(l0 - mx), jnp.exp(l1 - mx), jnp.exp(l2 - mx)
    ob = (e0 * o0_ref[...].astype(F32) + e1 * o1_ref[...].astype(F32)
          + e2 * o2_ref[...].astype(F32)) / (e0 + e1 + e2)
    pa = jnp.dot(oa_ref[...], wa_ref[...], preferred_element_type=F32)
    pb = jnp.dot(ob.astype(BF16), wb_ref[...], preferred_element_type=F32)
    ga = jax.nn.sigmoid(gt_ref[:, :D_MODEL].astype(F32))
    gb = jax.nn.sigmoid(gt_ref[:, D_MODEL:].astype(F32))
    merged = (ga * pa + gb * pb).astype(BF16)
    x1 = x_ref[...] + jnp.dot(merged, wo_ref[...], preferred_element_type=F32)
    x1_ref[...] = x1
    ms = jnp.mean(x1 * x1, axis=-1, keepdims=True)
    h2 = (x1 * lax.rsqrt(ms + EPS) * nf_ref[...]).astype(BF16)
    h2_ref[...] = h2

    lg = jnp.dot(h2, wr_ref[...], preferred_element_type=F32)
    lane = lax.broadcasted_iota(jnp.int32, lg.shape, 1)
    big = jnp.int32(ROUTE_LANES)
    gl = jnp.where(lane < N_GROUPS, lg, -jnp.inf)
    gmax = jnp.max(gl, axis=-1, keepdims=True)
    gi = jnp.min(jnp.where(gl == gmax, lane, big), axis=-1, keepdims=True)
    gw = 1.0 / jnp.sum(jnp.exp(gl - gmax), axis=-1, keepdims=True)
    e_lo = EXP_LANE0 + E_PER_GROUP * gi
    es = jnp.where((lane >= e_lo) & (lane < e_lo + E_PER_GROUP), lg, -jnp.inf)
    v1 = jnp.max(es, axis=-1, keepdims=True)
    i1 = jnp.min(jnp.where(es == v1, lane, big), axis=-1, keepdims=True)
    es2 = jnp.where(lane == i1, -jnp.inf, es)
    v2 = jnp.max(es2, axis=-1, keepdims=True)
    i2 = jnp.min(jnp.where(es2 == v2, lane, big), axis=-1, keepdims=True)
    t = jnp.exp(v2 - v1)
    w1 = gw / (1.0 + t)
    w2 = w1 * t
    dw_ref[...] = jnp.where(lane == i1, w1, 0.0) + jnp.where(lane == i2, w2, 0.0)


def _merge_route(x2, oa, dil, gates, wa, wb, wo, nf, wr):
    n = x2.shape[0]
    tm = TM_MERGE
    cw = H_DIL * HEAD_DIM
    row = lambda i: (i, 0)
    (o0, s0), (o1, s1), (o2, s2) = dil
    return pl.pallas_call(
        _merge_route_kernel,
        grid=(n // tm,),
        in_specs=[
            pl.BlockSpec((tm, D_MODEL), row),
            pl.BlockSpec((tm, DIFF_V), row),
            pl.BlockSpec((tm, cw), row), pl.BlockSpec((tm, cw), row), pl.BlockSpec((tm, cw), row),
            pl.BlockSpec((tm, cw), row), pl.BlockSpec((tm, cw), row), pl.BlockSpec((tm, cw), row),
            pl.BlockSpec((tm, GATE_COLS), row),
            _const_spec((DIFF_V, D_MODEL)),
            _const_spec((cw, D_MODEL)),
            _const_spec((D_MODEL, D_MODEL)),
            _const_spec((1, D_MODEL)),
            _const_spec((D_MODEL, ROUTE_LANES)),
        ],
        out_specs=[
            pl.BlockSpec((tm, D_MODEL), row),
            pl.BlockSpec((tm, D_MODEL), row),
            pl.BlockSpec((tm, ROUTE_LANES), row),
        ],
        out_shape=[
            jax.ShapeDtypeStruct((n, D_MODEL), F32),
            jax.ShapeDtypeStruct((n, D_MODEL), BF16),
            jax.ShapeDtypeStruct((n, ROUTE_LANES), F32),
        ],
        compiler_params=_params(("parallel",)),
        name="merge_route",
    )(x2, oa, o0, o1, o2, s0, s1, s2, gates, wa, wb, wo, nf, wr)


def _moe_final_kernel(x1_ref, h2_ref, dw_ref, wg_ref, wu_ref, wd_ref, nf_ref, y_ref, he_ref):
    h2 = h2_ref[...]
    dw = dw_ref[...]
    for e in range(N_EXPERTS):
        gp = jnp.dot(h2, wg_ref[e], preferred_element_type=F32)
        up = jnp.dot(h2, wu_ref[e], preferred_element_type=F32)
        we = dw[:, EXP_LANE0 + e:EXP_LANE0 + e + 1]
        he_ref[:, e * D_EXPERT:(e + 1) * D_EXPERT] = (jax.nn.silu(gp) * up * we).astype(BF16)
    moe = jnp.dot(he_ref[...], wd_ref[...], preferred_element_type=F32)
    x2 = x1_ref[...] + moe
    ms = jnp.mean(x2 * x2, axis=-1, keepdims=True)
    y_ref[...] = x2 * lax.rsqrt(ms + EPS) * nf_ref[...]


def _moe_final(x1, h2, dw, wg, wu, wd, nf):
    n = x1.shape[0]
    tm = TM_MOE
    row = lambda i: (i, 0)
    return pl.pallas_call(
        _moe_final_kernel,
        grid=(n // tm,),
        in_specs=[
            pl.BlockSpec((tm, D_MODEL), row),
            pl.BlockSpec((tm, D_MODEL), row),
            pl.BlockSpec((tm, ROUTE_LANES), row),
            _const_spec((N_EXPERTS, D_MODEL, D_EXPERT)),
            _const_spec((N_EXPERTS, D_MODEL, D_EXPERT)),
            _const_spec((N_EXPERTS * D_EXPERT, D_MODEL)),
            _const_spec((1, D_MODEL)),
        ],
        out_specs=pl.BlockSpec((tm, D_MODEL), row),
        out_shape=jax.ShapeDtypeStruct((n, D_MODEL), F32),
        scratch_shapes=[pltpu.VMEM((tm, N_EXPERTS * D_EXPERT), BF16)],
        compiler_params=_params(("parallel",)),
        name="moe_final",
    )(x1, h2, dw, wg, wu, wd, nf)


def _prepare_weights(rel_bias, norm_mix, w_in, lam_q1, lam_k1, lam_q2, lam_k2, subln_g,
                     w_branch_a, w_branch_b, w_out, norm_ffn, w_group, w_router,
                     w_exp_gate, w_exp_up, w_exp_down, norm_final):
    w = w_in[0]
    c = QK_SCALE * LOG2E
    col = jnp.arange(ATTN_COLS)
    qa_end, qb_start = DIFF_QK, 2 * DIFF_QK + DIFF_V
    is_q = (col < qa_end) | ((col >= qb_start) & (col < qb_start + DIL_W))
    w_attn = (w[:, :ATTN_COLS] * jnp.where(is_q, c, 1.0)[None, :]).astype(BF16)
    w_gate = w[:, ATTN_COLS:].astype(BF16)
    wr = jnp.concatenate([w_group[0], w_router[0].transpose(1, 0, 2).reshape(D_MODEL, N_EXPERTS)], axis=1)
    wr = jnp.pad(wr, ((0, 0), (0, ROUTE_LANES - wr.shape[1]))).astype(BF16)
    return dict(
        rel_bias=rel_bias.astype(F32),
        norm_mix=norm_mix[0][None].astype(F32),
        w_attn=w_attn, w_gate=w_gate,
        lam=jnp.stack([lam_q1[0], lam_k1[0], lam_q2[0], lam_k2[0]]).astype(F32),
        subln_g=subln_g[0][None].astype(F32),
        wa=w_branch_a[0].astype(BF16), wb=w_branch_b[0].astype(BF16), wo=w_out[0].astype(BF16),
        norm_ffn=norm_ffn[0][None].astype(F32), wr=wr,
        wg=w_exp_gate[0].astype(BF16), wu=w_exp_up[0].astype(BF16),
        wd=w_exp_down[0].reshape(N_EXPERTS * D_EXPERT, D_MODEL).astype(BF16),
        norm_final=norm_final[None].astype(F32),
    )


def _encoder(x, p):
    b, s, d = x.shape
    n = b * s
    x2 = x.reshape(n, d)
    proj, gates = _norm_proj(x2, p["norm_mix"], p["w_attn"], p["w_gate"])
    proj3 = proj.reshape(b, s, ATTN_COLS)
    tq, tk = min(TQ_DIFF, s), min(TK_DIFF, s)
    oa = _diff_attn(proj3, _diff_bias_tiles(p["rel_bias"], tq, tk), p["lam"], p["subln_g"])
    dil = []
    for g, (win, dilation) in enumerate(DIL_CONFIGS):
        tiles = _dil_bias_tiles(p["rel_bias"], g, s, win, dilation)
        dil.append(_dil_attn(proj3, tiles, g, win, dilation))
    x1, h2, dw = _merge_route(x2, oa.reshape(n, DIFF_V), dil, gates,
                              p["wa"], p["wb"], p["wo"], p["norm_ffn"], p["wr"])
    y = _moe_final(x1, h2, dw, p["wg"], p["wu"], p["wd"], p["norm_final"])
    return y.reshape(b, s, d)


def kernel(x_prompt, x_sample, rel_bias, norm_mix, w_in, lam_q1, lam_k1, lam_q2, lam_k2, subln_g,
           w_branch_a, w_branch_b, w_out, norm_ffn, w_group, w_router, w_exp_gate, w_exp_up,
           w_exp_down, norm_final):
    p = _prepare_weights(rel_bias, norm_mix, w_in, lam_q1, lam_k1, lam_q2, lam_k2, subln_g,
                         w_branch_a, w_branch_b, w_out, norm_ffn, w_group, w_router,
                         w_exp_gate, w_exp_up, w_exp_down, norm_final)
    return (_encoder(x_prompt, p), _encoder(x_sample, p))
```

```python
import functools
import math

import jax
import jax.numpy as jnp
from jax import lax
from jax.experimental import pallas as pl
from jax.experimental.pallas import tpu as pltpu

F32 = jnp.float32
BF16 = jnp.bfloat16

D_MODEL = 1024
HEAD_DIM = 64
H_DIFF = 4
H_DIL = 4
DIL_CONFIGS = ((128, 1), (512, 4), (2048, 16))
N_DIL = len(DIL_CONFIGS)
DIFF_QK = H_DIFF * 2 * HEAD_DIM
DIFF_V = H_DIFF * 2 * HEAD_DIM
DIL_W = N_DIL * H_DIL * HEAD_DIM
GRP_W = H_DIL * HEAD_DIM
GRP_COLS = 3 * GRP_W
NAT_COLS = 2 * DIFF_QK + DIFF_V + GRP_COLS
GATE_COLS = 2 * D_MODEL
Q_BLOCK = 128
N_BUCKETS = 32
MAX_DISTANCE = 128
N_GROUPS = 4
E_PER_GROUP = 4
N_EXPERTS = N_GROUPS * E_PER_GROUP
D_EXPERT = 256
EPS = 1e-6
NEG_INF = -1e30
LAM_INIT = 0.8 - 0.6 * math.exp(-0.3 * 0)
LOG2E = math.log2(math.e)
LN2 = math.log(2.0)
QK_SCALE = HEAD_DIM ** -0.5

LANES = 128
VMEM_LIMIT = 56 * 1024 * 1024

TM_PROJ = 512
TQ_DIFF = 512
TK_DIFF = 512
TM_MERGE = 256
TM_MOE = 512
ROUTE_LANES = 128
EXP_LANE0 = N_GROUPS


def _const_spec(shape):
    nd = len(shape)
    return pl.BlockSpec(shape, lambda *_: (0,) * nd, pipeline_mode=pl.Buffered(1))


def _params(sem):
    return pltpu.CompilerParams(dimension_semantics=sem, vmem_limit_bytes=VMEM_LIMIT)


def _rel_bucket(rel):
    half = N_BUCKETS // 2
    max_exact = half // 2
    ret = (rel > 0).astype(jnp.int32) * half
    n = jnp.abs(rel)
    nf = jnp.maximum(n, 1).astype(F32)
    large = max_exact + (jnp.log(nf / max_exact) / math.log(MAX_DISTANCE / max_exact)
                         * (half - max_exact)).astype(jnp.int32)
    large = jnp.minimum(large, half - 1)
    return ret + jnp.where(n < max_exact, n, large)


def _bias_of_rel(table, rel):
    onehot = _rel_bucket(rel)[..., None] == jnp.arange(N_BUCKETS)
    vals = jnp.sum(jnp.where(onehot[..., None], table.astype(F32), 0.0), axis=-2)
    return jnp.moveaxis(vals, -1, 0)


def _toeplitz(vec, rows, cols):
    w = rows + cols - 1
    lead = vec.shape[:-1]
    a = jnp.pad(vec, [(0, 0)] * len(lead) + [(0, 1)])
    a = jnp.broadcast_to(a[..., None, :], lead + (rows, w + 1)).reshape(lead + (rows * (w + 1),))
    return a[..., rows - 1:rows - 1 + rows * w].reshape(lead + (rows, w))[..., :cols]


def _chunks(total, width):
    out, c = [], 0
    while c < total:
        w = min(width, total - c)
        out.append((c, w))
        c += w
    return out


def _norm_proj_kernel(x_ref, g_ref, wn_ref, wg_ref, w1_ref, w2_ref,
                      on_ref, og_ref, o1_ref, o2_ref, hs_ref, *, tm):
    x = x_ref[0]
    ms = jnp.mean(x * x, axis=-1, keepdims=True)
    hf = x * lax.rsqrt(ms + EPS) * g_ref[...]
    for c in range(D_MODEL // LANES):
        hs_ref[c] = hf[:, c * LANES:(c + 1) * LANES]
    h = hf.astype(BF16)
    for c0, cw in _chunks(NAT_COLS, 512):
        on_ref[0, :, c0:c0 + cw] = jnp.dot(h, wn_ref[:, c0:c0 + cw],
                                           preferred_element_type=F32).astype(BF16)
    for c0, cw in _chunks(GATE_COLS, 512):
        og_ref[0, :, c0:c0 + cw] = jnp.dot(h, wg_ref[:, c0:c0 + cw],
                                           preferred_element_type=F32).astype(BF16)
    for w_ref, o_ref, d in ((w1_ref, o1_ref, DIL_CONFIGS[1][1]), (w2_ref, o2_ref, DIL_CONFIGS[2][1])):
        rows = tm // d
        hp = jnp.concatenate(
            [jnp.concatenate([hs_ref[c, pl.ds(r, rows, stride=d), :] for r in range(d)], axis=0)
             for c in range(D_MODEL // LANES)], axis=1).astype(BF16)
        for c0, cw in _chunks(GRP_COLS, 256):
            res = jnp.dot(hp, w_ref[:, c0:c0 + cw], preferred_element_type=F32).astype(BF16)
            o_ref[0, :, :, c0:c0 + cw] = res.reshape(d, rows, cw)


def _norm_proj(x, g, w_nat, w_gate, w_d1, w_d2):
    b, s, _ = x.shape
    tm = TM_PROJ
    d1, d2 = DIL_CONFIGS[1][1], DIL_CONFIGS[2][1]
    return pl.pallas_call(
        functools.partial(_norm_proj_kernel, tm=tm),
        grid=(b, s // tm),
        in_specs=[
            pl.BlockSpec((1, tm, D_MODEL), lambda bi, i: (bi, i, 0)),
            _const_spec((1, D_MODEL)),
            _const_spec((D_MODEL, NAT_COLS)),
            _const_spec((D_MODEL, GATE_COLS)),
            _const_spec((D_MODEL, GRP_COLS)),
            _const_spec((D_MODEL, GRP_COLS)),
        ],
        out_specs=[
            pl.BlockSpec((1, tm, NAT_COLS), lambda bi, i: (bi, i, 0)),
            pl.BlockSpec((1, tm, GATE_COLS), lambda bi, i: (bi, i, 0)),
            pl.BlockSpec((1, d1, tm // d1, GRP_COLS), lambda bi, i: (bi, 0, i, 0)),
            pl.BlockSpec((1, d2, tm // d2, GRP_COLS), lambda bi, i: (bi, 0, i, 0)),
        ],
        out_shape=[
            jax.ShapeDtypeStruct((b, s, NAT_COLS), BF16),
            jax.ShapeDtypeStruct((b, s, GATE_COLS), BF16),
            jax.ShapeDtypeStruct((b, d1, s // d1, GRP_COLS), BF16),
            jax.ShapeDtypeStruct((b, d2, s // d2, GRP_COLS), BF16),
        ],
        scratch_shapes=[pltpu.VMEM((D_MODEL // LANES, tm, LANES), F32)],
        compiler_params=_params(("parallel", "parallel")),
        name="norm_proj",
    )(x, g, w_nat, w_gate, w_d1, w_d2)


def _diff_tile_range(tq, tk):
    u = math.gcd(tq, tk)
    lo_in = -((tk + MAX_DISTANCE - 1) // u)
    hi_in = (tq + MAX_DISTANCE - 1) // u
    return u, lo_in - 1, hi_in + 1


def _diff_bias_tiles(rel_bias, tq, tk):
    u, lo, hi = _diff_tile_range(tq, tk)
    c = jnp.arange(tq + tk - 1) - (tq - 1)
    rel = (jnp.arange(lo, hi + 1) * u)[:, None] + c[None, :]
    vec = _bias_of_rel(rel_bias[:, :H_DIFF], rel) * LOG2E
    return _toeplitz(vec, tq, tk)


def _diff_attn_kernel(lam_ref, q_ref, k_ref, v_ref, bias_ref, g_ref, o_ref, *, tq, tk, nkb):
    qi = pl.program_id(2)
    u, lo, hi = _diff_tile_range(tq, tk)
    q = q_ref[0]
    lane = lax.broadcasted_iota(jnp.int32, (tq, LANES), 1)
    first = lane < HEAD_DIM
    zero = jnp.zeros_like(q)
    q2 = jnp.concatenate([jnp.where(first, q, zero), jnp.where(first, zero, q)], axis=0)

    def body(kb, carry):
        m, l, acc = carry
        ks = pl.multiple_of(kb * tk, tk)
        k = k_ref[0, pl.ds(ks, tk), :]
        v = v_ref[0, pl.ds(ks, tk), :]
        s = lax.dot_general(q2, k, (((1,), (1,)), ((), ())), preferred_element_type=F32)
        t = jnp.clip(kb * (tk // u) - qi * (tq // u), lo, hi) - lo
        s = s.reshape(2, tq, tk) + bias_ref[0, t][None]
        m_new = jnp.maximum(m, jnp.max(s, axis=-1, keepdims=True))
        p = jnp.exp2(s - m_new)
        alpha = jnp.exp2(m - m_new)
        l = alpha * l + jnp.sum(p, axis=-1, keepdims=True)
        pv = jnp.dot(p.astype(BF16).reshape(2 * tq, tk), v, preferred_element_type=F32)
        acc = alpha * acc + pv.reshape(2, tq, LANES)
        return m_new, l, acc

    m0 = jnp.full((2, tq, 1), -jnp.inf, F32)
    l0 = jnp.zeros((2, tq, 1), F32)
    a0 = jnp.zeros((2, tq, LANES), F32)
    m, l, acc = lax.fori_loop(0, nkb, body, (m0, l0, a0))

    lp = lam_ref[...]
    lam = (jnp.exp(jnp.sum(lp[0:1] * lp[1:2], axis=-1, keepdims=True))
           - jnp.exp(jnp.sum(lp[2:3] * lp[3:4], axis=-1, keepdims=True)) + LAM_INIT)
    o = acc[0] / l[0] - lam * (acc[1] / l[1])
    ms = jnp.mean(o * o, axis=-1, keepdims=True)
    y = o * lax.rsqrt(ms + EPS) * g_ref[...] * (1.0 - LAM_INIT)
    o_ref[0] = y.astype(BF16)


def _diff_attn(nat, bias_tiles, lam_params, subln_g):
    b, s, _ = nat.shape
    tq, tk = min(TQ_DIFF, s), min(TK_DIFF, s)
    nt = bias_tiles.shape[1]
    kern = functools.partial(_diff_attn_kernel, tq=tq, tk=tk, nkb=s // tk)
    return pl.pallas_call(
        kern,
        grid=(b, H_DIFF, s // tq),
        in_specs=[
            _const_spec((4, HEAD_DIM)),
            pl.BlockSpec((1, tq, LANES), lambda bi, h, qi: (bi, qi, h)),
            pl.BlockSpec((1, s, LANES), lambda bi, h, qi: (bi, 0, H_DIFF + h)),
            pl.BlockSpec((1, s, LANES), lambda bi, h, qi: (bi, 0, 2 * H_DIFF + h)),
            pl.BlockSpec((1, nt, tq, tk), lambda bi, h, qi: (h, 0, 0, 0)),
            _const_spec((1, 2 * HEAD_DIM)),
        ],
        out_specs=pl.BlockSpec((1, tq, LANES), lambda bi, h, qi: (bi, qi, h)),
        out_shape=jax.ShapeDtypeStruct((b, s, DIFF_V), BF16),
        compiler_params=_params(("parallel", "parallel", "arbitrary")),
        name="diff_attn",
    )(lam_params, nat, nat, nat, bias_tiles, subln_g)


def _dil_geometry(s, window, dilation):
    l = s // dilation
    w = window // (2 * dilation)
    kw = min(Q_BLOCK + 2 * w, l)
    nb = l // Q_BLOCK
    return l, w, kw, nb


def _dil_bias_tiles(rel_bias, g, s, window, dilation):
    l, w, kw, nb = _dil_geometry(s, window, dilation)
    c0 = H_DIFF + g * H_DIL
    c = jnp.arange(Q_BLOCK + kw - 1) - (Q_BLOCK - 1)
    shifts = []
    for n in (0, min(1, nb - 1), nb - 1):
        l0 = n * Q_BLOCK
        shifts.append(min(max(l0 - w, 0), l - kw) - l0)
    off = jnp.asarray(shifts)[:, None] + c[None, :]
    vec = _bias_of_rel(rel_bias[:, c0:c0 + H_DIL], off * dilation) * LOG2E
    vec = jnp.where((jnp.abs(off) <= w)[None], vec, NEG_INF)
    return _toeplitz(vec, Q_BLOCK, kw).transpose(1, 0, 2, 3)


def _dil_attn_kernel(q_ref, k_ref, v_ref, bias_ref, o_ref, lse_ref, *, l, w, kw, nb):
    lane = lax.broadcasted_iota(jnp.int32, (Q_BLOCK, LANES), 1)
    first = lane < HEAD_DIM

    def block(n, carry):
        l0 = pl.multiple_of(n * Q_BLOCK, Q_BLOCK)
        ws = pl.multiple_of(jnp.clip(l0 - w, 0, l - kw), 64)
        t = jnp.where(n == 0, 0, jnp.where(n == nb - 1, 2, 1))
        for pair in range(H_DIL // 2):
            cs = slice(pair * LANES, (pair + 1) * LANES)
            qp = q_ref[0, 0, pl.ds(l0, Q_BLOCK), cs]
            kp = k_ref[0, 0, pl.ds(ws, kw), cs]
            vp = v_ref[0, 0, pl.ds(ws, kw), cs]
            zero = jnp.zeros_like(qp)
            outs, lses = [], []
            for hh in range(2):
                qm = jnp.where(first, qp, zero) if hh == 0 else jnp.where(first, zero, qp)
                s = lax.dot_general(qm, kp, (((1,), (1,)), ((), ())), preferred_element_type=F32)
                s = s + bias_ref[t, pair * 2 + hh]
                mx = jnp.max(s, axis=-1, keepdims=True)
                p = jnp.exp2(s - mx)
                den = jnp.sum(p, axis=-1, keepdims=True)
                pv = jnp.dot(p.astype(BF16), vp, preferred_element_type=F32)
                outs.append(pv / den)
                lses.append(mx * LN2 + jnp.log(den))
            o_ref[0, 0, pl.ds(l0, Q_BLOCK), cs] = jnp.where(first, outs[0], outs[1]).astype(BF16)
            lse_ref[0, 0, pl.ds(l0, Q_BLOCK), cs] = jnp.where(first, lses[0], lses[1])
        return carry

    lax.fori_loop(0, nb, block, 0)


def _dil_attn(arr, col0, bias_tiles, g, window, dilation):
    b, d, l, _ = arr.shape
    _, w, kw, nb = _dil_geometry(l * d, window, dilation)
    kern = functools.partial(_dil_attn_kernel, l=l, w=w, kw=kw, nb=nb)
    return pl.pallas_call(
        kern,
        grid=(b, d),
        in_specs=[
            pl.BlockSpec((1, 1, l, GRP_W), lambda bi, r: (bi, r, 0, col0)),
            pl.BlockSpec((1, 1, l, GRP_W), lambda bi, r: (bi, r, 0, col0 + 1)),
            pl.BlockSpec((1, 1, l, GRP_W), lambda bi, r: (bi, r, 0, col0 + 2)),
            _const_spec((3, H_DIL, Q_BLOCK, kw)),
        ],
        out_specs=[
            pl.BlockSpec((1, 1, l, GRP_W), lambda bi, r: (bi, r, 0, 0)),
            pl.BlockSpec((1, 1, l, GRP_W), lambda bi, r: (bi, r, 0, 0)),
        ],
        out_shape=[
            jax.ShapeDtypeStruct((b, d, l, GRP_W), BF16),
            jax.ShapeDtypeStruct((b, d, l, GRP_W), F32),
        ],
        compiler_params=_params(("parallel", "parallel")),
        name=f"dil_attn_g{g}",
    )(arr, arr, arr, bias_tiles)


def _merge_route_kernel(x_ref, oa_ref, o0_ref, l0_ref, o1_ref, l1_ref, o2_ref, l2_ref, gt_ref,
                        wa_ref, wb_ref, wo_ref, nf_ref, wr_ref, x1_ref, h2_ref, dw_ref,
                        so1_ref, sl1_ref, so2_ref, sl2_ref, *, tm):
    for src, dst, d in ((o1_ref, so1_ref, DIL_CONFIGS[1][1]), (l1_ref, sl1_ref, DIL_CONFIGS[1][1]),
                        (o2_ref, so2_ref, DIL_CONFIGS[2][1]), (l2_ref, sl2_ref, DIL_CONFIGS[2][1])):
        for r in range(d):
            for c in range(GRP_W // LANES):
                dst[c, pl.ds(r, tm // d, stride=d), :] = src[0, r, :, c * LANES:(c + 1) * LANES].astype(F32)
    unsplit = lambda ref: jnp.concatenate([ref[c] for c in range(GRP_W // LANES)], axis=1)
    l0, l1, l2 = l0_ref[0, 0], unsplit(sl1_ref), unsplit(sl2_ref)
    mx = jnp.maximum(jnp.maximum(l0, l1), l2)
    e0, e1, e2 = jnp.exp(l0 - mx), jnp.exp(l1 - mx), jnp.exp(l2 - mx)
    ob = (e0 * o0_ref[0, 0].astype(F32) + e1 * unsplit(so1_ref) + e2 * unsplit(so2_ref)) / (e0 + e1 + e2)
    pa = jnp.dot(oa_ref[0], wa_ref[...], preferred_element_type=F32)
    pb = jnp.dot(ob.astype(BF16), wb_ref[...], preferred_element_type=F32)
    ga = jax.nn.sigmoid(gt_ref[0, :, :D_MODEL].astype(F32))
    gb = jax.nn.sigmoid(gt_ref[0, :, D_MODEL:].astype(F32))
    merged = (ga * pa + gb * pb).astype(BF16)
    x1 = x_ref[0] + jnp.dot(merged, wo_ref[...], preferred_element_type=F32)
    x1_ref[0] = x1
    ms = jnp.mean(x1 * x1, axis=-1, keepdims=True)
    h2 = (x1 * lax.rsqrt(ms + EPS) * nf_ref[...]).astype(BF16)
    h2_ref[0] = h2

    lg = jnp.dot(h2, wr_ref[...], preferred_element_type=F32)
    lane = lax.broadcasted_iota(jnp.int32, lg.shape, 1)
    big = jnp.int32(ROUTE_LANES)
    gl = jnp.where(lane < N_GROUPS, lg, -jnp.inf)
    gmax = jnp.max(gl, axis=-1, keepdims=True)
    gi = jnp.min(jnp.where(gl == gmax, lane, big), axis=-1, keepdims=True)
    gw = 1.0 / jnp.sum(jnp.exp(gl - gmax), axis=-1, keepdims=True)
    e_lo = EXP_LANE0 + E_PER_GROUP * gi
    es = jnp.where((lane >= e_lo) & (lane < e_lo + E_PER_GROUP), lg, -jnp.inf)
    v1 = jnp.max(es, axis=-1, keepdims=True)
    i1 = jnp.min(jnp.where(es == v1, lane, big), axis=-1, keepdims=True)
    es2 = jnp.where(lane == i1, -jnp.inf, es)
    v2 = jnp.max(es2, axis=-1, keepdims=True)
    i2 = jnp.min(jnp.where(es2 == v2, lane, big), axis=-1, keepdims=True)
    t = jnp.exp(v2 - v1)
    w1 = gw / (1.0 + t)
    w2 = w1 * t
    dw_ref[0] = jnp.where(lane == i1, w1, 0.0) + jnp.where(lane == i2, w2, 0.0)


def _merge_route(x, oa, dil, gates, wa, wb, wo, nf, wr):
    b, s, _ = x.shape
    tm = TM_MERGE
    d1, d2 = DIL_CONFIGS[1][1], DIL_CONFIGS[2][1]
    tok = lambda bi, i: (bi, i, 0)
    cls = lambda bi, i: (bi, 0, i, 0)
    (o0, s0), (o1, s1), (o2, s2) = dil
    return pl.pallas_call(
        functools.partial(_merge_route_kernel, tm=tm),
        grid=(b, s // tm),
        in_specs=[
            pl.BlockSpec((1, tm, D_MODEL), tok),
            pl.BlockSpec((1, tm, DIFF_V), tok),
            pl.BlockSpec((1, 1, tm, GRP_W), cls), pl.BlockSpec((1, 1, tm, GRP_W), cls),
            pl.BlockSpec((1, d1, tm // d1, GRP_W), cls), pl.BlockSpec((1, d1, tm // d1, GRP_W), cls),
            pl.BlockSpec((1, d2, tm // d2, GRP_W), cls), pl.BlockSpec((1, d2, tm // d2, GRP_W), cls),
            pl.BlockSpec((1, tm, GATE_COLS), tok),
            _const_spec((DIFF_V, D_MODEL)),
            _const_spec((GRP_W, D_MODEL)),
            _const_spec((D_MODEL, D_MODEL)),
            _const_spec((1, D_MODEL)),
            _const_spec((D_MODEL, ROUTE_LANES)),
        ],
        out_specs=[
            pl.BlockSpec((1, tm, D_MODEL), tok),
            pl.BlockSpec((1, tm, D_MODEL), tok),
            pl.BlockSpec((1, tm, ROUTE_LANES), tok),
        ],
        out_shape=[
            jax.ShapeDtypeStruct((b, s, D_MODEL), F32),
            jax.ShapeDtypeStruct((b, s, D_MODEL), BF16),
            jax.ShapeDtypeStruct((b, s, ROUTE_LANES), F32),
        ],
        scratch_shapes=[pltpu.VMEM((GRP_W // LANES, tm, LANES), F32) for _ in range(4)],
        compiler_params=_params(("parallel", "parallel")),
        name="merge_route",
    )(x, oa, o0, s0, o1, s1, o2, s2, gates, wa, wb, wo, nf, wr)


def _moe_final_kernel(x1_ref, h2_ref, dw_ref, wg_ref, wu_ref, wd_ref, nf_ref, y_ref, he_ref):
    h2 = h2_ref[...]
    dw = dw_ref[...]
    for e in range(N_EXPERTS):
        gp = jnp.dot(h2, wg_ref[e], preferred_element_type=F32)
        up = jnp.dot(h2, wu_ref[e], preferred_element_type=F32)
        we = dw[:, EXP_LANE0 + e:EXP_LANE0 + e + 1]
        he_ref[:, e * D_EXPERT:(e + 1) * D_EXPERT] = (jax.nn.silu(gp) * up * we).astype(BF16)
    moe = jnp.dot(he_ref[...], wd_ref[...], preferred_element_type=F32)
    x2 = x1_ref[...] + moe
    ms = jnp.mean(x2 * x2, axis=-1, keepdims=True)
    y_ref[...] = x2 * lax.rsqrt(ms + EPS) * nf_ref[...]


def _moe_final(x1, h2, dw, wg, wu, wd, nf):
    n = x1.shape[0]
    tm = TM_MOE
    row = lambda i: (i, 0)
    return pl.pallas_call(
        _moe_final_kernel,
        grid=(n // tm,),
        in_specs=[
            pl.BlockSpec((tm, D_MODEL), row),
            pl.BlockSpec((tm, D_MODEL), row),
            pl.BlockSpec((tm, ROUTE_LANES), row),
            _const_spec((N_EXPERTS, D_MODEL, D_EXPERT)),
            _const_spec((N_EXPERTS, D_MODEL, D_EXPERT)),
            _const_spec((N_EXPERTS * D_EXPERT, D_MODEL)),
            _const_spec((1, D_MODEL)),
        ],
        out_specs=pl.BlockSpec((tm, D_MODEL), row),
        out_shape=jax.ShapeDtypeStruct((n, D_MODEL), F32),
        scratch_shapes=[pltpu.VMEM((tm, N_EXPERTS * D_EXPERT), BF16)],
        compiler_params=_params(("parallel",)),
        name="moe_final",
    )(x1, h2, dw, wg, wu, wd, nf)


def _prepare_weights(rel_bias, norm_mix, w_in, lam_q1, lam_k1, lam_q2, lam_k2, subln_g,
                     w_branch_a, w_branch_b, w_out, norm_ffn, w_group, w_router,
                     w_exp_gate, w_exp_up, w_exp_down, norm_final):
    w = w_in[0]
    c = QK_SCALE * LOG2E
    o = 0
    qa, o = w[:, o:o + DIFF_QK] * c, o + DIFF_QK
    ka, o = w[:, o:o + DIFF_QK], o + DIFF_QK
    va, o = w[:, o:o + DIFF_V], o + DIFF_V
    qb, o = w[:, o:o + DIL_W] * c, o + DIL_W
    kb, o = w[:, o:o + DIL_W], o + DIL_W
    vb, o = w[:, o:o + DIL_W], o + DIL_W
    w_gate = w[:, o:].astype(BF16)
    grp = lambda g: [t[:, g * GRP_W:(g + 1) * GRP_W] for t in (qb, kb, vb)]
    w_nat = jnp.concatenate([qa, ka, va] + grp(0), axis=1).astype(BF16)
    w_d1 = jnp.concatenate(grp(1), axis=1).astype(BF16)
    w_d2 = jnp.concatenate(grp(2), axis=1).astype(BF16)
    wr = jnp.concatenate([w_group[0], w_router[0].transpose(1, 0, 2).reshape(D_MODEL, N_EXPERTS)], axis=1)
    wr = jnp.pad(wr, ((0, 0), (0, ROUTE_LANES - wr.shape[1]))).astype(BF16)
    return dict(
        rel_bias=rel_bias.astype(F32),
        norm_mix=norm_mix[0][None].astype(F32),
        w_nat=w_nat, w_gate=w_gate, w_d1=w_d1, w_d2=w_d2,
        lam=jnp.stack([lam_q1[0], lam_k1[0], lam_q2[0], lam_k2[0]]).astype(F32),
        subln_g=subln_g[0][None].astype(F32),
        wa=w_branch_a[0].astype(BF16), wb=w_branch_b[0].astype(BF16), wo=w_out[0].astype(BF16),
        norm_ffn=norm_ffn[0][None].astype(F32), wr=wr,
        wg=w_exp_gate[0].astype(BF16), wu=w_exp_up[0].astype(BF16),
        wd=w_exp_down[0].reshape(N_EXPERTS * D_EXPERT, D_MODEL).astype(BF16),
        norm_final=norm_final[None].astype(F32),
    )


def _encoder(x, p):
    b, s, d = x.shape
    n = b * s
    nat, gates, dil1, dil2 = _norm_proj(x, p["norm_mix"], p["w_nat"], p["w_gate"], p["w_d1"], p["w_d2"])
    tq, tk = min(TQ_DIFF, s), min(TK_DIFF, s)
    oa = _diff_attn(nat, _diff_bias_tiles(p["rel_bias"], tq, tk), p["lam"], p["subln_g"])
    srcs = ((nat.reshape(b, 1, s, NAT_COLS), (2 * DIFF_QK + DIFF_V) // GRP_W), (dil1, 0), (dil2, 0))
    dil = []
    for g, (win, dilation) in enumerate(DIL_CONFIGS):
        tiles = _dil_bias_tiles(p["rel_bias"], g, s, win, dilation)
        dil.append(_dil_attn(srcs[g][0], srcs[g][1], tiles, g, win, dilation))
    x1, h2, dw = _merge_route(x, oa, dil, gates, p["wa"], p["wb"], p["wo"], p["norm_ffn"], p["wr"])
    y = _moe_final(x1.reshape(n, d), h2.reshape(n, d), dw.reshape(n, ROUTE_LANES),
                   p["wg"], p["wu"], p["wd"], p["norm_final"])
    return y.reshape(b, s, d)


def kernel(x_prompt, x_sample, rel_bias, norm_mix, w_in, lam_q1, lam_k1, lam_q2, lam_k2, subln_g,
           w_branch_a, w_branch_b, w_out, norm_ffn, w_group, w_router, w_exp_gate, w_exp_up,
           w_exp_down, norm_final):
    p = _prepare_weights(rel_bias, norm_mix, w_in, lam_q1, lam_k1, lam_q2, lam_k2, subln_g,
                         w_branch_a, w_branch_b, w_out, norm_ffn, w_group, w_router,
                         w_exp_gate, w_exp_up, w_exp_down, norm_final)
    return (_encoder(x_prompt, p), _encoder(x_sample, p))
```

```python
import functools
import math

import jax
import jax.numpy as jnp
from jax import lax
from jax.experimental import pallas as pl
from jax.experimental.pallas import tpu as pltpu

F32 = jnp.float32
BF16 = jnp.bfloat16

D_MODEL = 1024
HEAD_DIM = 64
H_DIFF = 4
H_DIL = 4
DIL_CONFIGS = ((128, 1), (512, 4), (2048, 16))
N_DIL = len(DIL_CONFIGS)
DIFF_QK = H_DIFF * 2 * HEAD_DIM
DIFF_V = H_DIFF * 2 * HEAD_DIM
DIL_W = N_DIL * H_DIL * HEAD_DIM
GRP_W = H_DIL * HEAD_DIM
GRP_COLS = 3 * GRP_W
NAT_COLS = 2 * DIFF_QK + DIFF_V + GRP_COLS
GATE_COLS = 2 * D_MODEL
Q_BLOCK = 128
N_BUCKETS = 32
MAX_DISTANCE = 128
N_GROUPS = 4
E_PER_GROUP = 4
N_EXPERTS = N_GROUPS * E_PER_GROUP
D_EXPERT = 256
EPS = 1e-6
NEG_INF = -1e30
LAM_INIT = 0.8 - 0.6 * math.exp(-0.3 * 0)
LOG2E = math.log2(math.e)
LN2 = math.log(2.0)
QK_SCALE = HEAD_DIM ** -0.5

LANES = 128
VMEM_LIMIT = 56 * 1024 * 1024

TM_PROJ = 512
TQ_DIFF = 512
TK_DIFF = 256
DIFF_GROUP = 4
DEN_ROWS = 16
TM_MERGE = 256
TM_MOE = 512
ROUTE_LANES = 128
EXP_LANE0 = N_GROUPS


def _const_spec(shape):
    nd = len(shape)
    return pl.BlockSpec(shape, lambda *_: (0,) * nd, pipeline_mode=pl.Buffered(1))


def _params(sem):
    return pltpu.CompilerParams(dimension_semantics=sem, vmem_limit_bytes=VMEM_LIMIT)


def _rel_bucket(rel):
    half = N_BUCKETS // 2
    max_exact = half // 2
    ret = (rel > 0).astype(jnp.int32) * half
    n = jnp.abs(rel)
    nf = jnp.maximum(n, 1).astype(F32)
    large = max_exact + (jnp.log(nf / max_exact) / math.log(MAX_DISTANCE / max_exact)
                         * (half - max_exact)).astype(jnp.int32)
    large = jnp.minimum(large, half - 1)
    return ret + jnp.where(n < max_exact, n, large)


def _bias_of_rel(table, rel):
    onehot = _rel_bucket(rel)[..., None] == jnp.arange(N_BUCKETS)
    vals = jnp.sum(jnp.where(onehot[..., None], table.astype(F32), 0.0), axis=-2)
    return jnp.moveaxis(vals, -1, 0)


def _toeplitz(vec, rows, cols):
    w = rows + cols - 1
    lead = vec.shape[:-1]
    a = jnp.pad(vec, [(0, 0)] * len(lead) + [(0, 1)])
    a = jnp.broadcast_to(a[..., None, :], lead + (rows, w + 1)).reshape(lead + (rows * (w + 1),))
    return a[..., rows - 1:rows - 1 + rows * w].reshape(lead + (rows, w))[..., :cols]


def _chunks(total, width):
    out, c = [], 0
    while c < total:
        w = min(width, total - c)
        out.append((c, w))
        c += w
    return out


def _norm_proj_kernel(x_ref, g_ref, wn_ref, wg_ref, w1_ref, w2_ref,
                      on_ref, og_ref, o1_ref, o2_ref, hs_ref, *, tm):
    x = x_ref[0]
    ms = jnp.mean(x * x, axis=-1, keepdims=True)
    hf = x * lax.rsqrt(ms + EPS) * g_ref[...]
    for c in range(D_MODEL // LANES):
        hs_ref[c] = hf[:, c * LANES:(c + 1) * LANES]
    h = hf.astype(BF16)
    for c0, cw in _chunks(NAT_COLS, 512):
        on_ref[0, :, c0:c0 + cw] = jnp.dot(h, wn_ref[:, c0:c0 + cw],
                                           preferred_element_type=F32).astype(BF16)
    for c0, cw in _chunks(GATE_COLS, 512):
        og_ref[0, :, c0:c0 + cw] = jnp.dot(h, wg_ref[:, c0:c0 + cw],
                                           preferred_element_type=F32).astype(BF16)
    for w_ref, o_ref, d in ((w1_ref, o1_ref, DIL_CONFIGS[1][1]), (w2_ref, o2_ref, DIL_CONFIGS[2][1])):
        rows = tm // d
        hp = jnp.concatenate(
            [jnp.concatenate([hs_ref[c, pl.ds(r, rows, stride=d), :] for r in range(d)], axis=0)
             for c in range(D_MODEL // LANES)], axis=1).astype(BF16)
        for c0, cw in _chunks(GRP_COLS, 256):
            res = jnp.dot(hp, w_ref[:, c0:c0 + cw], preferred_element_type=F32).astype(BF16)
            o_ref[0, :, :, c0:c0 + cw] = res.reshape(d, rows, cw)


def _norm_proj(x, g, w_nat, w_gate, w_d1, w_d2):
    b, s, _ = x.shape
    tm = TM_PROJ
    d1, d2 = DIL_CONFIGS[1][1], DIL_CONFIGS[2][1]
    return pl.pallas_call(
        functools.partial(_norm_proj_kernel, tm=tm),
        grid=(b, s // tm),
        in_specs=[
            pl.BlockSpec((1, tm, D_MODEL), lambda bi, i: (bi, i, 0)),
            _const_spec((1, D_MODEL)),
            _const_spec((D_MODEL, NAT_COLS)),
            _const_spec((D_MODEL, GATE_COLS)),
            _const_spec((D_MODEL, GRP_COLS)),
            _const_spec((D_MODEL, GRP_COLS)),
        ],
        out_specs=[
            pl.BlockSpec((1, tm, NAT_COLS), lambda bi, i: (bi, i, 0)),
            pl.BlockSpec((1, tm, GATE_COLS), lambda bi, i: (bi, i, 0)),
            pl.BlockSpec((1, d1, tm // d1, GRP_COLS), lambda bi, i: (bi, 0, i, 0)),
            pl.BlockSpec((1, d2, tm // d2, GRP_COLS), lambda bi, i: (bi, 0, i, 0)),
        ],
        out_shape=[
            jax.ShapeDtypeStruct((b, s, NAT_COLS), BF16),
            jax.ShapeDtypeStruct((b, s, GATE_COLS), BF16),
            jax.ShapeDtypeStruct((b, d1, s // d1, GRP_COLS), BF16),
            jax.ShapeDtypeStruct((b, d2, s // d2, GRP_COLS), BF16),
        ],
        scratch_shapes=[pltpu.VMEM((D_MODEL // LANES, tm, LANES), F32)],
        compiler_params=_params(("parallel", "parallel")),
        name="norm_proj",
    )(x, g, w_nat, w_gate, w_d1, w_d2)


def _diff_bias_tiles(rel_bias, tq, tk):
    ratio = tq // tk
    c = jnp.arange(tk + tq - 1)
    t = jnp.arange(-2, ratio + 2)
    rel = (t * tk)[:, None] + (tk - 1) - c[None, :]
    vec = _bias_of_rel(rel_bias[:, :H_DIFF], rel) * LOG2E
    far = _bias_of_rel(rel_bias[:, :H_DIFF], jnp.asarray([-MAX_DISTANCE, MAX_DISTANCE])) * LOG2E
    return _toeplitz(vec, tk, tq), far


def _diff_attn_kernel(lam_ref, far_ref, q_ref, k_ref, v_ref, bias_ref, g_ref, o_ref,
                      vt_ref, st_ref, p_ref, acc_ref, *, tq, tk, nkb):
    h = pl.program_id(1)
    qi = pl.program_id(2)
    ratio = tq // tk
    n_band = ratio + 2
    vchunk = min(512, nkb * tk)

    @pl.when(qi == 0)
    def _():
        for c in range(nkb * tk // vchunk):
            cs = slice(c * vchunk, (c + 1) * vchunk)
            vt_ref[:LANES, cs] = v_ref[0, cs, :].astype(F32).T.astype(BF16)
        vt_ref[LANES:, :] = jnp.ones((DEN_ROWS, nkb * tk), BF16)

    qt = q_ref[0].astype(F32).T
    first = lax.broadcasted_iota(jnp.int32, (LANES, tq), 0) < HEAD_DIM
    qts = (jnp.where(first, qt, 0.0).astype(BF16), jnp.where(first, 0.0, qt).astype(BF16))
    kb0 = qi * ratio

    def key_block(j):
        kb = kb0 - 1 + j
        return jnp.where(kb < 0, kb + nkb, jnp.where(kb >= nkb, kb - nkb, kb))

    def scores(j, slot):
        ks = pl.multiple_of(key_block(j) * tk, tk)
        k = k_ref[0, pl.ds(ks, tk), :]
        for mp in range(2):
            st_ref[slot, mp] = jnp.dot(k, qts[mp], preferred_element_type=F32)

    def softmax(j, slot, stats, band):
        kb = key_block(j)
        if band:
            bt = bias_ref[0, jnp.clip(kb - kb0, -2, ratio + 1) + 2]
        else:
            far_c = jnp.where(kb < kb0, far_ref[h, 0], far_ref[h, 1])
        out, alphas = [], []
        for mp in range(2):
            m = stats[mp]
            st = st_ref[slot, mp]
            if band:
                st = st + bt
                m_new = jnp.maximum(m, jnp.max(st, axis=0, keepdims=True))
                shift = m_new
            else:
                m_new = jnp.maximum(m, jnp.max(st, axis=0, keepdims=True) + far_c)
                shift = m_new - far_c
            p = jnp.exp2(st - shift)
            p_ref[slot, mp] = p.astype(BF16)
            out.append(m_new)
            alphas.append(jnp.exp2(m - m_new))
        return tuple(out), tuple(alphas)

    def values(j, slot, alphas):
        ks = pl.multiple_of(key_block(j) * tk, tk)
        vt = vt_ref[:, pl.ds(ks, tk)]
        for mp in range(2):
            acc_ref[mp] = alphas[mp] * acc_ref[mp] + jnp.dot(vt, p_ref[slot, mp],
                                                             preferred_element_type=F32)

    def group(jj, carry, base, band):
        stats, alphas = carry[:2], carry[2:]
        j = base + DIFF_GROUP * jj
        for u in range(DIFF_GROUP):
            scores(j + u + 1, (u + 1) % DIFF_GROUP)
            values(j + u - 1, (u - 1) % DIFF_GROUP, alphas)
            stats, alphas = softmax(j + u, u, stats, band)
        return stats + alphas

    last = DIFF_GROUP - 1
    acc_ref[...] = jnp.zeros_like(acc_ref)
    p_ref[last] = jnp.zeros_like(p_ref[last])
    scores(0, 0)
    ninf, one = (jnp.full((1, tq), v, F32) for v in (-jnp.inf, 1.0))
    carry = (ninf, ninf, one, one)
    carry = lax.fori_loop(0, n_band // DIFF_GROUP, functools.partial(group, base=0, band=True), carry)
    carry = lax.fori_loop(0, (nkb - n_band) // DIFF_GROUP,
                          functools.partial(group, base=n_band, band=False), carry)
    values(nkb - 1, last, carry[2:])
    l0, l1 = acc_ref[0, LANES:LANES + 1, :], acc_ref[1, LANES:LANES + 1, :]

    lp = lam_ref[...]
    lam = (jnp.exp(jnp.sum(lp[0:1] * lp[1:2], axis=-1, keepdims=True))
           - jnp.exp(jnp.sum(lp[2:3] * lp[3:4], axis=-1, keepdims=True)) + LAM_INIT)
    ot = acc_ref[0, :LANES, :] / l0 - lam * (acc_ref[1, :LANES, :] / l1)
    ms = jnp.mean(ot * ot, axis=0, keepdims=True)
    y = (ot * lax.rsqrt(ms + EPS)).T * g_ref[...] * (1.0 - LAM_INIT)
    o_ref[0] = y.astype(BF16)


def _diff_attn(nat, bias_tiles, far, lam_params, subln_g):
    b, s, _ = nat.shape
    tq, tk = min(TQ_DIFF, s), min(TK_DIFF, s)
    nt = bias_tiles.shape[1]
    ratio, nkb = tq // tk, s // tk
    assert tq % tk == 0 and tk >= MAX_DISTANCE and nkb >= ratio + 2
    assert (ratio + 2) % DIFF_GROUP == 0 and nkb % DIFF_GROUP == 0
    kern = functools.partial(_diff_attn_kernel, tq=tq, tk=tk, nkb=s // tk)
    return pl.pallas_call(
        kern,
        grid=(b, H_DIFF, s // tq),
        in_specs=[
            _const_spec((4, HEAD_DIM)),
            pl.BlockSpec(memory_space=pltpu.SMEM),
            pl.BlockSpec((1, tq, LANES), lambda bi, h, qi: (bi, qi, h)),
            pl.BlockSpec((1, s, LANES), lambda bi, h, qi: (bi, 0, H_DIFF + h)),
            pl.BlockSpec((1, s, LANES), lambda bi, h, qi: (bi, 0, 2 * H_DIFF + h)),
            pl.BlockSpec((1, nt, tk, tq), lambda bi, h, qi: (h, 0, 0, 0)),
            _const_spec((1, 2 * HEAD_DIM)),
        ],
        out_specs=pl.BlockSpec((1, tq, LANES), lambda bi, h, qi: (bi, qi, h)),
        out_shape=jax.ShapeDtypeStruct((b, s, DIFF_V), BF16),
        scratch_shapes=[
            pltpu.VMEM((LANES + DEN_ROWS, s), BF16),
            pltpu.VMEM((DIFF_GROUP, 2, tk, tq), F32),
            pltpu.VMEM((DIFF_GROUP, 2, tk, tq), BF16),
            pltpu.VMEM((2, LANES + DEN_ROWS, tq), F32),
        ],
        compiler_params=_params(("parallel", "parallel", "arbitrary")),
        name="diff_attn",
    )(lam_params, far, nat, nat, nat, bias_tiles, subln_g)


def _dil_geometry(s, window, dilation):
    l = s // dilation
    w = window // (2 * dilation)
    kw = min(Q_BLOCK + 2 * w, l)
    nb = l // Q_BLOCK
    return l, w, kw, nb


def _dil_bias_tiles(rel_bias, g, s, window, dilation):
    l, w, kw, nb = _dil_geometry(s, window, dilation)
    c0 = H_DIFF + g * H_DIL
    c = jnp.arange(Q_BLOCK + kw - 1) - (Q_BLOCK - 1)
    shifts = []
    for n in (0, min(1, nb - 1), nb - 1):
        l0 = n * Q_BLOCK
        shifts.append(min(max(l0 - w, 0), l - kw) - l0)
    off = jnp.asarray(shifts)[:, None] + c[None, :]
    vec = _bias_of_rel(rel_bias[:, c0:c0 + H_DIL], off * dilation) * LOG2E
    vec = jnp.where((jnp.abs(off) <= w)[None], vec, NEG_INF)
    return _toeplitz(vec, Q_BLOCK, kw).transpose(1, 0, 2, 3)


def _dil_attn_kernel(q_ref, k_ref, v_ref, bias_ref, o_ref, lse_ref, *, l, w, kw, nb):
    lane = lax.broadcasted_iota(jnp.int32, (Q_BLOCK, LANES), 1)
    first = lane < HEAD_DIM

    def block(n, carry):
        l0 = pl.multiple_of(n * Q_BLOCK, Q_BLOCK)
        ws = pl.multiple_of(jnp.clip(l0 - w, 0, l - kw), 64)
        t = jnp.where(n == 0, 0, jnp.where(n == nb - 1, 2, 1))
        for pair in range(H_DIL // 2):
            cs = slice(pair * LANES, (pair + 1) * LANES)
            qp = q_ref[0, 0, pl.ds(l0, Q_BLOCK), cs]
            kp = k_ref[0, 0, pl.ds(ws, kw), cs]
            vp = v_ref[0, 0, pl.ds(ws, kw), cs]
            zero = jnp.zeros_like(qp)
            outs, lses = [], []
            for hh in range(2):
                qm = jnp.where(first, qp, zero) if hh == 0 else jnp.where(first, zero, qp)
                s = lax.dot_general(qm, kp, (((1,), (1,)), ((), ())), preferred_element_type=F32)
                s = s + bias_ref[t, pair * 2 + hh]
                mx = jnp.max(s, axis=-1, keepdims=True)
                p = jnp.exp2(s - mx)
                den = jnp.sum(p, axis=-1, keepdims=True)
                pv = jnp.dot(p.astype(BF16), vp, preferred_element_type=F32)
                outs.append(pv / den)
                lses.append(mx * LN2 + jnp.log(den))
            o_ref[0, 0, pl.ds(l0, Q_BLOCK), cs] = jnp.where(first, outs[0], outs[1]).astype(BF16)
            lse_ref[0, 0, pl.ds(l0, Q_BLOCK), cs] = jnp.where(first, lses[0], lses[1])
        return carry

    lax.fori_loop(0, nb, block, 0)


def _dil_attn(arr, col0, bias_tiles, g, window, dilation):
    b, d, l, _ = arr.shape
    _, w, kw, nb = _dil_geometry(l * d, window, dilation)
    kern = functools.partial(_dil_attn_kernel, l=l, w=w, kw=kw, nb=nb)
    return pl.pallas_call(
        kern,
        grid=(b, d),
        in_specs=[
            pl.BlockSpec((1, 1, l, GRP_W), lambda bi, r: (bi, r, 0, col0)),
            pl.BlockSpec((1, 1, l, GRP_W), lambda bi, r: (bi, r, 0, col0 + 1)),
            pl.BlockSpec((1, 1, l, GRP_W), lambda bi, r: (bi, r, 0, col0 + 2)),
            _const_spec((3, H_DIL, Q_BLOCK, kw)),
        ],
        out_specs=[
            pl.BlockSpec((1, 1, l, GRP_W), lambda bi, r: (bi, r, 0, 0)),
            pl.BlockSpec((1, 1, l, GRP_W), lambda bi, r: (bi, r, 0, 0)),
        ],
        out_shape=[
            jax.ShapeDtypeStruct((b, d, l, GRP_W), BF16),
            jax.ShapeDtypeStruct((b, d, l, GRP_W), F32),
        ],
        compiler_params=_params(("parallel", "parallel")),
        name=f"dil_attn_g{g}",
    )(arr, arr, arr, bias_tiles)


def _merge_route_kernel(x_ref, oa_ref, o0_ref, l0_ref, o1_ref, l1_ref, o2_ref, l2_ref, gt_ref,
                        wa_ref, wb_ref, wo_ref, nf_ref, wr_ref, x1_ref, h2_ref, dw_ref,
                        so1_ref, sl1_ref, so2_ref, sl2_ref, *, tm):
    for src, dst, d in ((o1_ref, so1_ref, DIL_CONFIGS[1][1]), (l1_ref, sl1_ref, DIL_CONFIGS[1][1]),
                        (o2_ref, so2_ref, DIL_CONFIGS[2][1]), (l2_ref, sl2_ref, DIL_CONFIGS[2][1])):
        for r in range(d):
            for c in range(GRP_W // LANES):
                dst[c, pl.ds(r, tm // d, stride=d), :] = src[0, r, :, c * LANES:(c + 1) * LANES].astype(F32)
    unsplit = lambda ref: jnp.concatenate([ref[c] for c in range(GRP_W // LANES)], axis=1)
    l0, l1, l2 = l0_ref[0, 0], unsplit(sl1_ref), unsplit(sl2_ref)
    mx = jnp.maximum(jnp.maximum(l0, l1), l2)
    e0, e1, e2 = jnp.exp(l0 - mx), jnp.exp(l1 - mx), jnp.exp(l2 - mx)
    ob = (e0 * o0_ref[0, 0].astype(F32) + e1 * unsplit(so1_ref) + e2 * unsplit(so2_ref)) / (e0 + e1 + e2)
    pa = jnp.dot(oa_ref[0], wa_ref[...], preferred_element_type=F32)
    pb = jnp.dot(ob.astype(BF16), wb_ref[...], preferred_element_type=F32)
    ga = jax.nn.sigmoid(gt_ref[0, :, :D_MODEL].astype(F32))
    gb = jax.nn.sigmoid(gt_ref[0, :, D_MODEL:].astype(F32))
    merged = (ga * pa + gb * pb).astype(BF16)
    x1 = x_ref[0] + jnp.dot(merged, wo_ref[...], preferred_element_type=F32)
    x1_ref[0] = x1
    ms = jnp.mean(x1 * x1, axis=-1, keepdims=True)
    h2 = (x1 * lax.rsqrt(ms + EPS) * nf_ref[...]).astype(BF16)
    h2_ref[0] = h2

    lg = jnp.dot(h2, wr_ref[...], preferred_element_type=F32)
    lane = lax.broadcasted_iota(jnp.int32, lg.shape, 1)
    big = jnp.int32(ROUTE_LANES)
    gl = jnp.where(lane < N_GROUPS, lg, -jnp.inf)
    gmax = jnp.max(gl, axis=-1, keepdims=True)
    gi = jnp.min(jnp.where(gl == gmax, lane, big), axis=-1, keepdims=True)
    gw = 1.0 / jnp.sum(jnp.exp(gl - gmax), axis=-1, keepdims=True)
    e_lo = EXP_LANE0 + E_PER_GROUP * gi
    es = jnp.where((lane >= e_lo) & (lane < e_lo + E_PER_GROUP), lg, -jnp.inf)
    v1 = jnp.max(es, axis=-1, keepdims=True)
    i1 = jnp.min(jnp.where(es == v1, lane, big), axis=-1, keepdims=True)
    es2 = jnp.where(lane == i1, -jnp.inf, es)
    v2 = jnp.max(es2, axis=-1, keepdims=True)
    i2 = jnp.min(jnp.where(es2 == v2, lane, big), axis=-1, keepdims=True)
    t = jnp.exp(v2 - v1)
    w1 = gw / (1.0 + t)
    w2 = w1 * t
    dw_ref[0] = jnp.where(lane == i1, w1, 0.0) + jnp.where(lane == i2, w2, 0.0)


def _merge_route(x, oa, dil, gates, wa, wb, wo, nf, wr):
    b, s, _ = x.shape
    tm = TM_MERGE
    d1, d2 = DIL_CONFIGS[1][1], DIL_CONFIGS[2][1]
    tok = lambda bi, i: (bi, i, 0)
    cls = lambda bi, i: (bi, 0, i, 0)
    (o0, s0), (o1, s1), (o2, s2) = dil
    return pl.pallas_call(
        functools.partial(_merge_route_kernel, tm=tm),
        grid=(b, s // tm),
        in_specs=[
            pl.BlockSpec((1, tm, D_MODEL), tok),
            pl.BlockSpec((1, tm, DIFF_V), tok),
            pl.BlockSpec((1, 1, tm, GRP_W), cls), pl.BlockSpec((1, 1, tm, GRP_W), cls),
            pl.BlockSpec((1, d1, tm // d1, GRP_W), cls), pl.BlockSpec((1, d1, tm // d1, GRP_W), cls),
            pl.BlockSpec((1, d2, tm // d2, GRP_W), cls), pl.BlockSpec((1, d2, tm // d2, GRP_W), cls),
            pl.BlockSpec((1, tm, GATE_COLS), tok),
            _const_spec((DIFF_V, D_MODEL)),
            _const_spec((GRP_W, D_MODEL)),
            _const_spec((D_MODEL, D_MODEL)),
            _const_spec((1, D_MODEL)),
            _const_spec((D_MODEL, ROUTE_LANES)),
        ],
        out_specs=[
            pl.BlockSpec((1, tm, D_MODEL), tok),
            pl.BlockSpec((1, tm, D_MODEL), tok),
            pl.BlockSpec((1, tm, ROUTE_LANES), tok),
        ],
        out_shape=[
            jax.ShapeDtypeStruct((b, s, D_MODEL), F32),
            jax.ShapeDtypeStruct((b, s, D_MODEL), BF16),
            jax.ShapeDtypeStruct((b, s, ROUTE_LANES), F32),
        ],
        scratch_shapes=[pltpu.VMEM((GRP_W // LANES, tm, LANES), F32) for _ in range(4)],
        compiler_params=_params(("parallel", "parallel")),
        name="merge_route",
    )(x, oa, o0, s0, o1, s1, o2, s2, gates, wa, wb, wo, nf, wr)


def _moe_final_kernel(x1_ref, h2_ref, dw_ref, wg_ref, wu_ref, wd_ref, nf_ref, y_ref, he_ref):
    h2 = h2_ref[...]
    dw = dw_ref[...]
    for e in range(N_EXPERTS):
        gp = jnp.dot(h2, wg_ref[e], preferred_element_type=F32)
        up = jnp.dot(h2, wu_ref[e], preferred_element_type=F32)
        we = dw[:, EXP_LANE0 + e:EXP_LANE0 + e + 1]
        he_ref[:, e * D_EXPERT:(e + 1) * D_EXPERT] = (jax.nn.silu(gp) * up * we).astype(BF16)
    moe = jnp.dot(he_ref[...], wd_ref[...], preferred_element_type=F32)
    x2 = x1_ref[...] + moe
    ms = jnp.mean(x2 * x2, axis=-1, keepdims=True)
    y_ref[...] = x2 * lax.rsqrt(ms + EPS) * nf_ref[...]


def _moe_final(x1, h2, dw, wg, wu, wd, nf):
    n = x1.shape[0]
    tm = TM_MOE
    row = lambda i: (i, 0)
    return pl.pallas_call(
        _moe_final_kernel,
        grid=(n // tm,),
        in_specs=[
            pl.BlockSpec((tm, D_MODEL), row),
            pl.BlockSpec((tm, D_MODEL), row),
            pl.BlockSpec((tm, ROUTE_LANES), row),
            _const_spec((N_EXPERTS, D_MODEL, D_EXPERT)),
            _const_spec((N_EXPERTS, D_MODEL, D_EXPERT)),
            _const_spec((N_EXPERTS * D_EXPERT, D_MODEL)),
            _const_spec((1, D_MODEL)),
        ],
        out_specs=pl.BlockSpec((tm, D_MODEL), row),
        out_shape=jax.ShapeDtypeStruct((n, D_MODEL), F32),
        scratch_shapes=[pltpu.VMEM((tm, N_EXPERTS * D_EXPERT), BF16)],
        compiler_params=_params(("parallel",)),
        name="moe_final",
    )(x1, h2, dw, wg, wu, wd, nf)


def _prepare_weights(rel_bias, norm_mix, w_in, lam_q1, lam_k1, lam_q2, lam_k2, subln_g,
                     w_branch_a, w_branch_b, w_out, norm_ffn, w_group, w_router,
                     w_exp_gate, w_exp_up, w_exp_down, norm_final):
    w = w_in[0]
    c = QK_SCALE * LOG2E
    o = 0
    qa, o = w[:, o:o + DIFF_QK] * c, o + DIFF_QK
    ka, o = w[:, o:o + DIFF_QK], o + DIFF_QK
    va, o = w[:, o:o + DIFF_V], o + DIFF_V
    qb, o = w[:, o:o + DIL_W] * c, o + DIL_W
    kb, o = w[:, o:o + DIL_W], o + DIL_W
    vb, o = w[:, o:o + DIL_W], o + DIL_W
    w_gate = w[:, o:].astype(BF16)
    grp = lambda g: [t[:, g * GRP_W:(g + 1) * GRP_W] for t in (qb, kb, vb)]
    w_nat = jnp.concatenate([qa, ka, va] + grp(0), axis=1).astype(BF16)
    w_d1 = jnp.concatenate(grp(1), axis=1).astype(BF16)
    w_d2 = jnp.concatenate(grp(2), axis=1).astype(BF16)
    wr = jnp.concatenate([w_group[0], w_router[0].transpose(1, 0, 2).reshape(D_MODEL, N_EXPERTS)], axis=1)
    wr = jnp.pad(wr, ((0, 0), (0, ROUTE_LANES - wr.shape[1]))).astype(BF16)
    return dict(
        rel_bias=rel_bias.astype(F32),
        norm_mix=norm_mix[0][None].astype(F32),
        w_nat=w_nat, w_gate=w_gate, w_d1=w_d1, w_d2=w_d2,
        lam=jnp.stack([lam_q1[0], lam_k1[0], lam_q2[0], lam_k2[0]]).astype(F32),
        subln_g=subln_g[0][None].astype(F32),
        wa=w_branch_a[0].astype(BF16), wb=w_branch_b[0].astype(BF16), wo=w_out[0].astype(BF16),
        norm_ffn=norm_ffn[0][None].astype(F32), wr=wr,
        wg=w_exp_gate[0].astype(BF16), wu=w_exp_up[0].astype(BF16),
        wd=w_exp_down[0].reshape(N_EXPERTS * D_EXPERT, D_MODEL).astype(BF16),
        norm_final=norm_final[None].astype(F32),
    )


def _encoder(x, p):
    b, s, d = x.shape
    n = b * s
    nat, gates, dil1, dil2 = _norm_proj(x, p["norm_mix"], p["w_nat"], p["w_gate"], p["w_d1"], p["w_d2"])
    tiles, far = _diff_bias_tiles(p["rel_bias"], min(TQ_DIFF, s), min(TK_DIFF, s))
    oa = _diff_attn(nat, tiles, far, p["lam"], p["subln_g"])
    srcs = ((nat.reshape(b, 1, s, NAT_COLS), (2 * DIFF_QK + DIFF_V) // GRP_W), (dil1, 0), (dil2, 0))
    dil = []
    for g, (win, dilation) in enumerate(DIL_CONFIGS):
        tiles = _dil_bias_tiles(p["rel_bias"], g, s, win, dilation)
        dil.append(_dil_attn(srcs[g][0], srcs[g][1], tiles, g, win, dilation))
    x1, h2, dw = _merge_route(x, oa, dil, gates, p["wa"], p["wb"], p["wo"], p["norm_ffn"], p["wr"])
    y = _moe_final(x1.reshape(n, d), h2.reshape(n, d), dw.reshape(n, ROUTE_LANES),
                   p["wg"], p["wu"], p["wd"], p["norm_final"])
    return y.reshape(b, s, d)


def kernel(x_prompt, x_sample, rel_bias, norm_mix, w_in, lam_q1, lam_k1, lam_q2, lam_k2, subln_g,
           w_branch_a, w_branch_b, w_out, norm_ffn, w_group, w_router, w_exp_gate, w_exp_up,
           w_exp_down, norm_final):
    p = _prepare_weights(rel_bias, norm_mix, w_in, lam_q1, lam_k1, lam_q2, lam_k2, subln_g,
                         w_branch_a, w_branch_b, w_out, norm_ffn, w_group, w_router,
                         w_exp_gate, w_exp_up, w_exp_down, norm_final)
    return (_encoder(x_prompt, p), _encoder(x_sample, p))
```

```python
import functools
import math

import jax
import jax.numpy as jnp
from jax import lax
from jax.experimental import pallas as pl
from jax.experimental.pallas import tpu as pltpu

F32 = jnp.float32
BF16 = jnp.bfloat16

D_MODEL = 1024
HEAD_DIM = 64
H_DIFF = 4
H_DIL = 4
DIL_CONFIGS = ((128, 1), (512, 4), (2048, 16))
N_DIL = len(DIL_CONFIGS)
DIFF_QK = H_DIFF * 2 * HEAD_DIM
DIFF_V = H_DIFF * 2 * HEAD_DIM
DIL_W = N_DIL * H_DIL * HEAD_DIM
GRP_W = H_DIL * HEAD_DIM
GRP_COLS = 3 * GRP_W
NAT_COLS = 2 * DIFF_QK + DIFF_V + GRP_COLS
GATE_COLS = 2 * D_MODEL
Q_BLOCK = 128
N_BUCKETS = 32
MAX_DISTANCE = 128
N_GROUPS = 4
E_PER_GROUP = 4
N_EXPERTS = N_GROUPS * E_PER_GROUP
D_EXPERT = 256
EPS = 1e-6
NEG_INF = -1e30
LAM_INIT = 0.8 - 0.6 * math.exp(-0.3 * 0)
LOG2E = math.log2(math.e)
LN2 = math.log(2.0)
QK_SCALE = HEAD_DIM ** -0.5

LANES = 128
VMEM_LIMIT = 56 * 1024 * 1024

TM_PROJ = 512
TQ_DIFF = 512
TK_DIFF = 256
DIFF_GROUP = 4
DEN_ROWS = 16
DIL_QB = 256
TM_MERGE = 512
SUB_MERGE = 256
TM_MOE = 512
ROUTE_LANES = 128
EXP_LANE0 = N_GROUPS


def _const_spec(shape):
    nd = len(shape)
    return pl.BlockSpec(shape, lambda *_: (0,) * nd, pipeline_mode=pl.Buffered(1))


def _params(sem):
    return pltpu.CompilerParams(dimension_semantics=sem, vmem_limit_bytes=VMEM_LIMIT)


def _rel_bucket(rel):
    half = N_BUCKETS // 2
    max_exact = half // 2
    ret = (rel > 0).astype(jnp.int32) * half
    n = jnp.abs(rel)
    nf = jnp.maximum(n, 1).astype(F32)
    large = max_exact + (jnp.log(nf / max_exact) / math.log(MAX_DISTANCE / max_exact)
                         * (half - max_exact)).astype(jnp.int32)
    large = jnp.minimum(large, half - 1)
    return ret + jnp.where(n < max_exact, n, large)


def _bias_of_rel(table, rel):
    onehot = _rel_bucket(rel)[..., None] == jnp.arange(N_BUCKETS)
    vals = jnp.sum(jnp.where(onehot[..., None], table.astype(F32), 0.0), axis=-2)
    return jnp.moveaxis(vals, -1, 0)


def _toeplitz(vec, rows, cols):
    w = rows + cols - 1
    lead = vec.shape[:-1]
    a = jnp.pad(vec, [(0, 0)] * len(lead) + [(0, 1)])
    a = jnp.broadcast_to(a[..., None, :], lead + (rows, w + 1)).reshape(lead + (rows * (w + 1),))
    return a[..., rows - 1:rows - 1 + rows * w].reshape(lead + (rows, w))[..., :cols]


def _chunks(total, width):
    out, c = [], 0
    while c < total:
        w = min(width, total - c)
        out.append((c, w))
        c += w
    return out


def _norm_proj_kernel(x_ref, g_ref, wn_ref, wg_ref, w1_ref, w2_ref,
                      on_ref, og_ref, o1_ref, o2_ref, hs_ref, *, tm):
    x = x_ref[0]
    ms = jnp.mean(x * x, axis=-1, keepdims=True)
    hf = x * lax.rsqrt(ms + EPS) * g_ref[...]
    for c in range(D_MODEL // LANES):
        hs_ref[c] = hf[:, c * LANES:(c + 1) * LANES]
    h = hf.astype(BF16)
    for c0, cw in _chunks(NAT_COLS, 512):
        on_ref[0, :, c0:c0 + cw] = jnp.dot(h, wn_ref[:, c0:c0 + cw],
                                           preferred_element_type=F32).astype(BF16)
    for c0, cw in _chunks(GATE_COLS, 512):
        og_ref[0, :, c0:c0 + cw] = jnp.dot(h, wg_ref[:, c0:c0 + cw],
                                           preferred_element_type=F32).astype(BF16)
    for w_ref, o_ref, d in ((w1_ref, o1_ref, DIL_CONFIGS[1][1]), (w2_ref, o2_ref, DIL_CONFIGS[2][1])):
        rows = tm // d
        hp = jnp.concatenate(
            [jnp.concatenate([hs_ref[c, pl.ds(r, rows, stride=d), :] for r in range(d)], axis=0)
             for c in range(D_MODEL // LANES)], axis=1).astype(BF16)
        for c0, cw in _chunks(GRP_COLS, 256):
            res = jnp.dot(hp, w_ref[:, c0:c0 + cw], preferred_element_type=F32).astype(BF16)
            o_ref[0, :, :, c0:c0 + cw] = res.reshape(d, rows, cw)


def _norm_proj(x, g, w_nat, w_gate, w_d1, w_d2):
    b, s, _ = x.shape
    tm = TM_PROJ
    d1, d2 = DIL_CONFIGS[1][1], DIL_CONFIGS[2][1]
    return pl.pallas_call(
        functools.partial(_norm_proj_kernel, tm=tm),
        grid=(b, s // tm),
        in_specs=[
            pl.BlockSpec((1, tm, D_MODEL), lambda bi, i: (bi, i, 0)),
            _const_spec((1, D_MODEL)),
            _const_spec((D_MODEL, NAT_COLS)),
            _const_spec((D_MODEL, GATE_COLS)),
            _const_spec((D_MODEL, GRP_COLS)),
            _const_spec((D_MODEL, GRP_COLS)),
        ],
        out_specs=[
            pl.BlockSpec((1, tm, NAT_COLS), lambda bi, i: (bi, i, 0)),
            pl.BlockSpec((1, tm, GATE_COLS), lambda bi, i: (bi, i, 0)),
            pl.BlockSpec((1, d1, tm // d1, GRP_COLS), lambda bi, i: (bi, 0, i, 0)),
            pl.BlockSpec((1, d2, tm // d2, GRP_COLS), lambda bi, i: (bi, 0, i, 0)),
        ],
        out_shape=[
            jax.ShapeDtypeStruct((b, s, NAT_COLS), BF16),
            jax.ShapeDtypeStruct((b, s, GATE_COLS), BF16),
            jax.ShapeDtypeStruct((b, d1, s // d1, GRP_COLS), BF16),
            jax.ShapeDtypeStruct((b, d2, s // d2, GRP_COLS), BF16),
        ],
        scratch_shapes=[pltpu.VMEM((D_MODEL // LANES, tm, LANES), F32)],
        compiler_params=_params(("parallel", "parallel")),
        name="norm_proj",
    )(x, g, w_nat, w_gate, w_d1, w_d2)


def _diff_bias_tiles(rel_bias, tq, tk):
    ratio = tq // tk
    c = jnp.arange(tk + tq - 1)
    t = jnp.arange(-2, ratio + 2)
    rel = (t * tk)[:, None] + (tk - 1) - c[None, :]
    vec = _bias_of_rel(rel_bias[:, :H_DIFF], rel) * LOG2E
    far = _bias_of_rel(rel_bias[:, :H_DIFF], jnp.asarray([-MAX_DISTANCE, MAX_DISTANCE])) * LOG2E
    return _toeplitz(vec, tk, tq), far


def _diff_attn_kernel(lam_ref, far_ref, q_ref, k_ref, v_ref, bias_ref, g_ref, o_ref,
                      vt_ref, st_ref, p_ref, acc_ref, *, tq, tk, nkb):
    h = pl.program_id(1)
    qi = pl.program_id(2)
    ratio = tq // tk
    n_band = ratio + 2
    vchunk = min(512, nkb * tk)

    @pl.when(qi == 0)
    def _():
        for c in range(nkb * tk // vchunk):
            cs = slice(c * vchunk, (c + 1) * vchunk)
            vt_ref[:LANES, cs] = v_ref[0, cs, :].astype(F32).T.astype(BF16)
        vt_ref[LANES:, :] = jnp.ones((DEN_ROWS, nkb * tk), BF16)

    qt = q_ref[0].astype(F32).T
    first = lax.broadcasted_iota(jnp.int32, (LANES, tq), 0) < HEAD_DIM
    qts = (jnp.where(first, qt, 0.0).astype(BF16), jnp.where(first, 0.0, qt).astype(BF16))
    kb0 = qi * ratio

    def key_block(j):
        kb = kb0 - 1 + j
        return jnp.where(kb < 0, kb + nkb, jnp.where(kb >= nkb, kb - nkb, kb))

    def scores(j, slot):
        ks = pl.multiple_of(key_block(j) * tk, tk)
        k = k_ref[0, pl.ds(ks, tk), :]
        for mp in range(2):
            st_ref[slot, mp] = jnp.dot(k, qts[mp], preferred_element_type=F32)

    def softmax(j, slot, stats, band):
        kb = key_block(j)
        if band:
            bt = bias_ref[0, jnp.clip(kb - kb0, -2, ratio + 1) + 2]
        else:
            far_c = jnp.where(kb < kb0, far_ref[h, 0], far_ref[h, 1])
        out, alphas = [], []
        for mp in range(2):
            m = stats[mp]
            st = st_ref[slot, mp]
            if band:
                st = st + bt
                m_new = jnp.maximum(m, jnp.max(st, axis=0, keepdims=True))
                shift = m_new
            else:
                m_new = jnp.maximum(m, jnp.max(st, axis=0, keepdims=True) + far_c)
                shift = m_new - far_c
            p = jnp.exp2(st - shift)
            p_ref[slot, mp] = p.astype(BF16)
            out.append(m_new)
            alphas.append(jnp.exp2(m - m_new))
        return tuple(out), tuple(alphas)

    def values(j, slot, alphas):
        ks = pl.multiple_of(key_block(j) * tk, tk)
        vt = vt_ref[:, pl.ds(ks, tk)]
        for mp in range(2):
            acc_ref[mp] = alphas[mp] * acc_ref[mp] + jnp.dot(vt, p_ref[slot, mp],
                                                             preferred_element_type=F32)

    def group(jj, carry, base, band):
        stats, alphas = carry[:2], carry[2:]
        j = base + DIFF_GROUP * jj
        for u in range(DIFF_GROUP):
            scores(j + u + 1, (u + 1) % DIFF_GROUP)
            values(j + u - 1, (u - 1) % DIFF_GROUP, alphas)
            stats, alphas = softmax(j + u, u, stats, band)
        return stats + alphas

    last = DIFF_GROUP - 1
    acc_ref[...] = jnp.zeros_like(acc_ref)
    p_ref[last] = jnp.zeros_like(p_ref[last])
    scores(0, 0)
    ninf, one = (jnp.full((1, tq), v, F32) for v in (-jnp.inf, 1.0))
    carry = (ninf, ninf, one, one)
    carry = lax.fori_loop(0, n_band // DIFF_GROUP, functools.partial(group, base=0, band=True), carry)
    carry = lax.fori_loop(0, (nkb - n_band) // DIFF_GROUP,
                          functools.partial(group, base=n_band, band=False), carry)
    values(nkb - 1, last, carry[2:])
    l0, l1 = acc_ref[0, LANES:LANES + 1, :], acc_ref[1, LANES:LANES + 1, :]

    lp = lam_ref[...]
    lam = (jnp.exp(jnp.sum(lp[0:1] * lp[1:2], axis=-1, keepdims=True))
           - jnp.exp(jnp.sum(lp[2:3] * lp[3:4], axis=-1, keepdims=True)) + LAM_INIT)
    ot = acc_ref[0, :LANES, :] / l0 - lam * (acc_ref[1, :LANES, :] / l1)
    ms = jnp.mean(ot * ot, axis=0, keepdims=True)
    y = (ot * lax.rsqrt(ms + EPS)).T * g_ref[...] * (1.0 - LAM_INIT)
    o_ref[0] = y.astype(BF16)


def _diff_attn(nat, bias_tiles, far, lam_params, subln_g):
    b, s, _ = nat.shape
    tq, tk = min(TQ_DIFF, s), min(TK_DIFF, s)
    nt = bias_tiles.shape[1]
    ratio, nkb = tq // tk, s // tk
    assert tq % tk == 0 and tk >= MAX_DISTANCE and nkb >= ratio + 2
    assert (ratio + 2) % DIFF_GROUP == 0 and nkb % DIFF_GROUP == 0
    kern = functools.partial(_diff_attn_kernel, tq=tq, tk=tk, nkb=s // tk)
    return pl.pallas_call(
        kern,
        grid=(b, H_DIFF, s // tq),
        in_specs=[
            _const_spec((4, HEAD_DIM)),
            pl.BlockSpec(memory_space=pltpu.SMEM),
            pl.BlockSpec((1, tq, LANES), lambda bi, h, qi: (bi, qi, h)),
            pl.BlockSpec((1, s, LANES), lambda bi, h, qi: (bi, 0, H_DIFF + h)),
            pl.BlockSpec((1, s, LANES), lambda bi, h, qi: (bi, 0, 2 * H_DIFF + h)),
            pl.BlockSpec((1, nt, tk, tq), lambda bi, h, qi: (h, 0, 0, 0)),
            _const_spec((1, 2 * HEAD_DIM)),
        ],
        out_specs=pl.BlockSpec((1, tq, LANES), lambda bi, h, qi: (bi, qi, h)),
        out_shape=jax.ShapeDtypeStruct((b, s, DIFF_V), BF16),
        scratch_shapes=[
            pltpu.VMEM((LANES + DEN_ROWS, s), BF16),
            pltpu.VMEM((DIFF_GROUP, 2, tk, tq), F32),
            pltpu.VMEM((DIFF_GROUP, 2, tk, tq), BF16),
            pltpu.VMEM((2, LANES + DEN_ROWS, tq), F32),
        ],
        compiler_params=_params(("parallel", "parallel", "arbitrary")),
        name="diff_attn",
    )(lam_params, far, nat, nat, nat, bias_tiles, subln_g)


def _dil_geometry(s, window, dilation):
    l = s // dilation
    w = window // (2 * dilation)
    qb = min(DIL_QB, l)
    kw = min(qb + 2 * LANES, l)
    assert w <= LANES and l % qb == 0 and l % LANES == 0
    return l, w, qb, kw, l // qb


def _dil_bias_tiles(rel_bias, g, s, window, dilation):
    l, w, qb, kw, nb = _dil_geometry(s, window, dilation)
    c0 = H_DIFF + g * H_DIL
    c = (kw - 1) - jnp.arange(kw + qb - 1)
    shifts = []
    for n in (0, min(1, nb - 1), nb - 1):
        l0 = n * qb
        shifts.append(min(max(l0 - LANES, 0), l - kw) - l0)
    off = jnp.asarray(shifts)[:, None] + c[None, :]
    vec = _bias_of_rel(rel_bias[:, c0:c0 + H_DIL], off * dilation) * LOG2E
    vec = jnp.where((jnp.abs(off) <= w)[None], vec, NEG_INF)
    t = _toeplitz(vec, kw, qb)
    t = t.reshape(H_DIL // 2, 2, 3, kw, qb).transpose(2, 0, 3, 1, 4)
    return t.reshape(3, H_DIL // 2, kw, 2 * qb)


def _dil_attn_kernel(q_ref, k_ref, v_ref, bias_ref, o_ref, lse_ref, qt_ref, vt_ref, st_ref,
                     *, l, qb, kw, nb):
    chunk = min(512, l)
    for pair in range(H_DIL // 2):
        cs = slice(pair * LANES, (pair + 1) * LANES)
        for c in range(l // chunk):
            rs = slice(c * chunk, (c + 1) * chunk)
            qt_ref[pair, :, rs] = q_ref[0, 0, rs, cs].astype(F32).T.astype(BF16)
            vt_ref[pair, :LANES, rs] = v_ref[0, 0, rs, cs].astype(F32).T.astype(BF16)
        vt_ref[pair, LANES:, :] = jnp.ones((DEN_ROWS, l), BF16)
    first = lax.broadcasted_iota(jnp.int32, (LANES, qb), 0) < HEAD_DIM

    def window(n):
        l0 = pl.multiple_of(n * qb, qb)
        return l0, pl.multiple_of(jnp.clip(l0 - LANES, 0, l - kw), LANES)

    def scores(n, slot):
        l0, ws = window(n)
        for pair in range(H_DIL // 2):
            qt = qt_ref[pair, :, pl.ds(l0, qb)]
            q2 = jnp.concatenate([jnp.where(first, qt, jnp.zeros_like(qt)),
                                  jnp.where(first, jnp.zeros_like(qt), qt)], axis=1)
            kp = k_ref[0, 0, pl.ds(ws, kw), pair * LANES:(pair + 1) * LANES]
            st_ref[slot, pair] = jnp.dot(kp, q2, preferred_element_type=F32)

    def finish(n, slot):
        l0, ws = window(n)
        t = jnp.where(n == 0, 0, jnp.where(n == nb - 1, 2, 1))
        for pair in range(H_DIL // 2):
            cs = slice(pair * LANES, (pair + 1) * LANES)
            st = st_ref[slot, pair] + bias_ref[t, pair]
            mx = jnp.max(st, axis=0, keepdims=True)
            p = jnp.exp2(st - mx).astype(BF16)
            acc = jnp.dot(vt_ref[pair, :, pl.ds(ws, kw)], p, preferred_element_type=F32)
            den = acc[LANES:LANES + 1, :]
            ot = acc[:LANES, :] / den
            lse = mx * LN2 + jnp.log(den)
            o_t = jnp.where(first, ot[:, :qb], ot[:, qb:])
            lse_t = jnp.where(first, lse[:, :qb], lse[:, qb:])
            o_ref[0, 0, pl.ds(l0, qb), cs] = o_t.T.astype(BF16)
            lse_ref[0, 0, pl.ds(l0, qb), cs] = lse_t.T

    def two_blocks(i, carry):
        n = 2 * i
        scores(n + 1, 1)
        finish(n, 0)
        scores(n + 2, 0)
        finish(n + 1, 1)
        return carry

    scores(0, 0)
    lax.fori_loop(0, (nb - 1) // 2, two_blocks, 0)
    if nb % 2 == 0:
        scores(nb - 1, 1)
        finish(nb - 2, 0)
        finish(nb - 1, 1)
    else:
        finish(nb - 1, 0)


def _dil_attn(arr, col0, bias_tiles, g, window, dilation):
    b, d, l, _ = arr.shape
    _, w, qb, kw, nb = _dil_geometry(l * d, window, dilation)
    kern = functools.partial(_dil_attn_kernel, l=l, qb=qb, kw=kw, nb=nb)
    return pl.pallas_call(
        kern,
        grid=(b, d),
        in_specs=[
            pl.BlockSpec((1, 1, l, GRP_W), lambda bi, r: (bi, r, 0, col0)),
            pl.BlockSpec((1, 1, l, GRP_W), lambda bi, r: (bi, r, 0, col0 + 1)),
            pl.BlockSpec((1, 1, l, GRP_W), lambda bi, r: (bi, r, 0, col0 + 2)),
            _const_spec((3, H_DIL // 2, kw, 2 * qb)),
        ],
        out_specs=[
            pl.BlockSpec((1, 1, l, GRP_W), lambda bi, r: (bi, r, 0, 0)),
            pl.BlockSpec((1, 1, l, GRP_W), lambda bi, r: (bi, r, 0, 0)),
        ],
        out_shape=[
            jax.ShapeDtypeStruct((b, d, l, GRP_W), BF16),
            jax.ShapeDtypeStruct((b, d, l, GRP_W), F32),
        ],
        scratch_shapes=[
            pltpu.VMEM((H_DIL // 2, LANES, l), BF16),
            pltpu.VMEM((H_DIL // 2, LANES + DEN_ROWS, l), BF16),
            pltpu.VMEM((2, H_DIL // 2, kw, 2 * qb), F32),
        ],
        compiler_params=_params(("parallel", "parallel")),
        name=f"dil_attn_g{g}",
    )(arr, arr, arr, bias_tiles)


def _merge_route_kernel(x_ref, oa_ref, o0_ref, l0_ref, o1_ref, l1_ref, o2_ref, l2_ref, gt_ref,
                        wa_ref, wb_ref, wo_ref, nf_ref, wr_ref, x1_ref, h2_ref, dw_ref,
                        so1_ref, sl1_ref, so2_ref, sl2_ref, *, tm):
    for src, dst, d in ((o1_ref, so1_ref, DIL_CONFIGS[1][1]), (l1_ref, sl1_ref, DIL_CONFIGS[1][1]),
                        (o2_ref, so2_ref, DIL_CONFIGS[2][1]), (l2_ref, sl2_ref, DIL_CONFIGS[2][1])):
        for r in range(d):
            for c in range(GRP_W // LANES):
                dst[c, pl.ds(r, tm // d, stride=d), :] = src[0, r, :, c * LANES:(c + 1) * LANES].astype(F32)
    subs = [slice(i * SUB_MERGE, (i + 1) * SUB_MERGE) for i in range(tm // SUB_MERGE)]
    unsplit = lambda ref, rs: jnp.concatenate([ref[c, rs, :] for c in range(GRP_W // LANES)], axis=1)

    def dil_mix(rs):
        l0, l1, l2 = l0_ref[0, 0, rs, :], unsplit(sl1_ref, rs), unsplit(sl2_ref, rs)
        mx = jnp.maximum(jnp.maximum(l0, l1), l2)
        e0, e1, e2 = jnp.exp(l0 - mx), jnp.exp(l1 - mx), jnp.exp(l2 - mx)
        ob = (e0 * o0_ref[0, 0, rs, :].astype(F32) + e1 * unsplit(so1_ref, rs)
              + e2 * unsplit(so2_ref, rs)) / (e0 + e1 + e2)
        return ob.astype(BF16)

    obs = [dil_mix(rs) for rs in subs]
    pas = [jnp.dot(oa_ref[0, rs, :], wa_ref[...], preferred_element_type=F32) for rs in subs]
    pbs = [jnp.dot(ob, wb_ref[...], preferred_element_type=F32) for ob in obs]

    def gate(rs, pa, pb):
        ga = jax.nn.sigmoid(gt_ref[0, rs, :D_MODEL].astype(F32))
        gb = jax.nn.sigmoid(gt_ref[0, rs, D_MODEL:].astype(F32))
        return (ga * pa + gb * pb).astype(BF16)

    mgs = [gate(rs, pa, pb) for rs, pa, pb in zip(subs, pas, pbs)]
    x1s = [x_ref[0, rs, :] + jnp.dot(mg, wo_ref[...], preferred_element_type=F32)
           for rs, mg in zip(subs, mgs)]

    def ffn_norm(rs, x1):
        x1_ref[0, rs, :] = x1
        ms = jnp.mean(x1 * x1, axis=-1, keepdims=True)
        h2 = (x1 * lax.rsqrt(ms + EPS) * nf_ref[...]).astype(BF16)
        h2_ref[0, rs, :] = h2
        return h2

    h2s = [ffn_norm(rs, x1) for rs, x1 in zip(subs, x1s)]
    lgs = [jnp.dot(h2, wr_ref[...], preferred_element_type=F32) for h2 in h2s]

    def route(rs, lg):
        lane = lax.broadcasted_iota(jnp.int32, lg.shape, 1)
        big = jnp.int32(ROUTE_LANES)
        gl = jnp.where(lane < N_GROUPS, lg, -jnp.inf)
        gmax = jnp.max(gl, axis=-1, keepdims=True)
        gi = jnp.min(jnp.where(gl == gmax, lane, big), axis=-1, keepdims=True)
        gw = 1.0 / jnp.sum(jnp.exp(gl - gmax), axis=-1, keepdims=True)
        e_lo = EXP_LANE0 + E_PER_GROUP * gi
        es = jnp.where((lane >= e_lo) & (lane < e_lo + E_PER_GROUP), lg, -jnp.inf)
        v1 = jnp.max(es, axis=-1, keepdims=True)
        i1 = jnp.min(jnp.where(es == v1, lane, big), axis=-1, keepdims=True)
        es2 = jnp.where(lane == i1, -jnp.inf, es)
        v2 = jnp.max(es2, axis=-1, keepdims=True)
        i2 = jnp.min(jnp.where(es2 == v2, lane, big), axis=-1, keepdims=True)
        t = jnp.exp(v2 - v1)
        w1 = gw / (1.0 + t)
        w2 = w1 * t
        dw_ref[0, rs, :] = jnp.where(lane == i1, w1, 0.0) + jnp.where(lane == i2, w2, 0.0)

    for rs, lg in zip(subs, lgs):
        route(rs, lg)


def _merge_route(x, oa, dil, gates, wa, wb, wo, nf, wr):
    b, s, _ = x.shape
    tm = TM_MERGE
    d1, d2 = DIL_CONFIGS[1][1], DIL_CONFIGS[2][1]
    tok = lambda bi, i: (bi, i, 0)
    cls = lambda bi, i: (bi, 0, i, 0)
    (o0, s0), (o1, s1), (o2, s2) = dil
    return pl.pallas_call(
        functools.partial(_merge_route_kernel, tm=tm),
        grid=(b, s // tm),
        in_specs=[
            pl.BlockSpec((1, tm, D_MODEL), tok),
            pl.BlockSpec((1, tm, DIFF_V), tok),
            pl.BlockSpec((1, 1, tm, GRP_W), cls), pl.BlockSpec((1, 1, tm, GRP_W), cls),
            pl.BlockSpec((1, d1, tm // d1, GRP_W), cls), pl.BlockSpec((1, d1, tm // d1, GRP_W), cls),
            pl.BlockSpec((1, d2, tm // d2, GRP_W), cls), pl.BlockSpec((1, d2, tm // d2, GRP_W), cls),
            pl.BlockSpec((1, tm, GATE_COLS), tok),
            _const_spec((DIFF_V, D_MODEL)),
            _const_spec((GRP_W, D_MODEL)),
            _const_spec((D_MODEL, D_MODEL)),
            _const_spec((1, D_MODEL)),
            _const_spec((D_MODEL, ROUTE_LANES)),
        ],
        out_specs=[
            pl.BlockSpec((1, tm, D_MODEL), tok),
            pl.BlockSpec((1, tm, D_MODEL), tok),
            pl.BlockSpec((1, tm, ROUTE_LANES), tok),
        ],
        out_shape=[
            jax.ShapeDtypeStruct((b, s, D_MODEL), F32),
            jax.ShapeDtypeStruct((b, s, D_MODEL), BF16),
            jax.ShapeDtypeStruct((b, s, ROUTE_LANES), F32),
        ],
        scratch_shapes=[pltpu.VMEM((GRP_W // LANES, tm, LANES), F32) for _ in range(4)],
        compiler_params=_params(("parallel", "parallel")),
        name="merge_route",
    )(x, oa, o0, s0, o1, s1, o2, s2, gates, wa, wb, wo, nf, wr)


def _moe_final_kernel(x1_ref, h2_ref, dw_ref, wg_ref, wu_ref, wd_ref, nf_ref, y_ref, he_ref):
    h2 = h2_ref[...]
    dw = dw_ref[...]
    for e in range(N_EXPERTS):
        gp = jnp.dot(h2, wg_ref[e], preferred_element_type=F32)
        up = jnp.dot(h2, wu_ref[e], preferred_element_type=F32)
        we = dw[:, EXP_LANE0 + e:EXP_LANE0 + e + 1]
        he_ref[:, e * D_EXPERT:(e + 1) * D_EXPERT] = (jax.nn.silu(gp) * up * we).astype(BF16)
    moe = jnp.dot(he_ref[...], wd_ref[...], preferred_element_type=F32)
    x2 = x1_ref[...] + moe
    ms = jnp.mean(x2 * x2, axis=-1, keepdims=True)
    y_ref[...] = x2 * lax.rsqrt(ms + EPS) * nf_ref[...]


def _moe_final(x1, h2, dw, wg, wu, wd, nf):
    n = x1.shape[0]
    tm = TM_MOE
    row = lambda i: (i, 0)
    return pl.pallas_call(
        _moe_final_kernel,
        grid=(n // tm,),
        in_specs=[
            pl.BlockSpec((tm, D_MODEL), row),
            pl.BlockSpec((tm, D_MODEL), row),
            pl.BlockSpec((tm, ROUTE_LANES), row),
            _const_spec((N_EXPERTS, D_MODEL, D_EXPERT)),
            _const_spec((N_EXPERTS, D_MODEL, D_EXPERT)),
            _const_spec((N_EXPERTS * D_EXPERT, D_MODEL)),
            _const_spec((1, D_MODEL)),
        ],
        out_specs=pl.BlockSpec((tm, D_MODEL), row),
        out_shape=jax.ShapeDtypeStruct((n, D_MODEL), F32),
        scratch_shapes=[pltpu.VMEM((tm, N_EXPERTS * D_EXPERT), BF16)],
        compiler_params=_params(("parallel",)),
        name="moe_final",
    )(x1, h2, dw, wg, wu, wd, nf)


def _prepare_weights(rel_bias, norm_mix, w_in, lam_q1, lam_k1, lam_q2, lam_k2, subln_g,
                     w_branch_a, w_branch_b, w_out, norm_ffn, w_group, w_router,
                     w_exp_gate, w_exp_up, w_exp_down, norm_final):
    w = w_in[0]
    c = QK_SCALE * LOG2E
    o = 0
    qa, o = w[:, o:o + DIFF_QK] * c, o + DIFF_QK
    ka, o = w[:, o:o + DIFF_QK], o + DIFF_QK
    va, o = w[:, o:o + DIFF_V], o + DIFF_V
    qb, o = w[:, o:o + DIL_W] * c, o + DIL_W
    kb, o = w[:, o:o + DIL_W], o + DIL_W
    vb, o = w[:, o:o + DIL_W], o + DIL_W
    w_gate = w[:, o:].astype(BF16)
    grp = lambda g: [t[:, g * GRP_W:(g + 1) * GRP_W] for t in (qb, kb, vb)]
    w_nat = jnp.concatenate([qa, ka, va] + grp(0), axis=1).astype(BF16)
    w_d1 = jnp.concatenate(grp(1), axis=1).astype(BF16)
    w_d2 = jnp.concatenate(grp(2), axis=1).astype(BF16)
    wr = jnp.concatenate([w_group[0], w_router[0].transpose(1, 0, 2).reshape(D_MODEL, N_EXPERTS)], axis=1)
    wr = jnp.pad(wr, ((0, 0), (0, ROUTE_LANES - wr.shape[1]))).astype(BF16)
    return dict(
        rel_bias=rel_bias.astype(F32),
        norm_mix=norm_mix[0][None].astype(F32),
        w_nat=w_nat, w_gate=w_gate, w_d1=w_d1, w_d2=w_d2,
        lam=jnp.stack([lam_q1[0], lam_k1[0], lam_q2[0], lam_k2[0]]).astype(F32),
        subln_g=subln_g[0][None].astype(F32),
        wa=w_branch_a[0].astype(BF16), wb=w_branch_b[0].astype(BF16), wo=w_out[0].astype(BF16),
        norm_ffn=norm_ffn[0][None].astype(F32), wr=wr,
        wg=w_exp_gate[0].astype(BF16), wu=w_exp_up[0].astype(BF16),
        wd=w_exp_down[0].reshape(N_EXPERTS * D_EXPERT, D_MODEL).astype(BF16),
        norm_final=norm_final[None].astype(F32),
    )


def _encoder(x, p):
    b, s, d = x.shape
    n = b * s
    nat, gates, dil1, dil2 = _norm_proj(x, p["norm_mix"], p["w_nat"], p["w_gate"], p["w_d1"], p["w_d2"])
    tiles, far = _diff_bias_tiles(p["rel_bias"], min(TQ_DIFF, s), min(TK_DIFF, s))
    oa = _diff_attn(nat, tiles, far, p["lam"], p["subln_g"])
    srcs = ((nat.reshape(b, 1, s, NAT_COLS), (2 * DIFF_QK + DIFF_V) // GRP_W), (dil1, 0), (dil2, 0))
    dil = []
    for g, (win, dilation) in enumerate(DIL_CONFIGS):
        tiles = _dil_bias_tiles(p["rel_bias"], g, s, win, dilation)
        dil.append(_dil_attn(srcs[g][0], srcs[g][1], tiles, g, win, dilation))
    x1, h2, dw = _merge_route(x, oa, dil, gates, p["wa"], p["wb"], p["wo"], p["norm_ffn"], p["wr"])
    y = _moe_final(x1.reshape(n, d), h2.reshape(n, d), dw.reshape(n, ROUTE_LANES),
                   p["wg"], p["wu"], p["wd"], p["norm_final"])
    return y.reshape(b, s, d)


def kernel(x_prompt, x_sample, rel_bias, norm_mix, w_in, lam_q1, lam_k1, lam_q2, lam_k2, subln_g,
           w_branch_a, w_branch_b, w_out, norm_ffn, w_group, w_router, w_exp_gate, w_exp_up,
           w_exp_down, norm_final):
    p = _prepare_weights(rel_bias, norm_mix, w_in, lam_q1, lam_k1, lam_q2, lam_k2, subln_g,
                         w_branch_a, w_branch_b, w_out, norm_ffn, w_group, w_router,
                         w_exp_gate, w_exp_up, w_exp_down, norm_final)
    return (_encoder(x_prompt, p), _encoder(x_sample, p))
```

```python
import functools
import math

import jax
import jax.numpy as jnp
from jax import lax
from jax.experimental import pallas as pl
from jax.experimental.pallas import tpu as pltpu

F32 = jnp.float32
BF16 = jnp.bfloat16

D_MODEL = 1024
HEAD_DIM = 64
H_DIFF = 4
H_DIL = 4
DIL_CONFIGS = ((128, 1), (512, 4), (2048, 16))
N_DIL = len(DIL_CONFIGS)
DIFF_QK = H_DIFF * 2 * HEAD_DIM
DIFF_V = H_DIFF * 2 * HEAD_DIM
DIL_W = N_DIL * H_DIL * HEAD_DIM
GRP_W = H_DIL * HEAD_DIM
GRP_COLS = 3 * GRP_W
NAT_COLS = 2 * DIFF_QK + DIFF_V + GRP_COLS
GATE_COLS = 2 * D_MODEL
Q_BLOCK = 128
N_BUCKETS = 32
MAX_DISTANCE = 128
N_GROUPS = 4
E_PER_GROUP = 4
N_EXPERTS = N_GROUPS * E_PER_GROUP
D_EXPERT = 256
EPS = 1e-6
NEG_INF = -1e30
LAM_INIT = 0.8 - 0.6 * math.exp(-0.3 * 0)
LOG2E = math.log2(math.e)
LN2 = math.log(2.0)
QK_SCALE = HEAD_DIM ** -0.5

LANES = 128
VMEM_LIMIT = 56 * 1024 * 1024

TM_PROJ = 512
TQ_DIFF = 512
TK_DIFF = 256
DIFF_GROUP = 4
DEN_ROWS = 16
DIL_QB = 256
TM_MERGE = 512
SUB_MERGE = 256
TM_MOE = 512
ROUTE_LANES = 128
EXP_LANE0 = N_GROUPS


def _const_spec(shape):
    nd = len(shape)
    return pl.BlockSpec(shape, lambda *_: (0,) * nd, pipeline_mode=pl.Buffered(1))


def _params(sem):
    return pltpu.CompilerParams(dimension_semantics=sem, vmem_limit_bytes=VMEM_LIMIT)


def _rel_bucket(rel):
    half = N_BUCKETS // 2
    max_exact = half // 2
    ret = (rel > 0).astype(jnp.int32) * half
    n = jnp.abs(rel)
    nf = jnp.maximum(n, 1).astype(F32)
    large = max_exact + (jnp.log(nf / max_exact) / math.log(MAX_DISTANCE / max_exact)
                         * (half - max_exact)).astype(jnp.int32)
    large = jnp.minimum(large, half - 1)
    return ret + jnp.where(n < max_exact, n, large)


def _bias_of_rel(table, rel):
    bucket = _rel_bucket(rel)[None]
    t = table.astype(F32)
    col = (t.shape[1],) + (1,) * rel.ndim
    out = jnp.zeros((t.shape[1],) + rel.shape, F32)
    for k in range(N_BUCKETS):
        out = jnp.where(bucket == k, t[k].reshape(col), out)
    return out


def _chunks(total, width):
    out, c = [], 0
    while c < total:
        w = min(width, total - c)
        out.append((c, w))
        c += w
    return out


def _norm_proj_kernel(x_ref, g_ref, wn_ref, wg_ref, w1_ref, w2_ref,
                      on_ref, og_ref, o1_ref, o2_ref, hs_ref, *, tm):
    x = x_ref[0]
    ms = jnp.mean(x * x, axis=-1, keepdims=True)
    hf = x * lax.rsqrt(ms + EPS) * g_ref[...]
    for c in range(D_MODEL // LANES):
        hs_ref[c] = hf[:, c * LANES:(c + 1) * LANES]
    h = hf.astype(BF16)
    for c0, cw in _chunks(NAT_COLS, 512):
        on_ref[0, :, c0:c0 + cw] = jnp.dot(h, wn_ref[:, c0:c0 + cw],
                                           preferred_element_type=F32).astype(BF16)
    for c0, cw in _chunks(GATE_COLS, 512):
        og_ref[0, :, c0:c0 + cw] = jnp.dot(h, wg_ref[:, c0:c0 + cw],
                                           preferred_element_type=F32).astype(BF16)
    for w_ref, o_ref, d in ((w1_ref, o1_ref, DIL_CONFIGS[1][1]), (w2_ref, o2_ref, DIL_CONFIGS[2][1])):
        rows = tm // d
        hp = jnp.concatenate(
            [jnp.concatenate([hs_ref[c, pl.ds(r, rows, stride=d), :] for r in range(d)], axis=0)
             for c in range(D_MODEL // LANES)], axis=1).astype(BF16)
        for c0, cw in _chunks(GRP_COLS, 256):
            res = jnp.dot(hp, w_ref[:, c0:c0 + cw], preferred_element_type=F32).astype(BF16)
            o_ref[0, :, :, c0:c0 + cw] = res.reshape(d, rows, cw)


def _norm_proj(x, g, w_nat, w_gate, w_d1, w_d2):
    b, s, _ = x.shape
    tm = TM_PROJ
    d1, d2 = DIL_CONFIGS[1][1], DIL_CONFIGS[2][1]
    return pl.pallas_call(
        functools.partial(_norm_proj_kernel, tm=tm),
        grid=(b, s // tm),
        in_specs=[
            pl.BlockSpec((1, tm, D_MODEL), lambda bi, i: (bi, i, 0)),
            _const_spec((1, D_MODEL)),
            _const_spec((D_MODEL, NAT_COLS)),
            _const_spec((D_MODEL, GATE_COLS)),
            _const_spec((D_MODEL, GRP_COLS)),
            _const_spec((D_MODEL, GRP_COLS)),
        ],
        out_specs=[
            pl.BlockSpec((1, tm, NAT_COLS), lambda bi, i: (bi, i, 0)),
            pl.BlockSpec((1, tm, GATE_COLS), lambda bi, i: (bi, i, 0)),
            pl.BlockSpec((1, d1, tm // d1, GRP_COLS), lambda bi, i: (bi, 0, i, 0)),
            pl.BlockSpec((1, d2, tm // d2, GRP_COLS), lambda bi, i: (bi, 0, i, 0)),
        ],
        out_shape=[
            jax.ShapeDtypeStruct((b, s, NAT_COLS), BF16),
            jax.ShapeDtypeStruct((b, s, GATE_COLS), BF16),
            jax.ShapeDtypeStruct((b, d1, s // d1, GRP_COLS), BF16),
            jax.ShapeDtypeStruct((b, d2, s // d2, GRP_COLS), BF16),
        ],
        scratch_shapes=[pltpu.VMEM((D_MODEL // LANES, tm, LANES), F32)],
        compiler_params=_params(("parallel", "parallel")),
        name="norm_proj",
    )(x, g, w_nat, w_gate, w_d1, w_d2)


def _diff_bias_tiles(rel_bias, tq, tk):
    ratio = tq // tk
    t = jnp.arange(-2, ratio + 2)[:, None, None]
    j = jnp.arange(tk)[None, :, None]
    i = jnp.arange(tq)[None, None, :]
    tiles = _bias_of_rel(rel_bias[:, :H_DIFF], t * tk + j - i) * LOG2E
    far = _bias_of_rel(rel_bias[:, :H_DIFF], jnp.asarray([-MAX_DISTANCE, MAX_DISTANCE])) * LOG2E
    return tiles, far


def _diff_attn_kernel(lam_ref, far_ref, q_ref, k_ref, v_ref, bias_ref, g_ref, o_ref,
                      vt_ref, st_ref, p_ref, acc_ref, *, tq, tk, nkb):
    h = pl.program_id(1)
    qi = pl.program_id(2)
    ratio = tq // tk
    n_band = ratio + 2
    vchunk = min(512, nkb * tk)

    @pl.when(qi == 0)
    def _():
        for c in range(nkb * tk // vchunk):
            cs = slice(c * vchunk, (c + 1) * vchunk)
            vt_ref[:LANES, cs] = v_ref[0, cs, :].astype(F32).T.astype(BF16)
        vt_ref[LANES:, :] = jnp.ones((DEN_ROWS, nkb * tk), BF16)

    qt = q_ref[0].astype(F32).T
    first = lax.broadcasted_iota(jnp.int32, (LANES, tq), 0) < HEAD_DIM
    qts = (jnp.where(first, qt, 0.0).astype(BF16), jnp.where(first, 0.0, qt).astype(BF16))
    kb0 = qi * ratio

    def key_block(j):
        kb = kb0 - 1 + j
        return jnp.where(kb < 0, kb + nkb, jnp.where(kb >= nkb, kb - nkb, kb))

    def scores(j, slot, band):
        kb = key_block(j)
        ks = pl.multiple_of(kb * tk, tk)
        k = k_ref[0, pl.ds(ks, tk), :]
        if band:
            bt = bias_ref[0, jnp.clip(kb - kb0, -2, ratio + 1) + 2]
        mx = []
        for mp in range(2):
            st = jnp.dot(k, qts[mp], preferred_element_type=F32)
            if band:
                st = st + bt
            st_ref[slot, mp] = st
            mx.append(jnp.max(st, axis=0, keepdims=True))
        return tuple(mx)

    def softmax(j, slot, stats, mx, band):
        if not band:
            kb = key_block(j)
            far_c = jnp.where(kb < kb0, far_ref[h, 0], far_ref[h, 1])
        out, alphas = [], []
        for mp in range(2):
            m = stats[mp]
            if band:
                m_new = jnp.maximum(m, mx[mp])
                shift = m_new
            else:
                m_new = jnp.maximum(m, mx[mp] + far_c)
                shift = m_new - far_c
            p_ref[slot, mp] = jnp.exp2((st_ref[slot, mp] - shift).astype(BF16))
            out.append(m_new)
            alphas.append(jnp.exp2(m - m_new))
        return tuple(out), tuple(alphas)

    def values(j, slot, alphas):
        ks = pl.multiple_of(key_block(j) * tk, tk)
        vt = vt_ref[:, pl.ds(ks, tk)]
        for mp in range(2):
            acc_ref[mp] = alphas[mp] * acc_ref[mp] + jnp.dot(vt, p_ref[slot, mp],
                                                             preferred_element_type=F32)

    def group(jj, carry, base, band):
        stats, alphas, mx = carry[:2], carry[2:4], carry[4:]
        j = base + DIFF_GROUP * jj
        for u in range(DIFF_GROUP):
            mx_next = scores(j + u + 1, (u + 1) % DIFF_GROUP, band and u + 1 < DIFF_GROUP)
            values(j + u - 1, (u - 1) % DIFF_GROUP, alphas)
            stats, alphas = softmax(j + u, u, stats, mx, band)
            mx = mx_next
        return stats + alphas + mx

    last = DIFF_GROUP - 1
    acc_ref[...] = jnp.zeros_like(acc_ref)
    p_ref[last] = jnp.zeros_like(p_ref[last])
    mx0 = scores(0, 0, True)
    ninf, one = (jnp.full((1, tq), v, F32) for v in (-jnp.inf, 1.0))
    carry = group(0, (ninf, ninf, one, one) + mx0, base=0, band=True)
    carry = lax.fori_loop(0, (nkb - n_band) // DIFF_GROUP,
                          functools.partial(group, base=n_band, band=False), carry)
    values(nkb - 1, last, carry[2:4])
    l0, l1 = acc_ref[0, LANES:LANES + 1, :], acc_ref[1, LANES:LANES + 1, :]

    lp = lam_ref[...]
    lam = (jnp.exp(jnp.sum(lp[0:1] * lp[1:2], axis=-1, keepdims=True))
           - jnp.exp(jnp.sum(lp[2:3] * lp[3:4], axis=-1, keepdims=True)) + LAM_INIT)
    ot = acc_ref[0, :LANES, :] / l0 - lam * (acc_ref[1, :LANES, :] / l1)
    ms = jnp.mean(ot * ot, axis=0, keepdims=True)
    y = (ot * lax.rsqrt(ms + EPS)).T * g_ref[...] * (1.0 - LAM_INIT)
    o_ref[0] = y.astype(BF16)


def _diff_attn(nat, bias_tiles, far, lam_params, subln_g):
    b, s, _ = nat.shape
    tq, tk = min(TQ_DIFF, s), min(TK_DIFF, s)
    nt = bias_tiles.shape[1]
    ratio, nkb = tq // tk, s // tk
    assert tq % tk == 0 and tk >= MAX_DISTANCE and nkb >= ratio + 2
    assert ratio + 2 == DIFF_GROUP and nkb % DIFF_GROUP == 0
    kern = functools.partial(_diff_attn_kernel, tq=tq, tk=tk, nkb=s // tk)
    return pl.pallas_call(
        kern,
        grid=(b, H_DIFF, s // tq),
        in_specs=[
            _const_spec((4, HEAD_DIM)),
            pl.BlockSpec(memory_space=pltpu.SMEM),
            pl.BlockSpec((1, tq, LANES), lambda bi, h, qi: (bi, qi, h)),
            pl.BlockSpec((1, s, LANES), lambda bi, h, qi: (bi, 0, H_DIFF + h)),
            pl.BlockSpec((1, s, LANES), lambda bi, h, qi: (bi, 0, 2 * H_DIFF + h)),
            pl.BlockSpec((1, nt, tk, tq), lambda bi, h, qi: (h, 0, 0, 0)),
            _const_spec((1, 2 * HEAD_DIM)),
        ],
        out_specs=pl.BlockSpec((1, tq, LANES), lambda bi, h, qi: (bi, qi, h)),
        out_shape=jax.ShapeDtypeStruct((b, s, DIFF_V), BF16),
        scratch_shapes=[
            pltpu.VMEM((LANES + DEN_ROWS, s), BF16),
            pltpu.VMEM((DIFF_GROUP, 2, tk, tq), F32),
            pltpu.VMEM((DIFF_GROUP, 2, tk, tq), BF16),
            pltpu.VMEM((2, LANES + DEN_ROWS, tq), F32),
        ],
        compiler_params=_params(("parallel", "parallel", "arbitrary")),
        name="diff_attn",
    )(lam_params, far, nat, nat, nat, bias_tiles, subln_g)


def _dil_geometry(s, window, dilation):
    l = s // dilation
    w = window // (2 * dilation)
    qb = min(DIL_QB, l)
    kw = min(qb + 2 * LANES, l)
    assert w <= LANES and l % qb == 0 and l % LANES == 0
    return l, w, qb, kw, l // qb


def _dil_bias_tiles(rel_bias, g, s, window, dilation):
    l, w, qb, kw, nb = _dil_geometry(s, window, dilation)
    c0 = H_DIFF + g * H_DIL
    shifts = []
    for n in (0, min(1, nb - 1), nb - 1):
        l0 = n * qb
        shifts.append(min(max(l0 - LANES, 0), l - kw) - l0)
    j = jnp.arange(kw)[None, :, None]
    i = jnp.arange(qb)[None, None, :]
    off = jnp.asarray(shifts)[:, None, None] + j - i
    b = _bias_of_rel(rel_bias[:, c0:c0 + H_DIL], off * dilation) * LOG2E
    b = jnp.where((jnp.abs(off) <= w)[None], b, NEG_INF)
    return jnp.stack([jnp.concatenate([b[2 * p], b[2 * p + 1]], axis=-1) for p in range(H_DIL // 2)], axis=1)


def _dil_attn_kernel(q_ref, k_ref, v_ref, bias_ref, o_ref, lse_ref, qt_ref, vt_ref, st_ref,
                     *, l, qb, kw, nb):
    chunk = min(512, l)
    for pair in range(H_DIL // 2):
        cs = slice(pair * LANES, (pair + 1) * LANES)
        for c in range(l // chunk):
            rs = slice(c * chunk, (c + 1) * chunk)
            qt_ref[pair, :, rs] = q_ref[0, 0, rs, cs].astype(F32).T.astype(BF16)
            vt_ref[pair, :LANES, rs] = v_ref[0, 0, rs, cs].astype(F32).T.astype(BF16)
        vt_ref[pair, LANES:, :] = jnp.ones((DEN_ROWS, l), BF16)
    first = lax.broadcasted_iota(jnp.int32, (LANES, qb), 0) < HEAD_DIM

    def window(n):
        l0 = pl.multiple_of(n * qb, qb)
        return l0, pl.multiple_of(jnp.clip(l0 - LANES, 0, l - kw), LANES)

    def scores(n, slot):
        l0, ws = window(n)
        for pair in range(H_DIL // 2):
            qt = qt_ref[pair, :, pl.ds(l0, qb)]
            q2 = jnp.concatenate([jnp.where(first, qt, jnp.zeros_like(qt)),
                                  jnp.where(first, jnp.zeros_like(qt), qt)], axis=1)
            kp = k_ref[0, 0, pl.ds(ws, kw), pair * LANES:(pair + 1) * LANES]
            st_ref[slot, pair] = jnp.dot(kp, q2, preferred_element_type=F32)

    def finish(n, slot):
        l0, ws = window(n)
        t = jnp.where(n == 0, 0, jnp.where(n == nb - 1, 2, 1))
        for pair in range(H_DIL // 2):
            cs = slice(pair * LANES, (pair + 1) * LANES)
            st = st_ref[slot, pair] + bias_ref[t, pair]
            mx = jnp.max(st, axis=0, keepdims=True)
            p = jnp.exp2(st - mx).astype(BF16)
            acc = jnp.dot(vt_ref[pair, :, pl.ds(ws, kw)], p, preferred_element_type=F32)
            den = acc[LANES:LANES + 1, :]
            ot = acc[:LANES, :] / den
            lse = mx * LN2 + jnp.log(den)
            o_t = jnp.where(first, ot[:, :qb], ot[:, qb:])
            lse_t = jnp.where(first, lse[:, :qb], lse[:, qb:])
            o_ref[0, 0, pl.ds(l0, qb), cs] = o_t.T.astype(BF16)
            lse_ref[0, 0, pl.ds(l0, qb), cs] = lse_t.T

    def two_blocks(i, carry):
        n = 2 * i
        scores(n + 1, 1)
        finish(n, 0)
        scores(n + 2, 0)
        finish(n + 1, 1)
        return carry

    scores(0, 0)
    lax.fori_loop(0, (nb - 1) // 2, two_blocks, 0)
    if nb % 2 == 0:
        scores(nb - 1, 1)
        finish(nb - 2, 0)
        finish(nb - 1, 1)
    else:
        finish(nb - 1, 0)


def _dil_attn(arr, col0, bias_tiles, g, window, dilation):
    b, d, l, _ = arr.shape
    _, w, qb, kw, nb = _dil_geometry(l * d, window, dilation)
    kern = functools.partial(_dil_attn_kernel, l=l, qb=qb, kw=kw, nb=nb)
    return pl.pallas_call(
        kern,
        grid=(b, d),
        in_specs=[
            pl.BlockSpec((1, 1, l, GRP_W), lambda bi, r: (bi, r, 0, col0)),
            pl.BlockSpec((1, 1, l, GRP_W), lambda bi, r: (bi, r, 0, col0 + 1)),
            pl.BlockSpec((1, 1, l, GRP_W), lambda bi, r: (bi, r, 0, col0 + 2)),
            _const_spec((3, H_DIL // 2, kw, 2 * qb)),
        ],
        out_specs=[
            pl.BlockSpec((1, 1, l, GRP_W), lambda bi, r: (bi, r, 0, 0)),
            pl.BlockSpec((1, 1, l, GRP_W), lambda bi, r: (bi, r, 0, 0)),
        ],
        out_shape=[
            jax.ShapeDtypeStruct((b, d, l, GRP_W), BF16),
            jax.ShapeDtypeStruct((b, d, l, GRP_W), F32),
        ],
        scratch_shapes=[
            pltpu.VMEM((H_DIL // 2, LANES, l), BF16),
            pltpu.VMEM((H_DIL // 2, LANES + DEN_ROWS, l), BF16),
            pltpu.VMEM((2, H_DIL // 2, kw, 2 * qb), F32),
        ],
        compiler_params=_params(("parallel", "parallel")),
        name=f"dil_attn_g{g}",
    )(arr, arr, arr, bias_tiles)


def _merge_route_kernel(x_ref, oa_ref, o0_ref, l0_ref, o1_ref, l1_ref, o2_ref, l2_ref, gt_ref,
                        wa_ref, wb_ref, wo_ref, nf_ref, wr_ref, x1_ref, h2_ref, dw_ref,
                        so1_ref, sl1_ref, so2_ref, sl2_ref, *, tm):
    for src, dst, d in ((o1_ref, so1_ref, DIL_CONFIGS[1][1]), (l1_ref, sl1_ref, DIL_CONFIGS[1][1]),
                        (o2_ref, so2_ref, DIL_CONFIGS[2][1]), (l2_ref, sl2_ref, DIL_CONFIGS[2][1])):
        for r in range(d):
            for c in range(GRP_W // LANES):
                dst[c, pl.ds(r, tm // d, stride=d), :] = src[0, r, :, c * LANES:(c + 1) * LANES].astype(F32)
    subs = [slice(i * SUB_MERGE, (i + 1) * SUB_MERGE) for i in range(tm // SUB_MERGE)]
    unsplit = lambda ref, rs: jnp.concatenate([ref[c, rs, :] for c in range(GRP_W // LANES)], axis=1)

    def dil_mix(rs):
        l0, l1, l2 = l0_ref[0, 0, rs, :], unsplit(sl1_ref, rs), unsplit(sl2_ref, rs)
        mx = jnp.maximum(jnp.maximum(l0, l1), l2)
        e0, e1, e2 = jnp.exp(l0 - mx), jnp.exp(l1 - mx), jnp.exp(l2 - mx)
        ob = (e0 * o0_ref[0, 0, rs, :].astype(F32) + e1 * unsplit(so1_ref, rs)
              + e2 * unsplit(so2_ref, rs)) / (e0 + e1 + e2)
        return ob.astype(BF16)

    obs = [dil_mix(rs) for rs in subs]
    pas = [jnp.dot(oa_ref[0, rs, :], wa_ref[...], preferred_element_type=F32) for rs in subs]
    pbs = [jnp.dot(ob, wb_ref[...], preferred_element_type=F32) for ob in obs]

    def gate(rs, pa, pb):
        ga = jax.nn.sigmoid(gt_ref[0, rs, :D_MODEL].astype(F32))
        gb = jax.nn.sigmoid(gt_ref[0, rs, D_MODEL:].astype(F32))
        return (ga * pa + gb * pb).astype(BF16)

    mgs = [gate(rs, pa, pb) for rs, pa, pb in zip(subs, pas, pbs)]
    x1s = [x_ref[0, rs, :] + jnp.dot(mg, wo_ref[...], preferred_element_type=F32)
           for rs, mg in zip(subs, mgs)]

    def ffn_norm(rs, x1):
        x1_ref[0, rs, :] = x1
        ms = jnp.mean(x1 * x1, axis=-1, keepdims=True)
        h2 = (x1 * lax.rsqrt(ms + EPS) * nf_ref[...]).astype(BF16)
        h2_ref[0, rs, :] = h2
        return h2

    h2s = [ffn_norm(rs, x1) for rs, x1 in zip(subs, x1s)]
    lgs = [jnp.dot(h2, wr_ref[...], preferred_element_type=F32) for h2 in h2s]

    def route(rs, lg):
        lane = lax.broadcasted_iota(jnp.int32, lg.shape, 1)
        big = jnp.int32(ROUTE_LANES)
        gl = jnp.where(lane < N_GROUPS, lg, -jnp.inf)
        gmax = jnp.max(gl, axis=-1, keepdims=True)
        gi = jnp.min(jnp.where(gl == gmax, lane, big), axis=-1, keepdims=True)
        gw = 1.0 / jnp.sum(jnp.exp(gl - gmax), axis=-1, keepdims=True)
        e_lo = EXP_LANE0 + E_PER_GROUP * gi
        es = jnp.where((lane >= e_lo) & (lane < e_lo + E_PER_GROUP), lg, -jnp.inf)
        v1 = jnp.max(es, axis=-1, keepdims=True)
        i1 = jnp.min(jnp.where(es == v1, lane, big), axis=-1, keepdims=True)
        es2 = jnp.where(lane == i1, -jnp.inf, es)
        v2 = jnp.max(es2, axis=-1, keepdims=True)
        i2 = jnp.min(jnp.where(es2 == v2, lane, big), axis=-1, keepdims=True)
        t = jnp.exp(v2 - v1)
        w1 = gw / (1.0 + t)
        w2 = w1 * t
        dw_ref[0, rs, :] = jnp.where(lane == i1, w1, 0.0) + jnp.where(lane == i2, w2, 0.0)

    for rs, lg in zip(subs, lgs):
        route(rs, lg)


def _merge_route(x, oa, dil, gates, wa, wb, wo, nf, wr):
    b, s, _ = x.shape
    tm = TM_MERGE
    d1, d2 = DIL_CONFIGS[1][1], DIL_CONFIGS[2][1]
    tok = lambda bi, i: (bi, i, 0)
    cls = lambda bi, i: (bi, 0, i, 0)
    (o0, s0), (o1, s1), (o2, s2) = dil
    return pl.pallas_call(
        functools.partial(_merge_route_kernel, tm=tm),
        grid=(b, s // tm),
        in_specs=[
            pl.BlockSpec((1, tm, D_MODEL), tok),
            pl.BlockSpec((1, tm, DIFF_V), tok),
            pl.BlockSpec((1, 1, tm, GRP_W), cls), pl.BlockSpec((1, 1, tm, GRP_W), cls),
            pl.BlockSpec((1, d1, tm // d1, GRP_W), cls), pl.BlockSpec((1, d1, tm // d1, GRP_W), cls),
            pl.BlockSpec((1, d2, tm // d2, GRP_W), cls), pl.BlockSpec((1, d2, tm // d2, GRP_W), cls),
            pl.BlockSpec((1, tm, GATE_COLS), tok),
            _const_spec((DIFF_V, D_MODEL)),
            _const_spec((GRP_W, D_MODEL)),
            _const_spec((D_MODEL, D_MODEL)),
            _const_spec((1, D_MODEL)),
            _const_spec((D_MODEL, ROUTE_LANES)),
        ],
        out_specs=[
            pl.BlockSpec((1, tm, D_MODEL), tok),
            pl.BlockSpec((1, tm, D_MODEL), tok),
            pl.BlockSpec((1, tm, ROUTE_LANES), tok),
        ],
        out_shape=[
            jax.ShapeDtypeStruct((b, s, D_MODEL), F32),
            jax.ShapeDtypeStruct((b, s, D_MODEL), BF16),
            jax.ShapeDtypeStruct((b, s, ROUTE_LANES), F32),
        ],
        scratch_shapes=[pltpu.VMEM((GRP_W // LANES, tm, LANES), F32) for _ in range(4)],
        compiler_params=_params(("parallel", "parallel")),
        name="merge_route",
    )(x, oa, o0, s0, o1, s1, o2, s2, gates, wa, wb, wo, nf, wr)


def _moe_final_kernel(x1_ref, h2_ref, dw_ref, wg_ref, wu_ref, wd_ref, nf_ref, y_ref, he_ref):
    h2 = h2_ref[...]
    dw = dw_ref[...]
    for e in range(N_EXPERTS):
        gp = jnp.dot(h2, wg_ref[e], preferred_element_type=F32)
        up = jnp.dot(h2, wu_ref[e], preferred_element_type=F32)
        we = dw[:, EXP_LANE0 + e:EXP_LANE0 + e + 1]
        he_ref[:, e * D_EXPERT:(e + 1) * D_EXPERT] = (jax.nn.silu(gp) * up * we).astype(BF16)
    moe = jnp.dot(he_ref[...], wd_ref[...], preferred_element_type=F32)
    x2 = x1_ref[...] + moe
    ms = jnp.mean(x2 * x2, axis=-1, keepdims=True)
    y_ref[...] = x2 * lax.rsqrt(ms + EPS) * nf_ref[...]


def _moe_final(x1, h2, dw, wg, wu, wd, nf):
    n = x1.shape[0]
    tm = TM_MOE
    row = lambda i: (i, 0)
    return pl.pallas_call(
        _moe_final_kernel,
        grid=(n // tm,),
        in_specs=[
            pl.BlockSpec((tm, D_MODEL), row),
            pl.BlockSpec((tm, D_MODEL), row),
            pl.BlockSpec((tm, ROUTE_LANES), row),
            _const_spec((N_EXPERTS, D_MODEL, D_EXPERT)),
            _const_spec((N_EXPERTS, D_MODEL, D_EXPERT)),
            _const_spec((N_EXPERTS * D_EXPERT, D_MODEL)),
            _const_spec((1, D_MODEL)),
        ],
        out_specs=pl.BlockSpec((tm, D_MODEL), row),
        out_shape=jax.ShapeDtypeStruct((n, D_MODEL), F32),
        scratch_shapes=[pltpu.VMEM((tm, N_EXPERTS * D_EXPERT), BF16)],
        compiler_params=_params(("parallel",)),
        name="moe_final",
    )(x1, h2, dw, wg, wu, wd, nf)


def _prepare_weights(rel_bias, norm_mix, w_in, lam_q1, lam_k1, lam_q2, lam_k2, subln_g,
                     w_branch_a, w_branch_b, w_out, norm_ffn, w_group, w_router,
                     w_exp_gate, w_exp_up, w_exp_down, norm_final):
    w = w_in[0]
    c = QK_SCALE * LOG2E
    o = 0
    qa, o = w[:, o:o + DIFF_QK] * c, o + DIFF_QK
    ka, o = w[:, o:o + DIFF_QK], o + DIFF_QK
    va, o = w[:, o:o + DIFF_V], o + DIFF_V
    qb, o = w[:, o:o + DIL_W] * c, o + DIL_W
    kb, o = w[:, o:o + DIL_W], o + DIL_W
    vb, o = w[:, o:o + DIL_W], o + DIL_W
    w_gate = w[:, o:].astype(BF16)
    grp = lambda g: [t[:, g * GRP_W:(g + 1) * GRP_W] for t in (qb, kb, vb)]
    w_nat = jnp.concatenate([qa, ka, va] + grp(0), axis=1).astype(BF16)
    w_d1 = jnp.concatenate(grp(1), axis=1).astype(BF16)
    w_d2 = jnp.concatenate(grp(2), axis=1).astype(BF16)
    wr = jnp.concatenate([w_group[0], w_router[0].transpose(1, 0, 2).reshape(D_MODEL, N_EXPERTS)], axis=1)
    wr = jnp.pad(wr, ((0, 0), (0, ROUTE_LANES - wr.shape[1]))).astype(BF16)
    return dict(
        rel_bias=rel_bias.astype(F32),
        norm_mix=norm_mix[0][None].astype(F32),
        w_nat=w_nat, w_gate=w_gate, w_d1=w_d1, w_d2=w_d2,
        lam=jnp.stack([lam_q1[0], lam_k1[0], lam_q2[0], lam_k2[0]]).astype(F32),
        subln_g=subln_g[0][None].astype(F32),
        wa=w_branch_a[0].astype(BF16), wb=w_branch_b[0].astype(BF16), wo=w_out[0].astype(BF16),
        norm_ffn=norm_ffn[0][None].astype(F32), wr=wr,
        wg=w_exp_gate[0].astype(BF16), wu=w_exp_up[0].astype(BF16),
        wd=w_exp_down[0].reshape(N_EXPERTS * D_EXPERT, D_MODEL).astype(BF16),
        norm_final=norm_final[None].astype(F32),
    )


def _encoder(x, p):
    b, s, d = x.shape
    n = b * s
    nat, gates, dil1, dil2 = _norm_proj(x, p["norm_mix"], p["w_nat"], p["w_gate"], p["w_d1"], p["w_d2"])
    tiles, far = _diff_bias_tiles(p["rel_bias"], min(TQ_DIFF, s), min(TK_DIFF, s))
    oa = _diff_attn(nat, tiles, far, p["lam"], p["subln_g"])
    srcs = ((nat.reshape(b, 1, s, NAT_COLS), (2 * DIFF_QK + DIFF_V) // GRP_W), (dil1, 0), (dil2, 0))
    dil = []
    for g, (win, dilation) in enumerate(DIL_CONFIGS):
        tiles = _dil_bias_tiles(p["rel_bias"], g, s, win, dilation)
        dil.append(_dil_attn(srcs[g][0], srcs[g][1], tiles, g, win, dilation))
    x1, h2, dw = _merge_route(x, oa, dil, gates, p["wa"], p["wb"], p["wo"], p["norm_ffn"], p["wr"])
    y = _moe_final(x1.reshape(n, d), h2.reshape(n, d), dw.reshape(n, ROUTE_LANES),
                   p["wg"], p["wu"], p["wd"], p["norm_final"])
    return y.reshape(b, s, d)


def kernel(x_prompt, x_sample, rel_bias, norm_mix, w_in, lam_q1, lam_k1, lam_q2, lam_k2, subln_g,
           w_branch_a, w_branch_b, w_out, norm_ffn, w_group, w_router, w_exp_gate, w_exp_up,
           w_exp_down, norm_final):
    p = _prepare_weights(rel_bias, norm_mix, w_in, lam_q1, lam_k1, lam_q2, lam_k2, subln_g,
                         w_branch_a, w_branch_b, w_out, norm_ffn, w_group, w_router,
                         w_exp_gate, w_exp_up, w_exp_down, norm_final)
    return (_encoder(x_prompt, p), _encoder(x_sample, p))
```

```python
import functools
import math

import jax
import jax.numpy as jnp
from jax import lax
from jax.experimental import pallas as pl
from jax.experimental.pallas import tpu as pltpu
from jax.experimental.pallas import tpu_sc as plsc

F32 = jnp.float32
BF16 = jnp.bfloat16

D_MODEL = 1024
HEAD_DIM = 64
H_DIFF = 4
H_DIL = 4
DIL_CONFIGS = ((128, 1), (512, 4), (2048, 16))
N_DIL = len(DIL_CONFIGS)
DIFF_QK = H_DIFF * 2 * HEAD_DIM
DIFF_V = H_DIFF * 2 * HEAD_DIM
DIL_W = N_DIL * H_DIL * HEAD_DIM
GRP_W = H_DIL * HEAD_DIM
GRP_COLS = 3 * GRP_W
NAT_COLS = 2 * DIFF_QK + DIFF_V + GRP_COLS
GATE_COLS = 2 * D_MODEL
Q_BLOCK = 128
N_BUCKETS = 32
MAX_DISTANCE = 128
N_GROUPS = 4
E_PER_GROUP = 4
N_EXPERTS = N_GROUPS * E_PER_GROUP
D_EXPERT = 256
EPS = 1e-6
NEG_INF = -1e30
LAM_INIT = 0.8 - 0.6 * math.exp(-0.3 * 0)
LOG2E = math.log2(math.e)
LN2 = math.log(2.0)
QK_SCALE = HEAD_DIM ** -0.5

LANES = 128
VMEM_LIMIT = 56 * 1024 * 1024

TM_PROJ = 512
TQ_DIFF = 512
TK_DIFF = 256
DIFF_GROUP = 4
DEN_ROWS = 16
DIL_QB = 256
SC_WINDOW = 128
SC_COLS = 256
TM_MERGE = 512
SUB_MERGE = 256
TM_MOE = 512
ROUTE_LANES = 128
EXP_LANE0 = N_GROUPS


def _const_spec(shape):
    nd = len(shape)
    return pl.BlockSpec(shape, lambda *_: (0,) * nd, pipeline_mode=pl.Buffered(1))


def _params(sem):
    return pltpu.CompilerParams(dimension_semantics=sem, vmem_limit_bytes=VMEM_LIMIT)


def _rel_bucket(rel):
    half = N_BUCKETS // 2
    max_exact = half // 2
    ret = (rel > 0).astype(jnp.int32) * half
    n = jnp.abs(rel)
    nf = jnp.maximum(n, 1).astype(F32)
    large = max_exact + (jnp.log(nf / max_exact) / math.log(MAX_DISTANCE / max_exact)
                         * (half - max_exact)).astype(jnp.int32)
    large = jnp.minimum(large, half - 1)
    return ret + jnp.where(n < max_exact, n, large)


def _bias_of_rel(table, rel):
    bucket = _rel_bucket(rel)[None]
    t = table.astype(F32)
    col = (t.shape[1],) + (1,) * rel.ndim
    out = jnp.zeros((t.shape[1],) + rel.shape, F32)
    for k in range(N_BUCKETS):
        out = jnp.where(bucket == k, t[k].reshape(col), out)
    return out


def _chunks(total, width):
    out, c = [], 0
    while c < total:
        w = min(width, total - c)
        out.append((c, w))
        c += w
    return out


def _norm_proj_kernel(x_ref, g_ref, wn_ref, wg_ref, w1_ref, w2_ref,
                      on_ref, og_ref, o1_ref, o2_ref, hs_ref, *, tm):
    x = x_ref[0]
    ms = jnp.mean(x * x, axis=-1, keepdims=True)
    hf = x * lax.rsqrt(ms + EPS) * g_ref[...]
    for c in range(D_MODEL // LANES):
        hs_ref[c] = hf[:, c * LANES:(c + 1) * LANES]
    h = hf.astype(BF16)
    for c0, cw in _chunks(NAT_COLS, 512):
        on_ref[0, :, c0:c0 + cw] = jnp.dot(h, wn_ref[:, c0:c0 + cw],
                                           preferred_element_type=F32).astype(BF16)
    for c0, cw in _chunks(GATE_COLS, 512):
        og_ref[0, :, c0:c0 + cw] = jnp.dot(h, wg_ref[:, c0:c0 + cw],
                                           preferred_element_type=F32).astype(BF16)
    for w_ref, o_ref, d in ((w1_ref, o1_ref, DIL_CONFIGS[1][1]), (w2_ref, o2_ref, DIL_CONFIGS[2][1])):
        rows = tm // d
        hp = jnp.concatenate(
            [jnp.concatenate([hs_ref[c, pl.ds(r, rows, stride=d), :] for r in range(d)], axis=0)
             for c in range(D_MODEL // LANES)], axis=1).astype(BF16)
        for c0, cw in _chunks(GRP_COLS, 256):
            res = jnp.dot(hp, w_ref[:, c0:c0 + cw], preferred_element_type=F32).astype(BF16)
            o_ref[0, :, :, c0:c0 + cw] = res.reshape(d, rows, cw)


def _norm_proj(x, g, w_nat, w_gate, w_d1, w_d2):
    b, s, _ = x.shape
    tm = TM_PROJ
    d1, d2 = DIL_CONFIGS[1][1], DIL_CONFIGS[2][1]
    return pl.pallas_call(
        functools.partial(_norm_proj_kernel, tm=tm),
        grid=(b, s // tm),
        in_specs=[
            pl.BlockSpec((1, tm, D_MODEL), lambda bi, i: (bi, i, 0)),
            _const_spec((1, D_MODEL)),
            _const_spec((D_MODEL, NAT_COLS)),
            _const_spec((D_MODEL, GATE_COLS)),
            _const_spec((D_MODEL, GRP_COLS)),
            _const_spec((D_MODEL, GRP_COLS)),
        ],
        out_specs=[
            pl.BlockSpec((1, tm, NAT_COLS), lambda bi, i: (bi, i, 0)),
            pl.BlockSpec((1, tm, GATE_COLS), lambda bi, i: (bi, i, 0)),
            pl.BlockSpec((1, d1, tm // d1, GRP_COLS), lambda bi, i: (bi, 0, i, 0)),
            pl.BlockSpec((1, d2, tm // d2, GRP_COLS), lambda bi, i: (bi, 0, i, 0)),
        ],
        out_shape=[
            jax.ShapeDtypeStruct((b, s, NAT_COLS), BF16),
            jax.ShapeDtypeStruct((b, s, GATE_COLS), BF16),
            jax.ShapeDtypeStruct((b, d1, s // d1, GRP_COLS), BF16),
            jax.ShapeDtypeStruct((b, d2, s // d2, GRP_COLS), BF16),
        ],
        scratch_shapes=[pltpu.VMEM((D_MODEL // LANES, tm, LANES), F32)],
        compiler_params=_params(("parallel", "parallel")),
        name="norm_proj",
    )(x, g, w_nat, w_gate, w_d1, w_d2)


def _diff_bias_tiles(rel_bias, tq, tk):
    ratio = tq // tk
    t = jnp.arange(-2, ratio + 2)[:, None, None]
    j = jnp.arange(tk)[None, :, None]
    i = jnp.arange(tq)[None, None, :]
    tiles = _bias_of_rel(rel_bias[:, :H_DIFF], t * tk + j - i) * LOG2E
    far = _bias_of_rel(rel_bias[:, :H_DIFF], jnp.asarray([-MAX_DISTANCE, MAX_DISTANCE])) * LOG2E
    return tiles, far


def _diff_attn_kernel(lam_ref, far_ref, q_ref, k_ref, v_ref, bias_ref, g_ref, o_ref,
                      vt_ref, st_ref, p_ref, acc_ref, *, tq, tk, nkb):
    h = pl.program_id(1)
    qi = pl.program_id(2)
    ratio = tq // tk
    n_band = ratio + 2
    vchunk = min(512, nkb * tk)

    @pl.when(qi == 0)
    def _():
        for c in range(nkb * tk // vchunk):
            cs = slice(c * vchunk, (c + 1) * vchunk)
            vt_ref[:LANES, cs] = v_ref[0, cs, :].astype(F32).T.astype(BF16)
        vt_ref[LANES:, :] = jnp.ones((DEN_ROWS, nkb * tk), BF16)

    qt = q_ref[0].astype(F32).T
    first = lax.broadcasted_iota(jnp.int32, (LANES, tq), 0) < HEAD_DIM
    qts = (jnp.where(first, qt, 0.0).astype(BF16), jnp.where(first, 0.0, qt).astype(BF16))
    kb0 = qi * ratio

    def key_block(j):
        kb = kb0 - 1 + j
        return jnp.where(kb < 0, kb + nkb, jnp.where(kb >= nkb, kb - nkb, kb))

    def scores(j, slot, band):
        kb = key_block(j)
        ks = pl.multiple_of(kb * tk, tk)
        k = k_ref[0, pl.ds(ks, tk), :]
        if band:
            bt = bias_ref[0, jnp.clip(kb - kb0, -2, ratio + 1) + 2]
        mx = []
        for mp in range(2):
            st = jnp.dot(k, qts[mp], preferred_element_type=F32)
            if band:
                st = st + bt
            st_ref[slot, mp] = st
            mx.append(jnp.max(st, axis=0, keepdims=True))
        return tuple(mx)

    def softmax(j, slot, stats, mx, band):
        if not band:
            kb = key_block(j)
            far_c = jnp.where(kb < kb0, far_ref[h, 0], far_ref[h, 1])
        out, alphas = [], []
        for mp in range(2):
            m = stats[mp]
            if band:
                m_new = jnp.maximum(m, mx[mp])
                shift = m_new
            else:
                m_new = jnp.maximum(m, mx[mp] + far_c)
                shift = m_new - far_c
            p_ref[slot, mp] = jnp.exp2((st_ref[slot, mp] - shift).astype(BF16))
            out.append(m_new)
            alphas.append(jnp.exp2(m - m_new))
        return tuple(out), tuple(alphas)

    def values(j, slot, alphas):
        ks = pl.multiple_of(key_block(j) * tk, tk)
        vt = vt_ref[:, pl.ds(ks, tk)]
        for mp in range(2):
            acc_ref[mp] = alphas[mp] * acc_ref[mp] + jnp.dot(vt, p_ref[slot, mp],
                                                             preferred_element_type=F32)

    def group(jj, carry, base, band):
        stats, alphas, mx = carry[:2], carry[2:4], carry[4:]
        j = base + DIFF_GROUP * jj
        for u in range(DIFF_GROUP):
            mx_next = scores(j + u + 1, (u + 1) % DIFF_GROUP, band and u + 1 < DIFF_GROUP)
            values(j + u - 1, (u - 1) % DIFF_GROUP, alphas)
            stats, alphas = softmax(j + u, u, stats, mx, band)
            mx = mx_next
        return stats + alphas + mx

    last = DIFF_GROUP - 1
    acc_ref[...] = jnp.zeros_like(acc_ref)
    p_ref[last] = jnp.zeros_like(p_ref[last])
    mx0 = scores(0, 0, True)
    ninf, one = (jnp.full((1, tq), v, F32) for v in (-jnp.inf, 1.0))
    carry = group(0, (ninf, ninf, one, one) + mx0, base=0, band=True)
    carry = lax.fori_loop(0, (nkb - n_band) // DIFF_GROUP,
                          functools.partial(group, base=n_band, band=False), carry)
    values(nkb - 1, last, carry[2:4])
    l0, l1 = acc_ref[0, LANES:LANES + 1, :], acc_ref[1, LANES:LANES + 1, :]

    lp = lam_ref[...]
    lam = (jnp.exp(jnp.sum(lp[0:1] * lp[1:2], axis=-1, keepdims=True))
           - jnp.exp(jnp.sum(lp[2:3] * lp[3:4], axis=-1, keepdims=True)) + LAM_INIT)
    ot = acc_ref[0, :LANES, :] / l0 - lam * (acc_ref[1, :LANES, :] / l1)
    ms = jnp.mean(ot * ot, axis=0, keepdims=True)
    y = (ot * lax.rsqrt(ms + EPS)).T * g_ref[...] * (1.0 - LAM_INIT)
    o_ref[0] = y.astype(BF16)


def _diff_attn(nat, bias_tiles, far, lam_params, subln_g):
    b, s, _ = nat.shape
    tq, tk = min(TQ_DIFF, s), min(TK_DIFF, s)
    nt = bias_tiles.shape[1]
    ratio, nkb = tq // tk, s // tk
    assert tq % tk == 0 and tk >= MAX_DISTANCE and nkb >= ratio + 2
    assert ratio + 2 == DIFF_GROUP and nkb % DIFF_GROUP == 0
    kern = functools.partial(_diff_attn_kernel, tq=tq, tk=tk, nkb=s // tk)
    return pl.pallas_call(
        kern,
        grid=(b, H_DIFF, s // tq),
        in_specs=[
            _const_spec((4, HEAD_DIM)),
            pl.BlockSpec(memory_space=pltpu.SMEM),
            pl.BlockSpec((1, tq, LANES), lambda bi, h, qi: (bi, qi, h)),
            pl.BlockSpec((1, s, LANES), lambda bi, h, qi: (bi, 0, H_DIFF + h)),
            pl.BlockSpec((1, s, LANES), lambda bi, h, qi: (bi, 0, 2 * H_DIFF + h)),
            pl.BlockSpec((1, nt, tk, tq), lambda bi, h, qi: (h, 0, 0, 0)),
            _const_spec((1, 2 * HEAD_DIM)),
        ],
        out_specs=pl.BlockSpec((1, tq, LANES), lambda bi, h, qi: (bi, qi, h)),
        out_shape=jax.ShapeDtypeStruct((b, s, DIFF_V), BF16),
        scratch_shapes=[
            pltpu.VMEM((LANES + DEN_ROWS, s), BF16),
            pltpu.VMEM((DIFF_GROUP, 2, tk, tq), F32),
            pltpu.VMEM((DIFF_GROUP, 2, tk, tq), BF16),
            pltpu.VMEM((2, LANES + DEN_ROWS, tq), F32),
        ],
        compiler_params=_params(("parallel", "parallel", "arbitrary")),
        name="diff_attn",
    )(lam_params, far, nat, nat, nat, bias_tiles, subln_g)


def _dil_geometry(s, window, dilation):
    l = s // dilation
    w = window // (2 * dilation)
    qb = min(DIL_QB, l)
    kw = min(qb + 2 * LANES, l)
    assert w <= LANES and l % qb == 0 and l % LANES == 0
    return l, w, qb, kw, l // qb


def _dil_bias_tiles(rel_bias, g, s, window, dilation):
    l, w, qb, kw, nb = _dil_geometry(s, window, dilation)
    c0 = H_DIFF + g * H_DIL
    shifts = []
    for n in (0, min(1, nb - 1), nb - 1):
        l0 = n * qb
        shifts.append(min(max(l0 - LANES, 0), l - kw) - l0)
    j = jnp.arange(kw)[None, :, None]
    i = jnp.arange(qb)[None, None, :]
    off = jnp.asarray(shifts)[:, None, None] + j - i
    b = _bias_of_rel(rel_bias[:, c0:c0 + H_DIL], off * dilation) * LOG2E
    b = jnp.where((jnp.abs(off) <= w)[None], b, NEG_INF)
    return jnp.stack([jnp.concatenate([b[2 * p], b[2 * p + 1]], axis=-1) for p in range(H_DIL // 2)], axis=1)


def _dil_attn_kernel(q_ref, k_ref, v_ref, bias_ref, o_ref, lse_ref, qt_ref, vt_ref, st_ref,
                     *, l, qb, kw, nb):
    chunk = min(512, l)
    for pair in range(H_DIL // 2):
        cs = slice(pair * LANES, (pair + 1) * LANES)
        for c in range(l // chunk):
            rs = slice(c * chunk, (c + 1) * chunk)
            qt_ref[pair, :, rs] = q_ref[0, 0, rs, cs].astype(F32).T.astype(BF16)
            vt_ref[pair, :LANES, rs] = v_ref[0, 0, rs, cs].astype(F32).T.astype(BF16)
        vt_ref[pair, LANES:, :] = jnp.ones((DEN_ROWS, l), BF16)
    first = lax.broadcasted_iota(jnp.int32, (LANES, qb), 0) < HEAD_DIM

    def window(n):
        l0 = pl.multiple_of(n * qb, qb)
        return l0, pl.multiple_of(jnp.clip(l0 - LANES, 0, l - kw), LANES)

    def scores(n, slot):
        l0, ws = window(n)
        for pair in range(H_DIL // 2):
            qt = qt_ref[pair, :, pl.ds(l0, qb)]
            q2 = jnp.concatenate([jnp.where(first, qt, jnp.zeros_like(qt)),
                                  jnp.where(first, jnp.zeros_like(qt), qt)], axis=1)
            kp = k_ref[0, 0, pl.ds(ws, kw), pair * LANES:(pair + 1) * LANES]
            st_ref[slot, pair] = jnp.dot(kp, q2, preferred_element_type=F32)

    def finish(n, slot):
        l0, ws = window(n)
        t = jnp.where(n == 0, 0, jnp.where(n == nb - 1, 2, 1))
        for pair in range(H_DIL // 2):
            cs = slice(pair * LANES, (pair + 1) * LANES)
            st = st_ref[slot, pair] + bias_ref[t, pair]
            mx = jnp.max(st, axis=0, keepdims=True)
            p = jnp.exp2(st - mx).astype(BF16)
            acc = jnp.dot(vt_ref[pair, :, pl.ds(ws, kw)], p, preferred_element_type=F32)
            den = acc[LANES:LANES + 1, :]
            ot = acc[:LANES, :] / den
            lse = mx * LN2 + jnp.log(den)
            o_t = jnp.where(first, ot[:, :qb], ot[:, qb:])
            lse_t = jnp.where(first, lse[:, :qb], lse[:, qb:])
            o_ref[0, 0, pl.ds(l0, qb), cs] = o_t.T.astype(BF16)
            lse_ref[0, 0, pl.ds(l0, qb), cs] = lse_t.T

    def two_blocks(i, carry):
        n = 2 * i
        scores(n + 1, 1)
        finish(n, 0)
        scores(n + 2, 0)
        finish(n + 1, 1)
        return carry

    scores(0, 0)
    lax.fori_loop(0, (nb - 1) // 2, two_blocks, 0)
    if nb % 2 == 0:
        scores(nb - 1, 1)
        finish(nb - 2, 0)
        finish(nb - 1, 1)
    else:
        finish(nb - 1, 0)


def _dil_attn(arr, col0, bias_tiles, g, window, dilation):
    b, d, l, _ = arr.shape
    _, w, qb, kw, nb = _dil_geometry(l * d, window, dilation)
    kern = functools.partial(_dil_attn_kernel, l=l, qb=qb, kw=kw, nb=nb)
    return pl.pallas_call(
        kern,
        grid=(b, d),
        in_specs=[
            pl.BlockSpec((1, 1, l, GRP_W), lambda bi, r: (bi, r, 0, col0)),
            pl.BlockSpec((1, 1, l, GRP_W), lambda bi, r: (bi, r, 0, col0 + 1)),
            pl.BlockSpec((1, 1, l, GRP_W), lambda bi, r: (bi, r, 0, col0 + 2)),
            _const_spec((3, H_DIL // 2, kw, 2 * qb)),
        ],
        out_specs=[
            pl.BlockSpec((1, 1, l, GRP_W), lambda bi, r: (bi, r, 0, 0)),
            pl.BlockSpec((1, 1, l, GRP_W), lambda bi, r: (bi, r, 0, 0)),
        ],
        out_shape=[
            jax.ShapeDtypeStruct((b, d, l, GRP_W), BF16),
            jax.ShapeDtypeStruct((b, d, l, GRP_W), F32),
        ],
        scratch_shapes=[
            pltpu.VMEM((H_DIL // 2, LANES, l), BF16),
            pltpu.VMEM((H_DIL // 2, LANES + DEN_ROWS, l), BF16),
            pltpu.VMEM((2, H_DIL // 2, kw, 2 * qb), F32),
        ],
        compiler_params=_params(("parallel", "parallel")),
        name=f"dil_attn_g{g}",
    )(arr, arr, arr, bias_tiles)


def _merge_route_kernel(x_ref, oa_ref, o0_ref, l0_ref, o1_ref, l1_ref, o2_ref, l2_ref, gt_ref,
                        wa_ref, wb_ref, wo_ref, nf_ref, wr_ref, x1_ref, h2_ref, dw_ref,
                        so1_ref, sl1_ref, so2_ref, sl2_ref, *, tm):
    for src, dst, d in ((o1_ref, so1_ref, DIL_CONFIGS[1][1]), (l1_ref, sl1_ref, DIL_CONFIGS[1][1]),
                        (o2_ref, so2_ref, DIL_CONFIGS[2][1]), (l2_ref, sl2_ref, DIL_CONFIGS[2][1])):
        for r in range(d):
            for c in range(GRP_W // LANES):
                dst[c, pl.ds(r, tm // d, stride=d), :] = src[0, r, :, c * LANES:(c + 1) * LANES].astype(F32)
    subs = [slice(i * SUB_MERGE, (i + 1) * SUB_MERGE) for i in range(tm // SUB_MERGE)]
    unsplit = lambda ref, rs: jnp.concatenate([ref[c, rs, :] for c in range(GRP_W // LANES)], axis=1)

    def dil_mix(rs):
        l0, l1, l2 = l0_ref[0, 0, rs, :], unsplit(sl1_ref, rs), unsplit(sl2_ref, rs)
        mx = jnp.maximum(jnp.maximum(l0, l1), l2)
        e0, e1, e2 = jnp.exp(l0 - mx), jnp.exp(l1 - mx), jnp.exp(l2 - mx)
        ob = (e0 * o0_ref[0, 0, rs, :].astype(F32) + e1 * unsplit(so1_ref, rs)
              + e2 * unsplit(so2_ref, rs)) / (e0 + e1 + e2)
        return ob.astype(BF16)

    obs = [dil_mix(rs) for rs in subs]
    pas = [jnp.dot(oa_ref[0, rs, :], wa_ref[...], preferred_element_type=F32) for rs in subs]
    pbs = [jnp.dot(ob, wb_ref[...], preferred_element_type=F32) for ob in obs]

    def gate(rs, pa, pb):
        ga = jax.nn.sigmoid(gt_ref[0, rs, :D_MODEL].astype(F32))
        gb = jax.nn.sigmoid(gt_ref[0, rs, D_MODEL:].astype(F32))
        return (ga * pa + gb * pb).astype(BF16)

    mgs = [gate(rs, pa, pb) for rs, pa, pb in zip(subs, pas, pbs)]
    x1s = [x_ref[0, rs, :] + jnp.dot(mg, wo_ref[...], preferred_element_type=F32)
           for rs, mg in zip(subs, mgs)]

    def ffn_norm(rs, x1):
        x1_ref[0, rs, :] = x1
        ms = jnp.mean(x1 * x1, axis=-1, keepdims=True)
        h2 = (x1 * lax.rsqrt(ms + EPS) * nf_ref[...]).astype(BF16)
        h2_ref[0, rs, :] = h2
        return h2

    h2s = [ffn_norm(rs, x1) for rs, x1 in zip(subs, x1s)]
    lgs = [jnp.dot(h2, wr_ref[...], preferred_element_type=F32) for h2 in h2s]

    def route(rs, lg):
        lane = lax.broadcasted_iota(jnp.int32, lg.shape, 1)
        big = jnp.int32(ROUTE_LANES)
        gl = jnp.where(lane < N_GROUPS, lg, -jnp.inf)
        gmax = jnp.max(gl, axis=-1, keepdims=True)
        gi = jnp.min(jnp.where(gl == gmax, lane, big), axis=-1, keepdims=True)
        gw = 1.0 / jnp.sum(jnp.exp(gl - gmax), axis=-1, keepdims=True)
        e_lo = EXP_LANE0 + E_PER_GROUP * gi
        es = jnp.where((lane >= e_lo) & (lane < e_lo + E_PER_GROUP), lg, -jnp.inf)
        v1 = jnp.max(es, axis=-1, keepdims=True)
        i1 = jnp.min(jnp.where(es == v1, lane, big), axis=-1, keepdims=True)
        es2 = jnp.where(lane == i1, -jnp.inf, es)
        v2 = jnp.max(es2, axis=-1, keepdims=True)
        i2 = jnp.min(jnp.where(es2 == v2, lane, big), axis=-1, keepdims=True)
        t = jnp.exp(v2 - v1)
        w1 = gw / (1.0 + t)
        w2 = w1 * t
        dw_ref[0, rs, :] = jnp.where(lane == i1, w1, 0.0) + jnp.where(lane == i2, w2, 0.0)

    for rs, lg in zip(subs, lgs):
        route(rs, lg)


def _merge_route(x, oa, dil, gates, wa, wb, wo, nf, wr):
    b, s, _ = x.shape
    tm = TM_MERGE
    d1, d2 = DIL_CONFIGS[1][1], DIL_CONFIGS[2][1]
    tok = lambda bi, i: (bi, i, 0)
    cls = lambda bi, i: (bi, 0, i, 0)
    (o0, s0), (o1, s1), (o2, s2) = dil
    return pl.pallas_call(
        functools.partial(_merge_route_kernel, tm=tm),
        grid=(b, s // tm),
        in_specs=[
            pl.BlockSpec((1, tm, D_MODEL), tok),
            pl.BlockSpec((1, tm, DIFF_V), tok),
            pl.BlockSpec((1, 1, tm, GRP_W), cls), pl.BlockSpec((1, 1, tm, GRP_W), cls),
            pl.BlockSpec((1, d1, tm // d1, GRP_W), cls), pl.BlockSpec((1, d1, tm // d1, GRP_W), cls),
            pl.BlockSpec((1, d2, tm // d2, GRP_W), cls), pl.BlockSpec((1, d2, tm // d2, GRP_W), cls),
            pl.BlockSpec((1, tm, GATE_COLS), tok),
            _const_spec((DIFF_V, D_MODEL)),
            _const_spec((GRP_W, D_MODEL)),
            _const_spec((D_MODEL, D_MODEL)),
            _const_spec((1, D_MODEL)),
            _const_spec((D_MODEL, ROUTE_LANES)),
        ],
        out_specs=[
            pl.BlockSpec((1, tm, D_MODEL), tok),
            pl.BlockSpec((1, tm, D_MODEL), tok),
            pl.BlockSpec((1, tm, ROUTE_LANES), tok),
        ],
        out_shape=[
            jax.ShapeDtypeStruct((b, s, D_MODEL), F32),
            jax.ShapeDtypeStruct((b, s, D_MODEL), BF16),
            jax.ShapeDtypeStruct((b, s, ROUTE_LANES), F32),
        ],
        scratch_shapes=[pltpu.VMEM((GRP_W // LANES, tm, LANES), F32) for _ in range(4)],
        compiler_params=_params(("parallel", "parallel")),
        name="merge_route",
    )(x, oa, o0, s0, o1, s1, o2, s2, gates, wa, wb, wo, nf, wr)


def _moe_final_kernel(x1_ref, h2_ref, dw_ref, wg_ref, wu_ref, wd_ref, nf_ref, y_ref, he_ref):
    h2 = h2_ref[...]
    dw = dw_ref[...]
    for e in range(N_EXPERTS):
        gp = jnp.dot(h2, wg_ref[e], preferred_element_type=F32)
        up = jnp.dot(h2, wu_ref[e], preferred_element_type=F32)
        we = dw[:, EXP_LANE0 + e:EXP_LANE0 + e + 1]
        he_ref[:, e * D_EXPERT:(e + 1) * D_EXPERT] = (jax.nn.silu(gp) * up * we).astype(BF16)
    moe = jnp.dot(he_ref[...], wd_ref[...], preferred_element_type=F32)
    x2 = x1_ref[...] + moe
    ms = jnp.mean(x2 * x2, axis=-1, keepdims=True)
    y_ref[...] = x2 * lax.rsqrt(ms + EPS) * nf_ref[...]


def _moe_final(x1, h2, dw, wg, wu, wd, nf):
    n = x1.shape[0]
    tm = TM_MOE
    row = lambda i: (i, 0)
    return pl.pallas_call(
        _moe_final_kernel,
        grid=(n // tm,),
        in_specs=[
            pl.BlockSpec((tm, D_MODEL), row),
            pl.BlockSpec((tm, D_MODEL), row),
            pl.BlockSpec((tm, ROUTE_LANES), row),
            _const_spec((N_EXPERTS, D_MODEL, D_EXPERT)),
            _const_spec((N_EXPERTS, D_MODEL, D_EXPERT)),
            _const_spec((N_EXPERTS * D_EXPERT, D_MODEL)),
            _const_spec((1, D_MODEL)),
        ],
        out_specs=pl.BlockSpec((tm, D_MODEL), row),
        out_shape=jax.ShapeDtypeStruct((n, D_MODEL), F32),
        scratch_shapes=[pltpu.VMEM((tm, N_EXPERTS * D_EXPERT), BF16)],
        compiler_params=_params(("parallel",)),
        name="moe_final",
    )(x1, h2, dw, wg, wu, wd, nf)


def _sc_mesh():
    return plsc.VectorSubcoreMesh(core_axis_name="core", subcore_axis_name="subcore")


def _sc_scatter_rows(x, idx, m):
    n, d = x.shape

    @pl.kernel(out_type=jax.ShapeDtypeStruct((m, d), x.dtype), mesh=_sc_mesh(), scratch_types=[])
    def scatter(x_hbm, i_hbm, o_hbm):
        for c in range(d // SC_COLS):
            cols = pl.ds(c * SC_COLS, SC_COLS)
            o_cols = o_hbm.at[:, cols]

            def body(x_vmem, i_vmem, o_cols=o_cols):
                pltpu.sync_copy(x_vmem, o_cols.at[i_vmem.at[0]])

            pltpu.emit_pipeline(
                body,
                grid=(n // SC_WINDOW,),
                in_specs=[pl.BlockSpec((SC_WINDOW, SC_COLS), lambda i, c=c: (i, c)),
                          pl.BlockSpec((1, SC_WINDOW), lambda i: (0, i))],
                out_specs=[],
                core_axis_name=("core", "subcore"),
                dimension_semantics=(pltpu.PARALLEL,),
            )(x_hbm, i_hbm)

    return scatter(x, idx.reshape(1, n))


def _sc_gather_rows(x, idx):
    n = idx.shape[0]
    d = x.shape[1]

    @pl.kernel(out_type=jax.ShapeDtypeStruct((n, d), x.dtype), mesh=_sc_mesh(), scratch_types=[])
    def gather(x_hbm, i_hbm, o_hbm):
        for c in range(d // SC_COLS):
            x_cols = x_hbm.at[:, pl.ds(c * SC_COLS, SC_COLS)]

            def body(i_vmem, o_vmem, x_cols=x_cols):
                pltpu.sync_copy(x_cols.at[i_vmem.at[0]], o_vmem)

            pltpu.emit_pipeline(
                body,
                grid=(n // SC_WINDOW,),
                in_specs=[pl.BlockSpec((1, SC_WINDOW), lambda i: (0, i))],
                out_specs=[pl.BlockSpec((SC_WINDOW, SC_COLS), lambda i, c=c: (i, c))],
                core_axis_name=("core", "subcore"),
                dimension_semantics=(pltpu.PARALLEL,),
            )(i_hbm, o_hbm)

    return gather(x, idx.reshape(1, n))


def _prepare_weights(rel_bias, norm_mix, w_in, lam_q1, lam_k1, lam_q2, lam_k2, subln_g,
                     w_branch_a, w_branch_b, w_out, norm_ffn, w_group, w_router,
                     w_exp_gate, w_exp_up, w_exp_down, norm_final):
    w = w_in[0]
    c = QK_SCALE * LOG2E
    o = 0
    qa, o = w[:, o:o + DIFF_QK] * c, o + DIFF_QK
    ka, o = w[:, o:o + DIFF_QK], o + DIFF_QK
    va, o = w[:, o:o + DIFF_V], o + DIFF_V
    qb, o = w[:, o:o + DIL_W] * c, o + DIL_W
    kb, o = w[:, o:o + DIL_W], o + DIL_W
    vb, o = w[:, o:o + DIL_W], o + DIL_W
    w_gate = w[:, o:].astype(BF16)
    grp = lambda g: [t[:, g * GRP_W:(g + 1) * GRP_W] for t in (qb, kb, vb)]
    w_nat = jnp.concatenate([qa, ka, va] + grp(0), axis=1).astype(BF16)
    w_d1 = jnp.concatenate(grp(1), axis=1).astype(BF16)
    w_d2 = jnp.concatenate(grp(2), axis=1).astype(BF16)
    wr = jnp.concatenate([w_group[0], w_router[0].transpose(1, 0, 2).reshape(D_MODEL, N_EXPERTS)], axis=1)
    wr = jnp.pad(wr, ((0, 0), (0, ROUTE_LANES - wr.shape[1]))).astype(BF16)
    return dict(
        rel_bias=rel_bias.astype(F32),
        norm_mix=norm_mix[0][None].astype(F32),
        w_nat=w_nat, w_gate=w_gate, w_d1=w_d1, w_d2=w_d2,
        lam=jnp.stack([lam_q1[0], lam_k1[0], lam_q2[0], lam_k2[0]]).astype(F32),
        subln_g=subln_g[0][None].astype(F32),
        wa=w_branch_a[0].astype(BF16), wb=w_branch_b[0].astype(BF16), wo=w_out[0].astype(BF16),
        norm_ffn=norm_ffn[0][None].astype(F32), wr=wr,
        wg=w_exp_gate[0].astype(BF16), wu=w_exp_up[0].astype(BF16),
        wd=w_exp_down[0].reshape(N_EXPERTS * D_EXPERT, D_MODEL).astype(BF16),
        norm_final=norm_final[None].astype(F32),
    )


def _encoder(x, p):
    b, s, d = x.shape
    n = b * s
    nat, gates, dil1, dil2 = _norm_proj(x, p["norm_mix"], p["w_nat"], p["w_gate"], p["w_d1"], p["w_d2"])
    tiles, far = _diff_bias_tiles(p["rel_bias"], min(TQ_DIFF, s), min(TK_DIFF, s))
    oa = _diff_attn(nat, tiles, far, p["lam"], p["subln_g"])
    srcs = ((nat.reshape(b, 1, s, NAT_COLS), (2 * DIFF_QK + DIFF_V) // GRP_W), (dil1, 0), (dil2, 0))
    dil = []
    for g, (win, dilation) in enumerate(DIL_CONFIGS):
        tiles = _dil_bias_tiles(p["rel_bias"], g, s, win, dilation)
        dil.append(_dil_attn(srcs[g][0], srcs[g][1], tiles, g, win, dilation))
    x1, h2, dw = _merge_route(x, oa, dil, gates, p["wa"], p["wb"], p["wo"], p["norm_ffn"], p["wr"])
    perm = (jnp.arange(n, dtype=jnp.int32) * 4099) % n
    x1 = _sc_gather_rows(_sc_scatter_rows(x1.reshape(n, d), perm, n), perm)
    y = _moe_final(x1.reshape(n, d), h2.reshape(n, d), dw.reshape(n, ROUTE_LANES),
                   p["wg"], p["wu"], p["wd"], p["norm_final"])
    return y.reshape(b, s, d)


def kernel(x_prompt, x_sample, rel_bias, norm_mix, w_in, lam_q1, lam_k1, lam_q2, lam_k2, subln_g,
           w_branch_a, w_branch_b, w_out, norm_ffn, w_group, w_router, w_exp_gate, w_exp_up,
           w_exp_down, norm_final):
    p = _prepare_weights(rel_bias, norm_mix, w_in, lam_q1, lam_k1, lam_q2, lam_k2, subln_g,
                         w_branch_a, w_branch_b, w_out, norm_ffn, w_group, w_router,
                         w_exp_gate, w_exp_up, w_exp_down, norm_final)
    return (_encoder(x_prompt, p), _encoder(x_sample, p))
```

```python
import functools
import math

import jax
import jax.numpy as jnp
from jax import lax
from jax.experimental import pallas as pl
from jax.experimental.pallas import tpu as pltpu
from jax.experimental.pallas import tpu_sc as plsc

F32 = jnp.float32
BF16 = jnp.bfloat16

D_MODEL = 1024
HEAD_DIM = 64
H_DIFF = 4
H_DIL = 4
DIL_CONFIGS = ((128, 1), (512, 4), (2048, 16))
N_DIL = len(DIL_CONFIGS)
DIFF_QK = H_DIFF * 2 * HEAD_DIM
DIFF_V = H_DIFF * 2 * HEAD_DIM
DIL_W = N_DIL * H_DIL * HEAD_DIM
GRP_W = H_DIL * HEAD_DIM
GRP_COLS = 3 * GRP_W
NAT_COLS = 2 * DIFF_QK + DIFF_V + GRP_COLS
GATE_COLS = 2 * D_MODEL
Q_BLOCK = 128
N_BUCKETS = 32
MAX_DISTANCE = 128
N_GROUPS = 4
E_PER_GROUP = 4
N_EXPERTS = N_GROUPS * E_PER_GROUP
D_EXPERT = 256
EPS = 1e-6
NEG_INF = -1e30
LAM_INIT = 0.8 - 0.6 * math.exp(-0.3 * 0)
LOG2E = math.log2(math.e)
LN2 = math.log(2.0)
QK_SCALE = HEAD_DIM ** -0.5

LANES = 128
SUBLANES = 8
VMEM_LIMIT = 56 * 1024 * 1024

TM_PROJ = 512
TQ_DIFF = 512
TK_DIFF = 256
DIFF_GROUP = 4
DEN_ROWS = 16
DIL_QB = 256
SC_WINDOW = 128
SC_COLS = 256
TM_MERGE = 512
SUB_MERGE = 256
TM_MOE = 512
ROUTE_LANES = 128
EXP_LANE0 = N_GROUPS


def _const_spec(shape):
    nd = len(shape)
    return pl.BlockSpec(shape, lambda *_: (0,) * nd, pipeline_mode=pl.Buffered(1))


def _params(sem):
    return pltpu.CompilerParams(dimension_semantics=sem, vmem_limit_bytes=VMEM_LIMIT)


def _rel_bucket(rel):
    half = N_BUCKETS // 2
    max_exact = half // 2
    ret = (rel > 0).astype(jnp.int32) * half
    n = jnp.abs(rel)
    nf = jnp.maximum(n, 1).astype(F32)
    large = max_exact + (jnp.log(nf / max_exact) / math.log(MAX_DISTANCE / max_exact)
                         * (half - max_exact)).astype(jnp.int32)
    large = jnp.minimum(large, half - 1)
    return ret + jnp.where(n < max_exact, n, large)


def _bias_of_rel(table, rel):
    bucket = _rel_bucket(rel)[None]
    t = table.astype(F32)
    col = (t.shape[1],) + (1,) * rel.ndim
    out = jnp.zeros((t.shape[1],) + rel.shape, F32)
    for k in range(N_BUCKETS):
        out = jnp.where(bucket == k, t[k].reshape(col), out)
    return out


def _chunks(total, width):
    out, c = [], 0
    while c < total:
        w = min(width, total - c)
        out.append((c, w))
        c += w
    return out


def _norm_proj_kernel(x_ref, g_ref, wn_ref, wg_ref, w1_ref, w2_ref,
                      on_ref, og_ref, o1_ref, o2_ref, hs_ref, *, tm):
    x = x_ref[0]
    ms = jnp.mean(x * x, axis=-1, keepdims=True)
    hf = x * lax.rsqrt(ms + EPS) * g_ref[...]
    for c in range(D_MODEL // LANES):
        hs_ref[c] = hf[:, c * LANES:(c + 1) * LANES]
    h = hf.astype(BF16)
    for c0, cw in _chunks(NAT_COLS, 512):
        on_ref[0, :, c0:c0 + cw] = jnp.dot(h, wn_ref[:, c0:c0 + cw],
                                           preferred_element_type=F32).astype(BF16)
    for c0, cw in _chunks(GATE_COLS, 512):
        og_ref[0, :, c0:c0 + cw] = jnp.dot(h, wg_ref[:, c0:c0 + cw],
                                           preferred_element_type=F32).astype(BF16)
    for w_ref, o_ref, d in ((w1_ref, o1_ref, DIL_CONFIGS[1][1]), (w2_ref, o2_ref, DIL_CONFIGS[2][1])):
        rows = tm // d
        hp = jnp.concatenate(
            [jnp.concatenate([hs_ref[c, pl.ds(r, rows, stride=d), :] for r in range(d)], axis=0)
             for c in range(D_MODEL // LANES)], axis=1).astype(BF16)
        for c0, cw in _chunks(GRP_COLS, 256):
            res = jnp.dot(hp, w_ref[:, c0:c0 + cw], preferred_element_type=F32).astype(BF16)
            o_ref[0, :, :, c0:c0 + cw] = res.reshape(d, rows, cw)


def _norm_proj(x, g, w_nat, w_gate, w_d1, w_d2):
    b, s, _ = x.shape
    tm = TM_PROJ
    d1, d2 = DIL_CONFIGS[1][1], DIL_CONFIGS[2][1]
    return pl.pallas_call(
        functools.partial(_norm_proj_kernel, tm=tm),
        grid=(b, s // tm),
        in_specs=[
            pl.BlockSpec((1, tm, D_MODEL), lambda bi, i: (bi, i, 0)),
            _const_spec((1, D_MODEL)),
            _const_spec((D_MODEL, NAT_COLS)),
            _const_spec((D_MODEL, GATE_COLS)),
            _const_spec((D_MODEL, GRP_COLS)),
            _const_spec((D_MODEL, GRP_COLS)),
        ],
        out_specs=[
            pl.BlockSpec((1, tm, NAT_COLS), lambda bi, i: (bi, i, 0)),
            pl.BlockSpec((1, tm, GATE_COLS), lambda bi, i: (bi, i, 0)),
            pl.BlockSpec((1, d1, tm // d1, GRP_COLS), lambda bi, i: (bi, 0, i, 0)),
            pl.BlockSpec((1, d2, tm // d2, GRP_COLS), lambda bi, i: (bi, 0, i, 0)),
        ],
        out_shape=[
            jax.ShapeDtypeStruct((b, s, NAT_COLS), BF16),
            jax.ShapeDtypeStruct((b, s, GATE_COLS), BF16),
            jax.ShapeDtypeStruct((b, d1, s // d1, GRP_COLS), BF16),
            jax.ShapeDtypeStruct((b, d2, s // d2, GRP_COLS), BF16),
        ],
        scratch_shapes=[pltpu.VMEM((D_MODEL // LANES, tm, LANES), F32)],
        compiler_params=_params(("parallel", "parallel")),
        name="norm_proj",
    )(x, g, w_nat, w_gate, w_d1, w_d2)


def _diff_bias_tiles(rel_bias, tq, tk):
    ratio = tq // tk
    t = jnp.arange(-2, ratio + 2)[:, None, None]
    j = jnp.arange(tk)[None, :, None]
    i = jnp.arange(tq)[None, None, :]
    tiles = _bias_of_rel(rel_bias[:, :H_DIFF], t * tk + j - i) * LOG2E
    far = _bias_of_rel(rel_bias[:, :H_DIFF], jnp.asarray([-MAX_DISTANCE, MAX_DISTANCE])) * LOG2E
    return tiles, far


def _diff_attn_kernel(lam_ref, far_ref, q_ref, k_ref, v_ref, bias_ref, g_ref, o_ref,
                      vt_ref, st_ref, p_ref, acc_ref, *, tq, tk, nkb):
    h = pl.program_id(1)
    qi = pl.program_id(2)
    ratio = tq // tk
    n_band = ratio + 2
    vchunk = min(512, nkb * tk)

    @pl.when(qi == 0)
    def _():
        for c in range(nkb * tk // vchunk):
            cs = slice(c * vchunk, (c + 1) * vchunk)
            vt_ref[:LANES, cs] = v_ref[0, cs, :].astype(F32).T.astype(BF16)
        vt_ref[LANES:, :] = jnp.ones((DEN_ROWS, nkb * tk), BF16)

    qt = q_ref[0].astype(F32).T
    first = lax.broadcasted_iota(jnp.int32, (LANES, tq), 0) < HEAD_DIM
    qts = (jnp.where(first, qt, 0.0).astype(BF16), jnp.where(first, 0.0, qt).astype(BF16))
    kb0 = qi * ratio

    def key_block(j):
        kb = kb0 - 1 + j
        return jnp.where(kb < 0, kb + nkb, jnp.where(kb >= nkb, kb - nkb, kb))

    def scores(j, slot, band):
        kb = key_block(j)
        ks = pl.multiple_of(kb * tk, tk)
        k = k_ref[0, pl.ds(ks, tk), :]
        if band:
            bt = bias_ref[0, jnp.clip(kb - kb0, -2, ratio + 1) + 2]
        mx = []
        for mp in range(2):
            st = jnp.dot(k, qts[mp], preferred_element_type=F32)
            if band:
                st = st + bt
            st_ref[slot, mp] = st
            mx.append(jnp.max(st, axis=0, keepdims=True))
        return tuple(mx)

    def softmax(j, slot, stats, mx, band):
        if not band:
            kb = key_block(j)
            far_c = jnp.where(kb < kb0, far_ref[h, 0], far_ref[h, 1])
        out, alphas = [], []
        for mp in range(2):
            m = stats[mp]
            if band:
                m_new = jnp.maximum(m, mx[mp])
                shift = m_new
            else:
                m_new = jnp.maximum(m, mx[mp] + far_c)
                shift = m_new - far_c
            p_ref[slot, mp] = jnp.exp2((st_ref[slot, mp] - shift).astype(BF16))
            out.append(m_new)
            alphas.append(jnp.exp2(m - m_new))
        return tuple(out), tuple(alphas)

    def values(j, slot, alphas):
        ks = pl.multiple_of(key_block(j) * tk, tk)
        vt = vt_ref[:, pl.ds(ks, tk)]
        for mp in range(2):
            acc_ref[mp] = alphas[mp] * acc_ref[mp] + jnp.dot(vt, p_ref[slot, mp],
                                                             preferred_element_type=F32)

    def group(jj, carry, base, band):
        stats, alphas, mx = carry[:2], carry[2:4], carry[4:]
        j = base + DIFF_GROUP * jj
        for u in range(DIFF_GROUP):
            mx_next = scores(j + u + 1, (u + 1) % DIFF_GROUP, band and u + 1 < DIFF_GROUP)
            values(j + u - 1, (u - 1) % DIFF_GROUP, alphas)
            stats, alphas = softmax(j + u, u, stats, mx, band)
            mx = mx_next
        return stats + alphas + mx

    last = DIFF_GROUP - 1
    acc_ref[...] = jnp.zeros_like(acc_ref)
    p_ref[last] = jnp.zeros_like(p_ref[last])
    mx0 = scores(0, 0, True)
    ninf, one = (jnp.full((1, tq), v, F32) for v in (-jnp.inf, 1.0))
    carry = group(0, (ninf, ninf, one, one) + mx0, base=0, band=True)
    carry = lax.fori_loop(0, (nkb - n_band) // DIFF_GROUP,
                          functools.partial(group, base=n_band, band=False), carry)
    values(nkb - 1, last, carry[2:4])
    l0, l1 = acc_ref[0, LANES:LANES + 1, :], acc_ref[1, LANES:LANES + 1, :]

    lp = lam_ref[...]
    lam = (jnp.exp(jnp.sum(lp[0:1] * lp[1:2], axis=-1, keepdims=True))
           - jnp.exp(jnp.sum(lp[2:3] * lp[3:4], axis=-1, keepdims=True)) + LAM_INIT)
    ot = acc_ref[0, :LANES, :] / l0 - lam * (acc_ref[1, :LANES, :] / l1)
    ms = jnp.mean(ot * ot, axis=0, keepdims=True)
    y = (ot * lax.rsqrt(ms + EPS)).T * g_ref[...] * (1.0 - LAM_INIT)
    o_ref[0] = y.astype(BF16)


def _diff_attn(nat, bias_tiles, far, lam_params, subln_g):
    b, s, _ = nat.shape
    tq, tk = min(TQ_DIFF, s), min(TK_DIFF, s)
    nt = bias_tiles.shape[1]
    ratio, nkb = tq // tk, s // tk
    assert tq % tk == 0 and tk >= MAX_DISTANCE and nkb >= ratio + 2
    assert ratio + 2 == DIFF_GROUP and nkb % DIFF_GROUP == 0
    kern = functools.partial(_diff_attn_kernel, tq=tq, tk=tk, nkb=s // tk)
    return pl.pallas_call(
        kern,
        grid=(b, H_DIFF, s // tq),
        in_specs=[
            _const_spec((4, HEAD_DIM)),
            pl.BlockSpec(memory_space=pltpu.SMEM),
            pl.BlockSpec((1, tq, LANES), lambda bi, h, qi: (bi, qi, h)),
            pl.BlockSpec((1, s, LANES), lambda bi, h, qi: (bi, 0, H_DIFF + h)),
            pl.BlockSpec((1, s, LANES), lambda bi, h, qi: (bi, 0, 2 * H_DIFF + h)),
            pl.BlockSpec((1, nt, tk, tq), lambda bi, h, qi: (h, 0, 0, 0)),
            _const_spec((1, 2 * HEAD_DIM)),
        ],
        out_specs=pl.BlockSpec((1, tq, LANES), lambda bi, h, qi: (bi, qi, h)),
        out_shape=jax.ShapeDtypeStruct((b, s, DIFF_V), BF16),
        scratch_shapes=[
            pltpu.VMEM((LANES + DEN_ROWS, s), BF16),
            pltpu.VMEM((DIFF_GROUP, 2, tk, tq), F32),
            pltpu.VMEM((DIFF_GROUP, 2, tk, tq), BF16),
            pltpu.VMEM((2, LANES + DEN_ROWS, tq), F32),
        ],
        compiler_params=_params(("parallel", "parallel", "arbitrary")),
        name="diff_attn",
    )(lam_params, far, nat, nat, nat, bias_tiles, subln_g)


def _dil_geometry(s, window, dilation):
    l = s // dilation
    w = window // (2 * dilation)
    qb = min(DIL_QB, l)
    kw = min(qb + 2 * LANES, l)
    assert w <= LANES and l % qb == 0 and l % LANES == 0
    return l, w, qb, kw, l // qb


def _dil_bias_tiles(rel_bias, g, s, window, dilation):
    l, w, qb, kw, nb = _dil_geometry(s, window, dilation)
    c0 = H_DIFF + g * H_DIL
    shifts = []
    for n in (0, min(1, nb - 1), nb - 1):
        l0 = n * qb
        shifts.append(min(max(l0 - LANES, 0), l - kw) - l0)
    j = jnp.arange(kw)[None, :, None]
    i = jnp.arange(qb)[None, None, :]
    off = jnp.asarray(shifts)[:, None, None] + j - i
    b = _bias_of_rel(rel_bias[:, c0:c0 + H_DIL], off * dilation) * LOG2E
    b = jnp.where((jnp.abs(off) <= w)[None], b, NEG_INF)
    return jnp.stack([jnp.concatenate([b[2 * p], b[2 * p + 1]], axis=-1) for p in range(H_DIL // 2)], axis=1)


def _dil_attn_kernel(q_ref, k_ref, v_ref, bias_ref, o_ref, lse_ref, qt_ref, vt_ref, st_ref,
                     *, l, qb, kw, nb):
    chunk = min(512, l)
    for pair in range(H_DIL // 2):
        cs = slice(pair * LANES, (pair + 1) * LANES)
        for c in range(l // chunk):
            rs = slice(c * chunk, (c + 1) * chunk)
            qt_ref[pair, :, rs] = q_ref[0, 0, rs, cs].astype(F32).T.astype(BF16)
            vt_ref[pair, :LANES, rs] = v_ref[0, 0, rs, cs].astype(F32).T.astype(BF16)
        vt_ref[pair, LANES:, :] = jnp.ones((DEN_ROWS, l), BF16)
    first = lax.broadcasted_iota(jnp.int32, (LANES, qb), 0) < HEAD_DIM

    def window(n):
        l0 = pl.multiple_of(n * qb, qb)
        return l0, pl.multiple_of(jnp.clip(l0 - LANES, 0, l - kw), LANES)

    def scores(n, slot):
        l0, ws = window(n)
        for pair in range(H_DIL // 2):
            qt = qt_ref[pair, :, pl.ds(l0, qb)]
            q2 = jnp.concatenate([jnp.where(first, qt, jnp.zeros_like(qt)),
                                  jnp.where(first, jnp.zeros_like(qt), qt)], axis=1)
            kp = k_ref[0, 0, pl.ds(ws, kw), pair * LANES:(pair + 1) * LANES]
            st_ref[slot, pair] = jnp.dot(kp, q2, preferred_element_type=F32)

    def finish(n, slot):
        l0, ws = window(n)
        t = jnp.where(n == 0, 0, jnp.where(n == nb - 1, 2, 1))
        for pair in range(H_DIL // 2):
            cs = slice(pair * LANES, (pair + 1) * LANES)
            st = st_ref[slot, pair] + bias_ref[t, pair]
            mx = jnp.max(st, axis=0, keepdims=True)
            p = jnp.exp2(st - mx).astype(BF16)
            acc = jnp.dot(vt_ref[pair, :, pl.ds(ws, kw)], p, preferred_element_type=F32)
            den = acc[LANES:LANES + 1, :]
            ot = acc[:LANES, :] / den
            lse = mx * LN2 + jnp.log(den)
            o_t = jnp.where(first, ot[:, :qb], ot[:, qb:])
            lse_t = jnp.where(first, lse[:, :qb], lse[:, qb:])
            o_ref[0, 0, pl.ds(l0, qb), cs] = o_t.T.astype(BF16)
            lse_ref[0, 0, pl.ds(l0, qb), cs] = lse_t.T

    def two_blocks(i, carry):
        n = 2 * i
        scores(n + 1, 1)
        finish(n, 0)
        scores(n + 2, 0)
        finish(n + 1, 1)
        return carry

    scores(0, 0)
    lax.fori_loop(0, (nb - 1) // 2, two_blocks, 0)
    if nb % 2 == 0:
        scores(nb - 1, 1)
        finish(nb - 2, 0)
        finish(nb - 1, 1)
    else:
        finish(nb - 1, 0)


def _dil_attn(arr, col0, bias_tiles, g, window, dilation):
    b, d, l, _ = arr.shape
    _, w, qb, kw, nb = _dil_geometry(l * d, window, dilation)
    kern = functools.partial(_dil_attn_kernel, l=l, qb=qb, kw=kw, nb=nb)
    return pl.pallas_call(
        kern,
        grid=(b, d),
        in_specs=[
            pl.BlockSpec((1, 1, l, GRP_W), lambda bi, r: (bi, r, 0, col0)),
            pl.BlockSpec((1, 1, l, GRP_W), lambda bi, r: (bi, r, 0, col0 + 1)),
            pl.BlockSpec((1, 1, l, GRP_W), lambda bi, r: (bi, r, 0, col0 + 2)),
            _const_spec((3, H_DIL // 2, kw, 2 * qb)),
        ],
        out_specs=[
            pl.BlockSpec((1, 1, l, GRP_W), lambda bi, r: (bi, r, 0, 0)),
            pl.BlockSpec((1, 1, l, GRP_W), lambda bi, r: (bi, r, 0, 0)),
        ],
        out_shape=[
            jax.ShapeDtypeStruct((b, d, l, GRP_W), BF16),
            jax.ShapeDtypeStruct((b, d, l, GRP_W), F32),
        ],
        scratch_shapes=[
            pltpu.VMEM((H_DIL // 2, LANES, l), BF16),
            pltpu.VMEM((H_DIL // 2, LANES + DEN_ROWS, l), BF16),
            pltpu.VMEM((2, H_DIL // 2, kw, 2 * qb), F32),
        ],
        compiler_params=_params(("parallel", "parallel")),
        name=f"dil_attn_g{g}",
    )(arr, arr, arr, bias_tiles)


def _pack_words(a, b):
    ua = lax.bitcast_convert_type(a.astype(BF16).astype(F32), jnp.uint32)
    ub = lax.bitcast_convert_type(b.astype(BF16).astype(F32), jnp.uint32)
    return (ua >> 16) | (ub & jnp.uint32(0xFFFF0000))


def _unpack_words(w):
    return (lax.bitcast_convert_type(w << 16, F32),
            lax.bitcast_convert_type(w & jnp.uint32(0xFFFF0000), F32))


def _route(lg):
    lane = lax.broadcasted_iota(jnp.int32, lg.shape, 1)
    big = jnp.int32(ROUTE_LANES)
    gl = jnp.where(lane < N_GROUPS, lg, -jnp.inf)
    gmax = jnp.max(gl, axis=-1, keepdims=True)
    gi = jnp.min(jnp.where(gl == gmax, lane, big), axis=-1, keepdims=True)
    gw = 1.0 / jnp.sum(jnp.exp(gl - gmax), axis=-1, keepdims=True)
    e_lo = EXP_LANE0 + E_PER_GROUP * gi
    es = jnp.where((lane >= e_lo) & (lane < e_lo + E_PER_GROUP), lg, -jnp.inf)
    v1 = jnp.max(es, axis=-1, keepdims=True)
    i1 = jnp.min(jnp.where(es == v1, lane, big), axis=-1, keepdims=True)
    es2 = jnp.where(lane == i1, -jnp.inf, es)
    v2 = jnp.max(es2, axis=-1, keepdims=True)
    i2 = jnp.min(jnp.where(es2 == v2, lane, big), axis=-1, keepdims=True)
    t = jnp.exp(v2 - v1)
    w1 = gw / (1.0 + t)
    w2 = w1 * t
    return gi, jnp.where(lane == i1, w1, 0.0) + jnp.where(lane == i2, w2, 0.0)


def _merge_route_kernel(x_ref, oa_ref, o0_ref, l0_ref, o1_ref, l1_ref, o2_ref, l2_ref, gt_ref,
                        wa_ref, wb_ref, wo_ref, nf_ref, wr_ref, x1_ref, hw_ref, gid_ref,
                        so1_ref, sl1_ref, so2_ref, sl2_ref, *, tm):
    for src, dst, d in ((o1_ref, so1_ref, DIL_CONFIGS[1][1]), (l1_ref, sl1_ref, DIL_CONFIGS[1][1]),
                        (o2_ref, so2_ref, DIL_CONFIGS[2][1]), (l2_ref, sl2_ref, DIL_CONFIGS[2][1])):
        for r in range(d):
            for c in range(GRP_W // LANES):
                dst[c, pl.ds(r, tm // d, stride=d), :] = src[0, r, :, c * LANES:(c + 1) * LANES].astype(F32)
    subs = [slice(i * SUB_MERGE, (i + 1) * SUB_MERGE) for i in range(tm // SUB_MERGE)]
    unsplit = lambda ref, rs: jnp.concatenate([ref[c, rs, :] for c in range(GRP_W // LANES)], axis=1)

    def dil_mix(rs):
        l0, l1, l2 = l0_ref[0, 0, rs, :], unsplit(sl1_ref, rs), unsplit(sl2_ref, rs)
        mx = jnp.maximum(jnp.maximum(l0, l1), l2)
        e0, e1, e2 = jnp.exp(l0 - mx), jnp.exp(l1 - mx), jnp.exp(l2 - mx)
        ob = (e0 * o0_ref[0, 0, rs, :].astype(F32) + e1 * unsplit(so1_ref, rs)
              + e2 * unsplit(so2_ref, rs)) / (e0 + e1 + e2)
        return ob.astype(BF16)

    obs = [dil_mix(rs) for rs in subs]
    pas = [jnp.dot(oa_ref[0, rs, :], wa_ref[...], preferred_element_type=F32) for rs in subs]
    pbs = [jnp.dot(ob, wb_ref[...], preferred_element_type=F32) for ob in obs]

    def gate(rs, pa, pb):
        ga = jax.nn.sigmoid(gt_ref[0, rs, :D_MODEL].astype(F32))
        gb = jax.nn.sigmoid(gt_ref[0, rs, D_MODEL:].astype(F32))
        return (ga * pa + gb * pb).astype(BF16)

    mgs = [gate(rs, pa, pb) for rs, pa, pb in zip(subs, pas, pbs)]
    x1s = [x_ref[0, rs, :] + jnp.dot(mg, wo_ref[...], preferred_element_type=F32)
           for rs, mg in zip(subs, mgs)]

    def ffn_norm(rs, x1):
        x1_ref[0, rs, :] = x1
        ms = jnp.mean(x1 * x1, axis=-1, keepdims=True)
        h2f = x1 * lax.rsqrt(ms + EPS) * nf_ref[...]
        hw_ref[0, rs, :] = _pack_words(h2f[:, :D_MODEL // 2], h2f[:, D_MODEL // 2:])
        return h2f.astype(BF16)

    h2s = [ffn_norm(rs, x1) for rs, x1 in zip(subs, x1s)]
    lgs = [jnp.dot(h2, wr_ref[...], preferred_element_type=F32) for h2 in h2s]
    for rs, lg in zip(subs, lgs):
        gi, _ = _route(lg)
        git = jnp.broadcast_to(gi.astype(F32), lg.shape).T
        gid_ref[0, 0, :, rs] = git[:SUBLANES, :].astype(jnp.int32)


def _merge_route(x, oa, dil, gates, wa, wb, wo, nf, wr):
    b, s, _ = x.shape
    tm = TM_MERGE
    d1, d2 = DIL_CONFIGS[1][1], DIL_CONFIGS[2][1]
    tok = lambda bi, i: (bi, i, 0)
    cls = lambda bi, i: (bi, 0, i, 0)
    (o0, s0), (o1, s1), (o2, s2) = dil
    return pl.pallas_call(
        functools.partial(_merge_route_kernel, tm=tm),
        grid=(b, s // tm),
        in_specs=[
            pl.BlockSpec((1, tm, D_MODEL), tok),
            pl.BlockSpec((1, tm, DIFF_V), tok),
            pl.BlockSpec((1, 1, tm, GRP_W), cls), pl.BlockSpec((1, 1, tm, GRP_W), cls),
            pl.BlockSpec((1, d1, tm // d1, GRP_W), cls), pl.BlockSpec((1, d1, tm // d1, GRP_W), cls),
            pl.BlockSpec((1, d2, tm // d2, GRP_W), cls), pl.BlockSpec((1, d2, tm // d2, GRP_W), cls),
            pl.BlockSpec((1, tm, GATE_COLS), tok),
            _const_spec((DIFF_V, D_MODEL)),
            _const_spec((GRP_W, D_MODEL)),
            _const_spec((D_MODEL, D_MODEL)),
            _const_spec((1, D_MODEL)),
            _const_spec((D_MODEL, ROUTE_LANES)),
        ],
        out_specs=[
            pl.BlockSpec((1, tm, D_MODEL), tok),
            pl.BlockSpec((1, tm, D_MODEL // 2), tok),
            pl.BlockSpec((1, 1, SUBLANES, tm), lambda bi, i: (bi, i, 0, 0)),
        ],
        out_shape=[
            jax.ShapeDtypeStruct((b, s, D_MODEL), F32),
            jax.ShapeDtypeStruct((b, s, D_MODEL // 2), jnp.uint32),
            jax.ShapeDtypeStruct((b, s // tm, SUBLANES, tm), jnp.int32),
        ],
        scratch_shapes=[pltpu.VMEM((GRP_W // LANES, tm, LANES), F32) for _ in range(4)],
        compiler_params=_params(("parallel", "parallel")),
        name="merge_route",
    )(x, oa, o0, s0, o1, s1, o2, s2, gates, wa, wb, wo, nf, wr)


def _dispatch_plan_kernel(gid_ref, dest_ref, tgrp_ref, *, rows, cols):
    gid = gid_ref[...]
    ci = lax.broadcasted_iota(jnp.int32, (cols, cols), 0)
    cj = lax.broadcasted_iota(jnp.int32, (cols, cols), 1)
    upto = (ci <= cj).astype(BF16)
    ri = lax.broadcasted_iota(jnp.int32, (rows, rows), 0)
    rj = lax.broadcasted_iota(jnp.int32, (rows, rows), 1)
    above = (rj < ri).astype(F32)
    seg = jnp.zeros((1, 1), F32)
    dest = jnp.zeros((rows, cols), F32)
    ends = []
    for g in range(N_GROUPS):
        mem = (gid == g).astype(F32)
        pref = jnp.dot(mem.astype(BF16), upto, preferred_element_type=F32)
        tot = pref[:, cols - 1:cols]
        before = jnp.dot(above, jnp.broadcast_to(tot, (rows, LANES)), preferred_element_type=F32,
                         precision=lax.Precision.HIGHEST)[:, :1]
        dest = dest + mem * (seg + before + pref - 1.0)
        cnt = jnp.sum(tot, axis=0, keepdims=True)
        seg = seg + jnp.ceil(cnt / TM_MOE) * TM_MOE
        ends.append(seg)
    dest_ref[...] = dest.astype(jnp.int32)
    tile = (lax.broadcasted_iota(jnp.int32, (SUBLANES, LANES), 0) * LANES
            + lax.broadcasted_iota(jnp.int32, (SUBLANES, LANES), 1))
    start = (tile * TM_MOE).astype(F32)
    tgrp = jnp.zeros((SUBLANES, LANES), jnp.int32)
    for g in range(N_GROUPS - 1):
        tgrp = tgrp + (start >= ends[g]).astype(jnp.int32)
    tgrp_ref[...] = tgrp


def _dispatch_plan(gid):
    rows, cols = gid.shape
    assert rows % SUBLANES == 0 and rows * cols // TM_MOE + N_GROUPS <= SUBLANES * LANES
    return pl.pallas_call(
        functools.partial(_dispatch_plan_kernel, rows=rows, cols=cols),
        out_shape=[jax.ShapeDtypeStruct((rows, cols), jnp.int32),
                   jax.ShapeDtypeStruct((SUBLANES, LANES), jnp.int32)],
        compiler_params=pltpu.CompilerParams(vmem_limit_bytes=VMEM_LIMIT),
        name="dispatch_plan",
    )(gid)


def _moe_group_kernel(tgrp_ref, hw_ref, wr_ref, wg_ref, wu_ref, wd_ref, ow_ref, he_ref):
    g = tgrp_ref[pl.program_id(0)]
    lo, hi = _unpack_words(hw_ref[...])
    h2 = jnp.concatenate([lo, hi], axis=1).astype(BF16)
    _, dw = _route(jnp.dot(h2, wr_ref[...], preferred_element_type=F32))
    lane = lax.broadcasted_iota(jnp.int32, dw.shape, 1)
    for e in range(E_PER_GROUP):
        we = jnp.sum(jnp.where(lane == EXP_LANE0 + E_PER_GROUP * g + e, dw, 0.0), axis=-1, keepdims=True)
        gp = jnp.dot(h2, wg_ref[0, e], preferred_element_type=F32)
        up = jnp.dot(h2, wu_ref[0, e], preferred_element_type=F32)
        he_ref[:, e * D_EXPERT:(e + 1) * D_EXPERT] = (jax.nn.silu(gp) * up * we).astype(BF16)
    moe = jnp.dot(he_ref[...], wd_ref[0], preferred_element_type=F32)
    ow_ref[...] = _pack_words(moe[:, :D_MODEL // 2], moe[:, D_MODEL // 2:])


def _moe_groups(tgrp, hw_sorted, wr, wg, wu, wd):
    m = hw_sorted.shape[0]
    tm = TM_MOE
    row = lambda i, t: (i, 0)
    grp = lambda i, t: (t[i], 0, 0, 0)
    return pl.pallas_call(
        _moe_group_kernel,
        grid_spec=pltpu.PrefetchScalarGridSpec(
            num_scalar_prefetch=1,
            grid=(m // tm,),
            in_specs=[
                pl.BlockSpec((tm, D_MODEL // 2), row),
                pl.BlockSpec((D_MODEL, ROUTE_LANES), lambda i, t: (0, 0)),
                pl.BlockSpec((1, E_PER_GROUP, D_MODEL, D_EXPERT), grp),
                pl.BlockSpec((1, E_PER_GROUP, D_MODEL, D_EXPERT), grp),
                pl.BlockSpec((1, E_PER_GROUP * D_EXPERT, D_MODEL), lambda i, t: (t[i], 0, 0)),
            ],
            out_specs=pl.BlockSpec((tm, D_MODEL // 2), row),
            scratch_shapes=[pltpu.VMEM((tm, E_PER_GROUP * D_EXPERT), BF16)],
        ),
        out_shape=jax.ShapeDtypeStruct((m, D_MODEL // 2), jnp.uint32),
        compiler_params=_params(("arbitrary",)),
        name="moe_groups",
    )(tgrp, hw_sorted, wr, wg, wu, wd)


def _final_norm_kernel(x1_ref, yw_ref, nf_ref, y_ref):
    lo, hi = _unpack_words(yw_ref[...])
    x2 = x1_ref[...] + jnp.concatenate([lo, hi], axis=1)
    ms = jnp.mean(x2 * x2, axis=-1, keepdims=True)
    y_ref[...] = x2 * lax.rsqrt(ms + EPS) * nf_ref[...]


def _final_norm(x1, yw, nf):
    n = x1.shape[0]
    tm = TM_MOE
    row = lambda i: (i, 0)
    return pl.pallas_call(
        _final_norm_kernel,
        grid=(n // tm,),
        in_specs=[pl.BlockSpec((tm, D_MODEL), row), pl.BlockSpec((tm, D_MODEL // 2), row),
                  _const_spec((1, D_MODEL))],
        out_specs=pl.BlockSpec((tm, D_MODEL), row),
        out_shape=jax.ShapeDtypeStruct((n, D_MODEL), F32),
        compiler_params=_params(("parallel",)),
        name="final_norm",
    )(x1, yw, nf)


def _sc_mesh():
    return plsc.VectorSubcoreMesh(core_axis_name="core", subcore_axis_name="subcore")


def _sc_scatter_rows(x, idx, m):
    n, d = x.shape

    @pl.kernel(out_type=jax.ShapeDtypeStruct((m, d), x.dtype), mesh=_sc_mesh(), scratch_types=[])
    def scatter(x_hbm, i_hbm, o_hbm):
        for c in range(d // SC_COLS):
            cols = pl.ds(c * SC_COLS, SC_COLS)
            o_cols = o_hbm.at[:, cols]

            def body(x_vmem, i_vmem, o_cols=o_cols):
                pltpu.sync_copy(x_vmem, o_cols.at[i_vmem.at[0]])

            pltpu.emit_pipeline(
                body,
                grid=(n // SC_WINDOW,),
                in_specs=[pl.BlockSpec((SC_WINDOW, SC_COLS), lambda i, c=c: (i, c)),
                          pl.BlockSpec((1, SC_WINDOW), lambda i: (0, i))],
                out_specs=[],
                core_axis_name=("core", "subcore"),
                dimension_semantics=(pltpu.PARALLEL,),
            )(x_hbm, i_hbm)

    return scatter(x, idx.reshape(1, n))


def _sc_gather_rows(x, idx):
    n = idx.shape[0]
    d = x.shape[1]

    @pl.kernel(out_type=jax.ShapeDtypeStruct((n, d), x.dtype), mesh=_sc_mesh(), scratch_types=[])
    def gather(x_hbm, i_hbm, o_hbm):
        for c in range(d // SC_COLS):
            x_cols = x_hbm.at[:, pl.ds(c * SC_COLS, SC_COLS)]

            def body(i_vmem, o_vmem, x_cols=x_cols):
                pltpu.sync_copy(x_cols.at[i_vmem.at[0]], o_vmem)

            pltpu.emit_pipeline(
                body,
                grid=(n // SC_WINDOW,),
                in_specs=[pl.BlockSpec((1, SC_WINDOW), lambda i: (0, i))],
                out_specs=[pl.BlockSpec((SC_WINDOW, SC_COLS), lambda i, c=c: (i, c))],
                core_axis_name=("core", "subcore"),
                dimension_semantics=(pltpu.PARALLEL,),
            )(i_hbm, o_hbm)

    return gather(x, idx.reshape(1, n))


def _prepare_weights(rel_bias, norm_mix, w_in, lam_q1, lam_k1, lam_q2, lam_k2, subln_g,
                     w_branch_a, w_branch_b, w_out, norm_ffn, w_group, w_router,
                     w_exp_gate, w_exp_up, w_exp_down, norm_final):
    w = w_in[0]
    c = QK_SCALE * LOG2E
    o = 0
    qa, o = w[:, o:o + DIFF_QK] * c, o + DIFF_QK
    ka, o = w[:, o:o + DIFF_QK], o + DIFF_QK
    va, o = w[:, o:o + DIFF_V], o + DIFF_V
    qb, o = w[:, o:o + DIL_W] * c, o + DIL_W
    kb, o = w[:, o:o + DIL_W], o + DIL_W
    vb, o = w[:, o:o + DIL_W], o + DIL_W
    w_gate = w[:, o:].astype(BF16)
    grp = lambda g: [t[:, g * GRP_W:(g + 1) * GRP_W] for t in (qb, kb, vb)]
    w_nat = jnp.concatenate([qa, ka, va] + grp(0), axis=1).astype(BF16)
    w_d1 = jnp.concatenate(grp(1), axis=1).astype(BF16)
    w_d2 = jnp.concatenate(grp(2), axis=1).astype(BF16)
    wr = jnp.concatenate([w_group[0], w_router[0].transpose(1, 0, 2).reshape(D_MODEL, N_EXPERTS)], axis=1)
    wr = jnp.pad(wr, ((0, 0), (0, ROUTE_LANES - wr.shape[1]))).astype(BF16)
    return dict(
        rel_bias=rel_bias.astype(F32),
        norm_mix=norm_mix[0][None].astype(F32),
        w_nat=w_nat, w_gate=w_gate, w_d1=w_d1, w_d2=w_d2,
        lam=jnp.stack([lam_q1[0], lam_k1[0], lam_q2[0], lam_k2[0]]).astype(F32),
        subln_g=subln_g[0][None].astype(F32),
        wa=w_branch_a[0].astype(BF16), wb=w_branch_b[0].astype(BF16), wo=w_out[0].astype(BF16),
        norm_ffn=norm_ffn[0][None].astype(F32), wr=wr,
        wg=w_exp_gate[0].reshape(N_GROUPS, E_PER_GROUP, D_MODEL, D_EXPERT).astype(BF16),
        wu=w_exp_up[0].reshape(N_GROUPS, E_PER_GROUP, D_MODEL, D_EXPERT).astype(BF16),
        wd=w_exp_down[0].reshape(N_GROUPS, E_PER_GROUP * D_EXPERT, D_MODEL).astype(BF16),
        norm_final=norm_final[None].astype(F32),
    )


def _encoder(x, p):
    b, s, d = x.shape
    n = b * s
    nat, gates, dil1, dil2 = _norm_proj(x, p["norm_mix"], p["w_nat"], p["w_gate"], p["w_d1"], p["w_d2"])
    tiles, far = _diff_bias_tiles(p["rel_bias"], min(TQ_DIFF, s), min(TK_DIFF, s))
    oa = _diff_attn(nat, tiles, far, p["lam"], p["subln_g"])
    srcs = ((nat.reshape(b, 1, s, NAT_COLS), (2 * DIFF_QK + DIFF_V) // GRP_W), (dil1, 0), (dil2, 0))
    dil = []
    for g, (win, dilation) in enumerate(DIL_CONFIGS):
        tiles = _dil_bias_tiles(p["rel_bias"], g, s, win, dilation)
        dil.append(_dil_attn(srcs[g][0], srcs[g][1], tiles, g, win, dilation))
    x1, hw, gid = _merge_route(x, oa, dil, gates, p["wa"], p["wb"], p["wo"], p["norm_ffn"], p["wr"])
    dest, tgrp = _dispatch_plan(gid[:, :, 0, :].reshape(n // TM_MERGE, TM_MERGE))
    dest = dest.reshape(n)
    n_tiles = n // TM_MOE + N_GROUPS
    hw_sorted = _sc_scatter_rows(hw.reshape(n, d // 2), dest, n_tiles * TM_MOE)
    yw_sorted = _moe_groups(tgrp.reshape(-1)[:n_tiles], hw_sorted, p["wr"], p["wg"], p["wu"], p["wd"])
    y = _final_norm(x1.reshape(n, d), _sc_gather_rows(yw_sorted, dest), p["norm_final"])
    return y.reshape(b, s, d)


def kernel(x_prompt, x_sample, rel_bias, norm_mix, w_in, lam_q1, lam_k1, lam_q2, lam_k2, subln_g,
           w_branch_a, w_branch_b, w_out, norm_ffn, w_group, w_router, w_exp_gate, w_exp_up,
           w_exp_down, norm_final):
    p = _prepare_weights(rel_bias, norm_mix, w_in, lam_q1, lam_k1, lam_q2, lam_k2, subln_g,
                         w_branch_a, w_branch_b, w_out, norm_ffn, w_group, w_router,
                         w_exp_gate, w_exp_up, w_exp_down, norm_final)
    return (_encoder(x_prompt, p), _encoder(x_sample, p))
```

```python
import functools
import math

import jax
import jax.numpy as jnp
from jax import lax
from jax.experimental import pallas as pl
from jax.experimental.pallas import tpu as pltpu
from jax.experimental.pallas import tpu_sc as plsc

F32 = jnp.float32
BF16 = jnp.bfloat16

D_MODEL = 1024
HEAD_DIM = 64
H_DIFF = 4
H_DIL = 4
DIL_CONFIGS = ((128, 1), (512, 4), (2048, 16))
N_DIL = len(DIL_CONFIGS)
DIFF_QK = H_DIFF * 2 * HEAD_DIM
DIFF_V = H_DIFF * 2 * HEAD_DIM
DIL_W = N_DIL * H_DIL * HEAD_DIM
GRP_W = H_DIL * HEAD_DIM
GRP_COLS = 3 * GRP_W
NAT_COLS = 2 * DIFF_QK + DIFF_V + GRP_COLS
GATE_COLS = 2 * D_MODEL
Q_BLOCK = 128
N_BUCKETS = 32
MAX_DISTANCE = 128
N_GROUPS = 4
E_PER_GROUP = 4
N_EXPERTS = N_GROUPS * E_PER_GROUP
D_EXPERT = 256
EPS = 1e-6
NEG_INF = -1e30
LAM_INIT = 0.8 - 0.6 * math.exp(-0.3 * 0)
LOG2E = math.log2(math.e)
LN2 = math.log(2.0)
QK_SCALE = HEAD_DIM ** -0.5

LANES = 128
SUBLANES = 8
VMEM_LIMIT = 56 * 1024 * 1024

TM_PROJ = 512
TQ_DIFF = 512
TK_DIFF = 256
DIFF_SLOTS = 4
DEN_ROWS = 16
DIL_QB = 256
SC_WINDOW = 128
SC_COLS = 256
TM_MERGE = 512
SUB_MERGE = 256
TM_MOE = 512
ROUTE_LANES = 128
EXP_LANE0 = N_GROUPS


def _const_spec(shape):
    nd = len(shape)
    return pl.BlockSpec(shape, lambda *_: (0,) * nd, pipeline_mode=pl.Buffered(1))


def _params(sem):
    return pltpu.CompilerParams(dimension_semantics=sem, vmem_limit_bytes=VMEM_LIMIT)


def _rel_bucket(rel):
    half = N_BUCKETS // 2
    max_exact = half // 2
    ret = (rel > 0).astype(jnp.int32) * half
    n = jnp.abs(rel)
    nf = jnp.maximum(n, 1).astype(F32)
    large = max_exact + (jnp.log(nf / max_exact) / math.log(MAX_DISTANCE / max_exact)
                         * (half - max_exact)).astype(jnp.int32)
    large = jnp.minimum(large, half - 1)
    return ret + jnp.where(n < max_exact, n, large)


def _bias_of_rel(table, rel):
    bucket = _rel_bucket(rel)[None]
    t = table.astype(F32)
    col = (t.shape[1],) + (1,) * rel.ndim
    out = jnp.zeros((t.shape[1],) + rel.shape, F32)
    for k in range(N_BUCKETS):
        out = jnp.where(bucket == k, t[k].reshape(col), out)
    return out


def _chunks(total, width):
    out, c = [], 0
    while c < total:
        w = min(width, total - c)
        out.append((c, w))
        c += w
    return out


def _norm_proj_kernel(x_ref, g_ref, wn_ref, wg_ref, w1_ref, w2_ref,
                      on_ref, og_ref, o1_ref, o2_ref, hs_ref, *, tm):
    x = x_ref[0]
    ms = jnp.mean(x * x, axis=-1, keepdims=True)
    hf = x * lax.rsqrt(ms + EPS) * g_ref[...]
    for c in range(D_MODEL // LANES):
        hs_ref[c] = hf[:, c * LANES:(c + 1) * LANES]
    h = hf.astype(BF16)
    for c0, cw in _chunks(NAT_COLS, 512):
        on_ref[0, :, c0:c0 + cw] = jnp.dot(h, wn_ref[:, c0:c0 + cw],
                                           preferred_element_type=F32).astype(BF16)
    for c0, cw in _chunks(GATE_COLS, 512):
        og_ref[0, :, c0:c0 + cw] = jnp.dot(h, wg_ref[:, c0:c0 + cw],
                                           preferred_element_type=F32).astype(BF16)
    for w_ref, o_ref, d in ((w1_ref, o1_ref, DIL_CONFIGS[1][1]), (w2_ref, o2_ref, DIL_CONFIGS[2][1])):
        rows = tm // d
        hp = jnp.concatenate(
            [jnp.concatenate([hs_ref[c, pl.ds(r, rows, stride=d), :] for r in range(d)], axis=0)
             for c in range(D_MODEL // LANES)], axis=1).astype(BF16)
        for c0, cw in _chunks(GRP_COLS, 256):
            res = jnp.dot(hp, w_ref[:, c0:c0 + cw], preferred_element_type=F32).astype(BF16)
            o_ref[0, :, :, c0:c0 + cw] = res.reshape(d, rows, cw)


def _norm_proj(x, g, w_nat, w_gate, w_d1, w_d2):
    b, s, _ = x.shape
    tm = TM_PROJ
    d1, d2 = DIL_CONFIGS[1][1], DIL_CONFIGS[2][1]
    return pl.pallas_call(
        functools.partial(_norm_proj_kernel, tm=tm),
        grid=(b, s // tm),
        in_specs=[
            pl.BlockSpec((1, tm, D_MODEL), lambda bi, i: (bi, i, 0)),
            _const_spec((1, D_MODEL)),
            _const_spec((D_MODEL, NAT_COLS)),
            _const_spec((D_MODEL, GATE_COLS)),
            _const_spec((D_MODEL, GRP_COLS)),
            _const_spec((D_MODEL, GRP_COLS)),
        ],
        out_specs=[
            pl.BlockSpec((1, tm, NAT_COLS), lambda bi, i: (bi, i, 0)),
            pl.BlockSpec((1, tm, GATE_COLS), lambda bi, i: (bi, i, 0)),
            pl.BlockSpec((1, d1, tm // d1, GRP_COLS), lambda bi, i: (bi, 0, i, 0)),
            pl.BlockSpec((1, d2, tm // d2, GRP_COLS), lambda bi, i: (bi, 0, i, 0)),
        ],
        out_shape=[
            jax.ShapeDtypeStruct((b, s, NAT_COLS), BF16),
            jax.ShapeDtypeStruct((b, s, GATE_COLS), BF16),
            jax.ShapeDtypeStruct((b, d1, s // d1, GRP_COLS), BF16),
            jax.ShapeDtypeStruct((b, d2, s // d2, GRP_COLS), BF16),
        ],
        scratch_shapes=[pltpu.VMEM((D_MODEL // LANES, tm, LANES), F32)],
        compiler_params=_params(("parallel", "parallel")),
        name="norm_proj",
    )(x, g, w_nat, w_gate, w_d1, w_d2)


def _diff_bias_tiles(rel_bias, tq, tk):
    ratio = tq // tk
    t = jnp.arange(-2, ratio + 2)[:, None, None]
    j = jnp.arange(tk)[None, :, None]
    i = jnp.arange(tq)[None, None, :]
    tiles = _bias_of_rel(rel_bias[:, :H_DIFF], t * tk + j - i) * LOG2E
    far = _bias_of_rel(rel_bias[:, :H_DIFF], jnp.asarray([-MAX_DISTANCE, MAX_DISTANCE])) * LOG2E
    return tiles, far


def _diff_attn_kernel(lam_ref, far_ref, q_ref, k_ref, v_ref, bias_ref, g_ref, o_ref,
                      vt_ref, st_ref, p_ref, acc_ref, *, tq, tk, nkb):
    h = pl.program_id(1)
    qi = pl.program_id(2)
    ratio = tq // tk
    n_band = ratio + 2
    vchunk = min(512, nkb * tk)

    @pl.when(qi == 0)
    def _():
        for c in range(nkb * tk // vchunk):
            cs = slice(c * vchunk, (c + 1) * vchunk)
            vt_ref[:LANES, cs] = v_ref[0, cs, :].astype(F32).T.astype(BF16)
        vt_ref[LANES:, :] = jnp.ones((DEN_ROWS, nkb * tk), BF16)

    qt = q_ref[0].astype(F32).T
    first = lax.broadcasted_iota(jnp.int32, (LANES, tq), 0) < HEAD_DIM
    qts = (jnp.where(first, qt, 0.0).astype(BF16), jnp.where(first, 0.0, qt).astype(BF16))
    kb0 = qi * ratio

    def key_block(j):
        kb = kb0 - 1 + j
        return jnp.where(kb < 0, kb + nkb, jnp.where(kb >= nkb, kb - nkb, kb))

    def scores(j, slot, band):
        kb = key_block(j)
        ks = pl.multiple_of(kb * tk, tk)
        k = k_ref[0, pl.ds(ks, tk), :]
        if band:
            bt = bias_ref[0, jnp.clip(kb - kb0, -2, ratio + 1) + 2]
        mx = []
        for mp in range(2):
            st = jnp.dot(k, qts[mp], preferred_element_type=F32)
            if band:
                st = st + bt
            st_ref[slot, mp] = st
            mx.append(jnp.max(st, axis=0, keepdims=True))
        return tuple(mx)

    def softmax(j, slot, stats, mx, band):
        if not band:
            kb = key_block(j)
            far_c = jnp.where(kb < kb0, far_ref[h, 0], far_ref[h, 1])
        out, alphas = [], []
        for mp in range(2):
            m = stats[mp]
            if band:
                m_new = jnp.maximum(m, mx[mp])
                shift = m_new
            else:
                m_new = jnp.maximum(m, mx[mp] + far_c)
                shift = m_new - far_c
            p_ref[slot, mp] = jnp.exp2((st_ref[slot, mp] - shift).astype(BF16))
            out.append(m_new)
            alphas.append(jnp.exp2(m - m_new))
        return tuple(out), tuple(alphas)

    def values(j, slot, alphas):
        ks = pl.multiple_of(key_block(j) * tk, tk)
        vt = vt_ref[:, pl.ds(ks, tk)]
        for mp in range(2):
            acc_ref[mp] = alphas[mp] * acc_ref[mp] + jnp.dot(vt, p_ref[slot, mp],
                                                             preferred_element_type=F32)

    acc_ref[...] = jnp.zeros_like(acc_ref)
    mx = scores(0, 0, True)
    stats = (jnp.full((1, tq), -jnp.inf, F32),) * 2
    alphas = None
    for j in range(nkb):
        if j + 1 < nkb:
            mx_next = scores(j + 1, (j + 1) % DIFF_SLOTS, j + 1 < n_band)
        if j >= 1:
            values(j - 1, (j - 1) % DIFF_SLOTS, alphas)
        stats, alphas = softmax(j, j % DIFF_SLOTS, stats, mx, j < n_band)
        mx = mx_next
    values(nkb - 1, (nkb - 1) % DIFF_SLOTS, alphas)
    l0, l1 = acc_ref[0, LANES:LANES + 1, :], acc_ref[1, LANES:LANES + 1, :]

    lp = lam_ref[...]
    lam = (jnp.exp(jnp.sum(lp[0:1] * lp[1:2], axis=-1, keepdims=True))
           - jnp.exp(jnp.sum(lp[2:3] * lp[3:4], axis=-1, keepdims=True)) + LAM_INIT)
    ot = acc_ref[0, :LANES, :] / l0 - lam * (acc_ref[1, :LANES, :] / l1)
    ms = jnp.mean(ot * ot, axis=0, keepdims=True)
    y = (ot * lax.rsqrt(ms + EPS)).T * g_ref[...] * (1.0 - LAM_INIT)
    o_ref[0] = y.astype(BF16)


def _diff_attn(nat, bias_tiles, far, lam_params, subln_g):
    b, s, _ = nat.shape
    tq, tk = min(TQ_DIFF, s), min(TK_DIFF, s)
    nt = bias_tiles.shape[1]
    ratio, nkb = tq // tk, s // tk
    assert tq % tk == 0 and tk >= MAX_DISTANCE and nkb >= ratio + 2
    assert DIFF_SLOTS >= 3
    kern = functools.partial(_diff_attn_kernel, tq=tq, tk=tk, nkb=s // tk)
    return pl.pallas_call(
        kern,
        grid=(b, H_DIFF, s // tq),
        in_specs=[
            _const_spec((4, HEAD_DIM)),
            pl.BlockSpec(memory_space=pltpu.SMEM),
            pl.BlockSpec((1, tq, LANES), lambda bi, h, qi: (bi, qi, h)),
            pl.BlockSpec((1, s, LANES), lambda bi, h, qi: (bi, 0, H_DIFF + h)),
            pl.BlockSpec((1, s, LANES), lambda bi, h, qi: (bi, 0, 2 * H_DIFF + h)),
            pl.BlockSpec((1, nt, tk, tq), lambda bi, h, qi: (h, 0, 0, 0)),
            _const_spec((1, 2 * HEAD_DIM)),
        ],
        out_specs=pl.BlockSpec((1, tq, LANES), lambda bi, h, qi: (bi, qi, h)),
        out_shape=jax.ShapeDtypeStruct((b, s, DIFF_V), BF16),
        scratch_shapes=[
            pltpu.VMEM((LANES + DEN_ROWS, s), BF16),
            pltpu.VMEM((DIFF_SLOTS, 2, tk, tq), F32),
            pltpu.VMEM((DIFF_SLOTS, 2, tk, tq), BF16),
            pltpu.VMEM((2, LANES + DEN_ROWS, tq), F32),
        ],
        compiler_params=_params(("parallel", "parallel", "arbitrary")),
        name="diff_attn",
    )(lam_params, far, nat, nat, nat, bias_tiles, subln_g)


def _dil_geometry(s, window, dilation):
    l = s // dilation
    w = window // (2 * dilation)
    qb = min(DIL_QB, l)
    kw = min(qb + 2 * LANES, l)
    assert w <= LANES and l % qb == 0 and l % LANES == 0
    return l, w, qb, kw, l // qb


def _dil_bias_tiles(rel_bias, g, s, window, dilation):
    l, w, qb, kw, nb = _dil_geometry(s, window, dilation)
    c0 = H_DIFF + g * H_DIL
    shifts = []
    for n in (0, min(1, nb - 1), nb - 1):
        l0 = n * qb
        shifts.append(min(max(l0 - LANES, 0), l - kw) - l0)
    j = jnp.arange(kw)[None, :, None]
    i = jnp.arange(qb)[None, None, :]
    off = jnp.asarray(shifts)[:, None, None] + j - i
    b = _bias_of_rel(rel_bias[:, c0:c0 + H_DIL], off * dilation) * LOG2E
    b = jnp.where((jnp.abs(off) <= w)[None], b, NEG_INF)
    return jnp.stack([jnp.concatenate([b[2 * p], b[2 * p + 1]], axis=-1) for p in range(H_DIL // 2)], axis=1)


def _dil_attn_kernel(q_ref, k_ref, v_ref, bias_ref, o_ref, lse_ref, qt_ref, vt_ref, st_ref,
                     *, l, qb, kw, nb):
    chunk = min(512, l)
    for pair in range(H_DIL // 2):
        cs = slice(pair * LANES, (pair + 1) * LANES)
        for c in range(l // chunk):
            rs = slice(c * chunk, (c + 1) * chunk)
            qt_ref[pair, :, rs] = q_ref[0, 0, rs, cs].astype(F32).T.astype(BF16)
            vt_ref[pair, :LANES, rs] = v_ref[0, 0, rs, cs].astype(F32).T.astype(BF16)
        vt_ref[pair, LANES:, :] = jnp.ones((DEN_ROWS, l), BF16)
    first = lax.broadcasted_iota(jnp.int32, (LANES, qb), 0) < HEAD_DIM

    def window(n):
        l0 = pl.multiple_of(n * qb, qb)
        return l0, pl.multiple_of(jnp.clip(l0 - LANES, 0, l - kw), LANES)

    def scores(n, slot):
        l0, ws = window(n)
        for pair in range(H_DIL // 2):
            qt = qt_ref[pair, :, pl.ds(l0, qb)]
            q2 = jnp.concatenate([jnp.where(first, qt, jnp.zeros_like(qt)),
                                  jnp.where(first, jnp.zeros_like(qt), qt)], axis=1)
            kp = k_ref[0, 0, pl.ds(ws, kw), pair * LANES:(pair + 1) * LANES]
            st_ref[slot, pair] = jnp.dot(kp, q2, preferred_element_type=F32)

    def finish(n, slot):
        l0, ws = window(n)
        t = jnp.where(n == 0, 0, jnp.where(n == nb - 1, 2, 1))
        for pair in range(H_DIL // 2):
            cs = slice(pair * LANES, (pair + 1) * LANES)
            st = st_ref[slot, pair] + bias_ref[t, pair]
            mx = jnp.max(st, axis=0, keepdims=True)
            p = jnp.exp2(st - mx).astype(BF16)
            acc = jnp.dot(vt_ref[pair, :, pl.ds(ws, kw)], p, preferred_element_type=F32)
            den = acc[LANES:LANES + 1, :]
            ot = acc[:LANES, :] / den
            lse = mx * LN2 + jnp.log(den)
            o_t = jnp.where(first, ot[:, :qb], ot[:, qb:])
            lse_t = jnp.where(first, lse[:, :qb], lse[:, qb:])
            o_ref[0, 0, pl.ds(l0, qb), cs] = o_t.T.astype(BF16)
            lse_ref[0, 0, pl.ds(l0, qb), cs] = lse_t.T

    def two_blocks(i, carry):
        n = 2 * i
        scores(n + 1, 1)
        finish(n, 0)
        scores(n + 2, 0)
        finish(n + 1, 1)
        return carry

    scores(0, 0)
    lax.fori_loop(0, (nb - 1) // 2, two_blocks, 0, unroll=True)
    if nb % 2 == 0:
        scores(nb - 1, 1)
        finish(nb - 2, 0)
        finish(nb - 1, 1)
    else:
        finish(nb - 1, 0)


def _dil_attn(arr, col0, bias_tiles, g, window, dilation):
    b, d, l, _ = arr.shape
    _, w, qb, kw, nb = _dil_geometry(l * d, window, dilation)
    kern = functools.partial(_dil_attn_kernel, l=l, qb=qb, kw=kw, nb=nb)
    return pl.pallas_call(
        kern,
        grid=(b, d),
        in_specs=[
            pl.BlockSpec((1, 1, l, GRP_W), lambda bi, r: (bi, r, 0, col0)),
            pl.BlockSpec((1, 1, l, GRP_W), lambda bi, r: (bi, r, 0, col0 + 1)),
            pl.BlockSpec((1, 1, l, GRP_W), lambda bi, r: (bi, r, 0, col0 + 2)),
            _const_spec((3, H_DIL // 2, kw, 2 * qb)),
        ],
        out_specs=[
            pl.BlockSpec((1, 1, l, GRP_W), lambda bi, r: (bi, r, 0, 0)),
            pl.BlockSpec((1, 1, l, GRP_W), lambda bi, r: (bi, r, 0, 0)),
        ],
        out_shape=[
            jax.ShapeDtypeStruct((b, d, l, GRP_W), BF16),
            jax.ShapeDtypeStruct((b, d, l, GRP_W), F32),
        ],
        scratch_shapes=[
            pltpu.VMEM((H_DIL // 2, LANES, l), BF16),
            pltpu.VMEM((H_DIL // 2, LANES + DEN_ROWS, l), BF16),
            pltpu.VMEM((2, H_DIL // 2, kw, 2 * qb), F32),
        ],
        compiler_params=_params(("parallel", "parallel")),
        name=f"dil_attn_g{g}",
    )(arr, arr, arr, bias_tiles)


def _pack_words(a, b):
    ua = lax.bitcast_convert_type(a.astype(BF16).astype(F32), jnp.uint32)
    ub = lax.bitcast_convert_type(b.astype(BF16).astype(F32), jnp.uint32)
    return (ua >> 16) | (ub & jnp.uint32(0xFFFF0000))


def _unpack_words(w):
    return (lax.bitcast_convert_type(w << 16, F32),
            lax.bitcast_convert_type(w & jnp.uint32(0xFFFF0000), F32))


def _route(lg):
    lane = lax.broadcasted_iota(jnp.int32, lg.shape, 1)
    big = jnp.int32(ROUTE_LANES)
    gl = jnp.where(lane < N_GROUPS, lg, -jnp.inf)
    gmax = jnp.max(gl, axis=-1, keepdims=True)
    gi = jnp.min(jnp.where(gl == gmax, lane, big), axis=-1, keepdims=True)
    gw = 1.0 / jnp.sum(jnp.exp(gl - gmax), axis=-1, keepdims=True)
    e_lo = EXP_LANE0 + E_PER_GROUP * gi
    es = jnp.where((lane >= e_lo) & (lane < e_lo + E_PER_GROUP), lg, -jnp.inf)
    v1 = jnp.max(es, axis=-1, keepdims=True)
    i1 = jnp.min(jnp.where(es == v1, lane, big), axis=-1, keepdims=True)
    es2 = jnp.where(lane == i1, -jnp.inf, es)
    v2 = jnp.max(es2, axis=-1, keepdims=True)
    i2 = jnp.min(jnp.where(es2 == v2, lane, big), axis=-1, keepdims=True)
    t = jnp.exp(v2 - v1)
    w1 = gw / (1.0 + t)
    w2 = w1 * t
    return gi, jnp.where(lane == i1, w1, 0.0) + jnp.where(lane == i2, w2, 0.0)


def _merge_route_kernel(x_ref, oa_ref, o0_ref, l0_ref, o1_ref, l1_ref, o2_ref, l2_ref, gt_ref,
                        wa_ref, wb_ref, wo_ref, nf_ref, wr_ref, x1_ref, hw_ref, gid_ref,
                        so1_ref, sl1_ref, so2_ref, sl2_ref, *, tm):
    for src, dst, d in ((o1_ref, so1_ref, DIL_CONFIGS[1][1]), (l1_ref, sl1_ref, DIL_CONFIGS[1][1]),
                        (o2_ref, so2_ref, DIL_CONFIGS[2][1]), (l2_ref, sl2_ref, DIL_CONFIGS[2][1])):
        for r in range(d):
            for c in range(GRP_W // LANES):
                dst[c, pl.ds(r, tm // d, stride=d), :] = src[0, r, :, c * LANES:(c + 1) * LANES].astype(F32)
    subs = [slice(i * SUB_MERGE, (i + 1) * SUB_MERGE) for i in range(tm // SUB_MERGE)]
    unsplit = lambda ref, rs: jnp.concatenate([ref[c, rs, :] for c in range(GRP_W // LANES)], axis=1)

    def dil_mix(rs):
        l0, l1, l2 = l0_ref[0, 0, rs, :], unsplit(sl1_ref, rs), unsplit(sl2_ref, rs)
        mx = jnp.maximum(jnp.maximum(l0, l1), l2)
        e0, e1, e2 = jnp.exp(l0 - mx), jnp.exp(l1 - mx), jnp.exp(l2 - mx)
        ob = (e0 * o0_ref[0, 0, rs, :].astype(F32) + e1 * unsplit(so1_ref, rs)
              + e2 * unsplit(so2_ref, rs)) / (e0 + e1 + e2)
        return ob.astype(BF16)

    obs = [dil_mix(rs) for rs in subs]
    pas = [jnp.dot(oa_ref[0, rs, :], wa_ref[...], preferred_element_type=F32) for rs in subs]
    pbs = [jnp.dot(ob, wb_ref[...], preferred_element_type=F32) for ob in obs]

    def gate(rs, pa, pb):
        ga = jax.nn.sigmoid(gt_ref[0, rs, :D_MODEL].astype(F32))
        gb = jax.nn.sigmoid(gt_ref[0, rs, D_MODEL:].astype(F32))
        return (ga * pa + gb * pb).astype(BF16)

    mgs = [gate(rs, pa, pb) for rs, pa, pb in zip(subs, pas, pbs)]
    x1s = [x_ref[0, rs, :] + jnp.dot(mg, wo_ref[...], preferred_element_type=F32)
           for rs, mg in zip(subs, mgs)]

    def ffn_norm(rs, x1):
        x1_ref[0, rs, :] = x1
        ms = jnp.mean(x1 * x1, axis=-1, keepdims=True)
        h2f = x1 * lax.rsqrt(ms + EPS) * nf_ref[...]
        hw_ref[0, rs, :] = _pack_words(h2f[:, :D_MODEL // 2], h2f[:, D_MODEL // 2:])
        return h2f.astype(BF16)

    h2s = [ffn_norm(rs, x1) for rs, x1 in zip(subs, x1s)]
    lgs = [jnp.dot(h2, wr_ref[...], preferred_element_type=F32) for h2 in h2s]
    for rs, lg in zip(subs, lgs):
        gi, _ = _route(lg)
        git = jnp.broadcast_to(gi.astype(F32), lg.shape).T
        gid_ref[0, 0, :, rs] = git[:SUBLANES, :].astype(jnp.int32)


def _merge_route(x, oa, dil, gates, wa, wb, wo, nf, wr):
    b, s, _ = x.shape
    tm = TM_MERGE
    d1, d2 = DIL_CONFIGS[1][1], DIL_CONFIGS[2][1]
    tok = lambda bi, i: (bi, i, 0)
    cls = lambda bi, i: (bi, 0, i, 0)
    (o0, s0), (o1, s1), (o2, s2) = dil
    return pl.pallas_call(
        functools.partial(_merge_route_kernel, tm=tm),
        grid=(b, s // tm),
        in_specs=[
            pl.BlockSpec((1, tm, D_MODEL), tok),
            pl.BlockSpec((1, tm, DIFF_V), tok),
            pl.BlockSpec((1, 1, tm, GRP_W), cls), pl.BlockSpec((1, 1, tm, GRP_W), cls),
            pl.BlockSpec((1, d1, tm // d1, GRP_W), cls), pl.BlockSpec((1, d1, tm // d1, GRP_W), cls),
            pl.BlockSpec((1, d2, tm // d2, GRP_W), cls), pl.BlockSpec((1, d2, tm // d2, GRP_W), cls),
            pl.BlockSpec((1, tm, GATE_COLS), tok),
            _const_spec((DIFF_V, D_MODEL)),
            _const_spec((GRP_W, D_MODEL)),
            _const_spec((D_MODEL, D_MODEL)),
            _const_spec((1, D_MODEL)),
            _const_spec((D_MODEL, ROUTE_LANES)),
        ],
        out_specs=[
            pl.BlockSpec((1, tm, D_MODEL), tok),
            pl.BlockSpec((1, tm, D_MODEL // 2), tok),
            pl.BlockSpec((1, 1, SUBLANES, tm), lambda bi, i: (bi, i, 0, 0)),
        ],
        out_shape=[
            jax.ShapeDtypeStruct((b, s, D_MODEL), F32),
            jax.ShapeDtypeStruct((b, s, D_MODEL // 2), jnp.uint32),
            jax.ShapeDtypeStruct((b, s // tm, SUBLANES, tm), jnp.int32),
        ],
        scratch_shapes=[pltpu.VMEM((GRP_W // LANES, tm, LANES), F32) for _ in range(4)],
        compiler_params=_params(("parallel", "parallel")),
        name="merge_route",
    )(x, oa, o0, s0, o1, s1, o2, s2, gates, wa, wb, wo, nf, wr)


def _dispatch_plan_kernel(gid_ref, dest_ref, tgrp_ref, *, rows, cols):
    gid = gid_ref[...]
    ci = lax.broadcasted_iota(jnp.int32, (cols, cols), 0)
    cj = lax.broadcasted_iota(jnp.int32, (cols, cols), 1)
    upto = (ci <= cj).astype(BF16)
    ri = lax.broadcasted_iota(jnp.int32, (rows, rows), 0)
    rj = lax.broadcasted_iota(jnp.int32, (rows, rows), 1)
    above = (rj < ri).astype(F32)
    seg = jnp.zeros((1, 1), F32)
    dest = jnp.zeros((rows, cols), F32)
    ends = []
    for g in range(N_GROUPS):
        mem = (gid == g).astype(F32)
        pref = jnp.dot(mem.astype(BF16), upto, preferred_element_type=F32)
        tot = pref[:, cols - 1:cols]
        before = jnp.dot(above, jnp.broadcast_to(tot, (rows, LANES)), preferred_element_type=F32,
                         precision=lax.Precision.HIGHEST)[:, :1]
        dest = dest + mem * (seg + before + pref - 1.0)
        cnt = jnp.sum(tot, axis=0, keepdims=True)
        seg = seg + jnp.ceil(cnt / TM_MOE) * TM_MOE
        ends.append(seg)
    dest_ref[...] = dest.astype(jnp.int32)
    tile = (lax.broadcasted_iota(jnp.int32, (SUBLANES, LANES), 0) * LANES
            + lax.broadcasted_iota(jnp.int32, (SUBLANES, LANES), 1))
    start = (tile * TM_MOE).astype(F32)
    tgrp = jnp.zeros((SUBLANES, LANES), jnp.int32)
    for g in range(N_GROUPS - 1):
        tgrp = tgrp + (start >= ends[g]).astype(jnp.int32)
    tgrp_ref[...] = tgrp


def _dispatch_plan(gid):
    rows, cols = gid.shape
    assert rows % SUBLANES == 0 and rows * cols // TM_MOE + N_GROUPS <= SUBLANES * LANES
    return pl.pallas_call(
        functools.partial(_dispatch_plan_kernel, rows=rows, cols=cols),
        out_shape=[jax.ShapeDtypeStruct((rows, cols), jnp.int32),
                   jax.ShapeDtypeStruct((SUBLANES, LANES), jnp.int32)],
        compiler_params=pltpu.CompilerParams(vmem_limit_bytes=VMEM_LIMIT),
        name="dispatch_plan",
    )(gid)


def _moe_group_kernel(tgrp_ref, hw_ref, wr_ref, wg_ref, wu_ref, wd_ref, ow_ref, he_ref):
    g = tgrp_ref[pl.program_id(0)]
    lo, hi = _unpack_words(hw_ref[...])
    h2 = jnp.concatenate([lo, hi], axis=1).astype(BF16)
    _, dw = _route(jnp.dot(h2, wr_ref[...], preferred_element_type=F32))
    lane = lax.broadcasted_iota(jnp.int32, dw.shape, 1)
    for e in range(E_PER_GROUP):
        we = jnp.sum(jnp.where(lane == EXP_LANE0 + E_PER_GROUP * g + e, dw, 0.0), axis=-1, keepdims=True)
        gp = jnp.dot(h2, wg_ref[0, e], preferred_element_type=F32)
        up = jnp.dot(h2, wu_ref[0, e], preferred_element_type=F32)
        he_ref[:, e * D_EXPERT:(e + 1) * D_EXPERT] = (jax.nn.silu(gp) * up * we).astype(BF16)
    moe = jnp.dot(he_ref[...], wd_ref[0], preferred_element_type=F32)
    ow_ref[...] = _pack_words(moe[:, :D_MODEL // 2], moe[:, D_MODEL // 2:])


def _moe_groups(tgrp, hw_sorted, wr, wg, wu, wd):
    m = hw_sorted.shape[0]
    tm = TM_MOE
    row = lambda i, t: (i, 0)
    grp = lambda i, t: (t[i], 0, 0, 0)
    return pl.pallas_call(
        _moe_group_kernel,
        grid_spec=pltpu.PrefetchScalarGridSpec(
            num_scalar_prefetch=1,
            grid=(m // tm,),
            in_specs=[
                pl.BlockSpec((tm, D_MODEL // 2), row),
                pl.BlockSpec((D_MODEL, ROUTE_LANES), lambda i, t: (0, 0)),
                pl.BlockSpec((1, E_PER_GROUP, D_MODEL, D_EXPERT), grp),
                pl.BlockSpec((1, E_PER_GROUP, D_MODEL, D_EXPERT), grp),
                pl.BlockSpec((1, E_PER_GROUP * D_EXPERT, D_MODEL), lambda i, t: (t[i], 0, 0)),
            ],
            out_specs=pl.BlockSpec((tm, D_MODEL // 2), row),
            scratch_shapes=[pltpu.VMEM((tm, E_PER_GROUP * D_EXPERT), BF16)],
        ),
        out_shape=jax.ShapeDtypeStruct((m, D_MODEL // 2), jnp.uint32),
        compiler_params=_params(("arbitrary",)),
        name="moe_groups",
    )(tgrp, hw_sorted, wr, wg, wu, wd)


def _final_norm_kernel(x1_ref, yw_ref, nf_ref, y_ref):
    lo, hi = _unpack_words(yw_ref[...])
    x2 = x1_ref[...] + jnp.concatenate([lo, hi], axis=1)
    ms = jnp.mean(x2 * x2, axis=-1, keepdims=True)
    y_ref[...] = x2 * lax.rsqrt(ms + EPS) * nf_ref[...]


def _final_norm(x1, yw, nf):
    n = x1.shape[0]
    tm = TM_MOE
    row = lambda i: (i, 0)
    return pl.pallas_call(
        _final_norm_kernel,
        grid=(n // tm,),
        in_specs=[pl.BlockSpec((tm, D_MODEL), row), pl.BlockSpec((tm, D_MODEL // 2), row),
                  _const_spec((1, D_MODEL))],
        out_specs=pl.BlockSpec((tm, D_MODEL), row),
        out_shape=jax.ShapeDtypeStruct((n, D_MODEL), F32),
        compiler_params=_params(("parallel",)),
        name="final_norm",
    )(x1, yw, nf)


def _sc_mesh():
    return plsc.VectorSubcoreMesh(core_axis_name="core", subcore_axis_name="subcore")


def _sc_scatter_rows(x, idx, m):
    n, d = x.shape

    @pl.kernel(out_type=jax.ShapeDtypeStruct((m, d), x.dtype), mesh=_sc_mesh(), scratch_types=[])
    def scatter(x_hbm, i_hbm, o_hbm):
        for c in range(d // SC_COLS):
            cols = pl.ds(c * SC_COLS, SC_COLS)
            o_cols = o_hbm.at[:, cols]

            def body(x_vmem, i_vmem, o_cols=o_cols):
                pltpu.sync_copy(x_vmem, o_cols.at[i_vmem.at[0]])

            pltpu.emit_pipeline(
                body,
                grid=(n // SC_WINDOW,),
                in_specs=[pl.BlockSpec((SC_WINDOW, SC_COLS), lambda i, c=c: (i, c)),
                          pl.BlockSpec((1, SC_WINDOW), lambda i: (0, i))],
                out_specs=[],
                core_axis_name=("core", "subcore"),
                dimension_semantics=(pltpu.PARALLEL,),
            )(x_hbm, i_hbm)

    return scatter(x, idx.reshape(1, n))


def _sc_gather_rows(x, idx):
    n = idx.shape[0]
    d = x.shape[1]

    @pl.kernel(out_type=jax.ShapeDtypeStruct((n, d), x.dtype), mesh=_sc_mesh(), scratch_types=[])
    def gather(x_hbm, i_hbm, o_hbm):
        for c in range(d // SC_COLS):
            x_cols = x_hbm.at[:, pl.ds(c * SC_COLS, SC_COLS)]

            def body(i_vmem, o_vmem, x_cols=x_cols):
                pltpu.sync_copy(x_cols.at[i_vmem.at[0]], o_vmem)

            pltpu.emit_pipeline(
                body,
                grid=(n // SC_WINDOW,),
                in_specs=[pl.BlockSpec((1, SC_WINDOW), lambda i: (0, i))],
                out_specs=[pl.BlockSpec((SC_WINDOW, SC_COLS), lambda i, c=c: (i, c))],
                core_axis_name=("core", "subcore"),
                dimension_semantics=(pltpu.PARALLEL,),
            )(i_hbm, o_hbm)

    return gather(x, idx.reshape(1, n))


def _prepare_weights(rel_bias, norm_mix, w_in, lam_q1, lam_k1, lam_q2, lam_k2, subln_g,
                     w_branch_a, w_branch_b, w_out, norm_ffn, w_group, w_router,
                     w_exp_gate, w_exp_up, w_exp_down, norm_final):
    w = w_in[0]
    c = QK_SCALE * LOG2E
    o = 0
    qa, o = w[:, o:o + DIFF_QK] * c, o + DIFF_QK
    ka, o = w[:, o:o + DIFF_QK], o + DIFF_QK
    va, o = w[:, o:o + DIFF_V], o + DIFF_V
    qb, o = w[:, o:o + DIL_W] * c, o + DIL_W
    kb, o = w[:, o:o + DIL_W], o + DIL_W
    vb, o = w[:, o:o + DIL_W], o + DIL_W
    w_gate = w[:, o:].astype(BF16)
    grp = lambda g: [t[:, g * GRP_W:(g + 1) * GRP_W] for t in (qb, kb, vb)]
    w_nat = jnp.concatenate([qa, ka, va] + grp(0), axis=1).astype(BF16)
    w_d1 = jnp.concatenate(grp(1), axis=1).astype(BF16)
    w_d2 = jnp.concatenate(grp(2), axis=1).astype(BF16)
    wr = jnp.concatenate([w_group[0], w_router[0].transpose(1, 0, 2).reshape(D_MODEL, N_EXPERTS)], axis=1)
    wr = jnp.pad(wr, ((0, 0), (0, ROUTE_LANES - wr.shape[1]))).astype(BF16)
    return dict(
        rel_bias=rel_bias.astype(F32),
        norm_mix=norm_mix[0][None].astype(F32),
        w_nat=w_nat, w_gate=w_gate, w_d1=w_d1, w_d2=w_d2,
        lam=jnp.stack([lam_q1[0], lam_k1[0], lam_q2[0], lam_k2[0]]).astype(F32),
        subln_g=subln_g[0][None].astype(F32),
        wa=w_branch_a[0].astype(BF16), wb=w_branch_b[0].astype(BF16), wo=w_out[0].astype(BF16),
        norm_ffn=norm_ffn[0][None].astype(F32), wr=wr,
        wg=w_exp_gate[0].reshape(N_GROUPS, E_PER_GROUP, D_MODEL, D_EXPERT).astype(BF16),
        wu=w_exp_up[0].reshape(N_GROUPS, E_PER_GROUP, D_MODEL, D_EXPERT).astype(BF16),
        wd=w_exp_down[0].reshape(N_GROUPS, E_PER_GROUP * D_EXPERT, D_MODEL).astype(BF16),
        norm_final=norm_final[None].astype(F32),
    )


def _encoder(x, p):
    b, s, d = x.shape
    n = b * s
    nat, gates, dil1, dil2 = _norm_proj(x, p["norm_mix"], p["w_nat"], p["w_gate"], p["w_d1"], p["w_d2"])
    tiles, far = _diff_bias_tiles(p["rel_bias"], min(TQ_DIFF, s), min(TK_DIFF, s))
    oa = _diff_attn(nat, tiles, far, p["lam"], p["subln_g"])
    srcs = ((nat.reshape(b, 1, s, NAT_COLS), (2 * DIFF_QK + DIFF_V) // GRP_W), (dil1, 0), (dil2, 0))
    dil = []
    for g, (win, dilation) in enumerate(DIL_CONFIGS):
        tiles = _dil_bias_tiles(p["rel_bias"], g, s, win, dilation)
        dil.append(_dil_attn(srcs[g][0], srcs[g][1], tiles, g, win, dilation))
    x1, hw, gid = _merge_route(x, oa, dil, gates, p["wa"], p["wb"], p["wo"], p["norm_ffn"], p["wr"])
    dest, tgrp = _dispatch_plan(gid[:, :, 0, :].reshape(n // TM_MERGE, TM_MERGE))
    dest = dest.reshape(n)
    n_tiles = n // TM_MOE + N_GROUPS
    hw_sorted = _sc_scatter_rows(hw.reshape(n, d // 2), dest, n_tiles * TM_MOE)
    yw_sorted = _moe_groups(tgrp.reshape(-1)[:n_tiles], hw_sorted, p["wr"], p["wg"], p["wu"], p["wd"])
    y = _final_norm(x1.reshape(n, d), _sc_gather_rows(yw_sorted, dest), p["norm_final"])
    return y.reshape(b, s, d)


def kernel(x_prompt, x_sample, rel_bias, norm_mix, w_in, lam_q1, lam_k1, lam_q2, lam_k2, subln_g,
           w_branch_a, w_branch_b, w_out, norm_ffn, w_group, w_router, w_exp_gate, w_exp_up,
           w_exp_down, norm_final):
    p = _prepare_weights(rel_bias, norm_mix, w_in, lam_q1, lam_k1, lam_q2, lam_k2, subln_g,
                         w_branch_a, w_branch_b, w_out, norm_ffn, w_group, w_router,
                         w_exp_gate, w_exp_up, w_exp_down, norm_final)
    return (_encoder(x_prompt, p), _encoder(x_sample, p))
```

```python
import functools
import math

import jax
import jax.numpy as jnp
from jax import lax
from jax.experimental import pallas as pl
from jax.experimental.pallas import tpu as pltpu
from jax.experimental.pallas import tpu_sc as plsc

F32 = jnp.float32
BF16 = jnp.bfloat16

D_MODEL = 1024
HEAD_DIM = 64
H_DIFF = 4
H_DIL = 4
DIL_CONFIGS = ((128, 1), (512, 4), (2048, 16))
N_DIL = len(DIL_CONFIGS)
DIFF_QK = H_DIFF * 2 * HEAD_DIM
DIFF_V = H_DIFF * 2 * HEAD_DIM
DIL_W = N_DIL * H_DIL * HEAD_DIM
GRP_W = H_DIL * HEAD_DIM
GRP_COLS = 3 * GRP_W
NAT_COLS = 2 * DIFF_QK + DIFF_V + GRP_COLS
GATE_COLS = 2 * D_MODEL
Q_BLOCK = 128
N_BUCKETS = 32
MAX_DISTANCE = 128
N_GROUPS = 4
E_PER_GROUP = 4
N_EXPERTS = N_GROUPS * E_PER_GROUP
D_EXPERT = 256
EPS = 1e-6
NEG_INF = -1e30
LAM_INIT = 0.8 - 0.6 * math.exp(-0.3 * 0)
LOG2E = math.log2(math.e)
LN2 = math.log(2.0)
QK_SCALE = HEAD_DIM ** -0.5

LANES = 128
SUBLANES = 8
VMEM_LIMIT = 56 * 1024 * 1024

TM_PROJ = 512
TQ_DIFF = 1024
TK_DIFF = 256
DIFF_SLOTS = 4
DEN_ROWS = 16
DIL_QB = 256
DIL_STEP_ROWS = 1024
SC_WINDOW = 128
SC_COLS = 256
TM_MERGE = 512
SUB_MERGE = 256
TM_MOE = 512
ROUTE_LANES = 128
EXP_LANE0 = N_GROUPS


def _const_spec(shape):
    nd = len(shape)
    return pl.BlockSpec(shape, lambda *_: (0,) * nd, pipeline_mode=pl.Buffered(1))


def _params(sem):
    return pltpu.CompilerParams(dimension_semantics=sem, vmem_limit_bytes=VMEM_LIMIT)


def _rel_bucket(rel):
    half = N_BUCKETS // 2
    max_exact = half // 2
    ret = (rel > 0).astype(jnp.int32) * half
    n = jnp.abs(rel)
    nf = jnp.maximum(n, 1).astype(F32)
    large = max_exact + (jnp.log(nf / max_exact) / math.log(MAX_DISTANCE / max_exact)
                         * (half - max_exact)).astype(jnp.int32)
    large = jnp.minimum(large, half - 1)
    return ret + jnp.where(n < max_exact, n, large)


def _bias_of_rel(table, rel):
    bucket = _rel_bucket(rel)[None]
    t = table.astype(F32)
    col = (t.shape[1],) + (1,) * rel.ndim
    out = jnp.zeros((t.shape[1],) + rel.shape, F32)
    for k in range(N_BUCKETS):
        out = jnp.where(bucket == k, t[k].reshape(col), out)
    return out


def _chunks(total, width):
    out, c = [], 0
    while c < total:
        w = min(width, total - c)
        out.append((c, w))
        c += w
    return out


def _norm_proj_kernel(x_ref, g_ref, wn_ref, wg_ref, w1_ref, w2_ref,
                      on_ref, og_ref, o1_ref, o2_ref, hs_ref, *, tm):
    x = x_ref[0]
    ms = jnp.mean(x * x, axis=-1, keepdims=True)
    hf = x * lax.rsqrt(ms + EPS) * g_ref[...]
    for c in range(D_MODEL // LANES):
        hs_ref[c] = hf[:, c * LANES:(c + 1) * LANES]
    h = hf.astype(BF16)
    for c0, cw in _chunks(NAT_COLS, 512):
        on_ref[0, :, c0:c0 + cw] = jnp.dot(h, wn_ref[:, c0:c0 + cw],
                                           preferred_element_type=F32).astype(BF16)
    for c0, cw in _chunks(GATE_COLS, 512):
        og_ref[0, :, c0:c0 + cw] = jnp.dot(h, wg_ref[:, c0:c0 + cw],
                                           preferred_element_type=F32).astype(BF16)
    for w_ref, o_ref, d in ((w1_ref, o1_ref, DIL_CONFIGS[1][1]), (w2_ref, o2_ref, DIL_CONFIGS[2][1])):
        rows = tm // d
        hp = jnp.concatenate(
            [jnp.concatenate([hs_ref[c, pl.ds(r, rows, stride=d), :] for r in range(d)], axis=0)
             for c in range(D_MODEL // LANES)], axis=1).astype(BF16)
        for c0, cw in _chunks(GRP_COLS, 256):
            res = jnp.dot(hp, w_ref[:, c0:c0 + cw], preferred_element_type=F32).astype(BF16)
            o_ref[0, :, :, c0:c0 + cw] = res.reshape(d, rows, cw)


def _norm_proj(x, g, w_nat, w_gate, w_d1, w_d2):
    b, s, _ = x.shape
    tm = TM_PROJ
    d1, d2 = DIL_CONFIGS[1][1], DIL_CONFIGS[2][1]
    return pl.pallas_call(
        functools.partial(_norm_proj_kernel, tm=tm),
        grid=(b, s // tm),
        in_specs=[
            pl.BlockSpec((1, tm, D_MODEL), lambda bi, i: (bi, i, 0)),
            _const_spec((1, D_MODEL)),
            _const_spec((D_MODEL, NAT_COLS)),
            _const_spec((D_MODEL, GATE_COLS)),
            _const_spec((D_MODEL, GRP_COLS)),
            _const_spec((D_MODEL, GRP_COLS)),
        ],
        out_specs=[
            pl.BlockSpec((1, tm, NAT_COLS), lambda bi, i: (bi, i, 0)),
            pl.BlockSpec((1, tm, GATE_COLS), lambda bi, i: (bi, i, 0)),
            pl.BlockSpec((1, d1, tm // d1, GRP_COLS), lambda bi, i: (bi, 0, i, 0)),
            pl.BlockSpec((1, d2, tm // d2, GRP_COLS), lambda bi, i: (bi, 0, i, 0)),
        ],
        out_shape=[
            jax.ShapeDtypeStruct((b, s, NAT_COLS), BF16),
            jax.ShapeDtypeStruct((b, s, GATE_COLS), BF16),
            jax.ShapeDtypeStruct((b, d1, s // d1, GRP_COLS), BF16),
            jax.ShapeDtypeStruct((b, d2, s // d2, GRP_COLS), BF16),
        ],
        scratch_shapes=[pltpu.VMEM((D_MODEL // LANES, tm, LANES), F32)],
        compiler_params=_params(("parallel", "parallel")),
        name="norm_proj",
    )(x, g, w_nat, w_gate, w_d1, w_d2)


def _diff_bias_tiles(rel_bias, tq, tk):
    ratio = tq // tk
    t = jnp.arange(-2, ratio + 2)[:, None, None]
    j = jnp.arange(tk)[None, :, None]
    i = jnp.arange(tq)[None, None, :]
    tiles = _bias_of_rel(rel_bias[:, :H_DIFF], t * tk + j - i) * LOG2E
    far = _bias_of_rel(rel_bias[:, :H_DIFF], jnp.asarray([-MAX_DISTANCE, MAX_DISTANCE])) * LOG2E
    return tiles, far


def _diff_attn_kernel(lam_ref, far_ref, q_ref, k_ref, v_ref, bias_ref, g_ref, o_ref,
                      vt_ref, st_ref, p_ref, acc_ref, *, tq, tk, nkb):
    h = pl.program_id(1)
    qi = pl.program_id(2)
    ratio = tq // tk
    n_band = ratio + 2
    vchunk = min(512, nkb * tk)

    @pl.when(qi == 0)
    def _():
        for c in range(nkb * tk // vchunk):
            cs = slice(c * vchunk, (c + 1) * vchunk)
            vt_ref[:LANES, cs] = v_ref[0, cs, :].astype(F32).T.astype(BF16)
        vt_ref[LANES:, :] = jnp.ones((DEN_ROWS, nkb * tk), BF16)

    qt = q_ref[0].astype(F32).T
    first = lax.broadcasted_iota(jnp.int32, (LANES, tq), 0) < HEAD_DIM
    qts = (jnp.where(first, qt, 0.0).astype(BF16), jnp.where(first, 0.0, qt).astype(BF16))
    kb0 = qi * ratio

    def key_block(j):
        kb = kb0 - 1 + j
        return jnp.where(kb < 0, kb + nkb, jnp.where(kb >= nkb, kb - nkb, kb))

    def scores(j, slot, band):
        kb = key_block(j)
        ks = pl.multiple_of(kb * tk, tk)
        k = k_ref[0, pl.ds(ks, tk), :]
        if band:
            bt = bias_ref[0, jnp.clip(kb - kb0, -2, ratio + 1) + 2]
        mx = []
        for mp in range(2):
            st = jnp.dot(k, qts[mp], preferred_element_type=F32)
            if band:
                st = st + bt
            st_ref[slot, mp] = st
            mx.append(jnp.max(st, axis=0, keepdims=True))
        return tuple(mx)

    def softmax(j, slot, stats, mx, band):
        if not band:
            kb = key_block(j)
            far_c = jnp.where(kb < kb0, far_ref[h, 0], far_ref[h, 1])
        out, alphas = [], []
        for mp in range(2):
            m = stats[mp]
            if band:
                m_new = jnp.maximum(m, mx[mp])
                shift = m_new
            else:
                m_new = jnp.maximum(m, mx[mp] + far_c)
                shift = m_new - far_c
            p_ref[slot, mp] = jnp.exp2((st_ref[slot, mp] - shift).astype(BF16))
            out.append(m_new)
            alphas.append(jnp.exp2(m - m_new))
        return tuple(out), tuple(alphas)

    def values(j, slot, alphas):
        ks = pl.multiple_of(key_block(j) * tk, tk)
        vt = vt_ref[:, pl.ds(ks, tk)]
        for mp in range(2):
            acc_ref[mp] = alphas[mp] * acc_ref[mp] + jnp.dot(vt, p_ref[slot, mp],
                                                             preferred_element_type=F32)

    acc_ref[...] = jnp.zeros_like(acc_ref)
    mx = scores(0, 0, True)
    stats = (jnp.full((1, tq), -jnp.inf, F32),) * 2
    alphas = None
    for j in range(nkb):
        if j + 1 < nkb:
            mx_next = scores(j + 1, (j + 1) % DIFF_SLOTS, j + 1 < n_band)
        if j >= 1:
            values(j - 1, (j - 1) % DIFF_SLOTS, alphas)
        stats, alphas = softmax(j, j % DIFF_SLOTS, stats, mx, j < n_band)
        mx = mx_next
    values(nkb - 1, (nkb - 1) % DIFF_SLOTS, alphas)
    l0, l1 = acc_ref[0, LANES:LANES + 1, :], acc_ref[1, LANES:LANES + 1, :]

    lp = lam_ref[...]
    lam = (jnp.exp(jnp.sum(lp[0:1] * lp[1:2], axis=-1, keepdims=True))
           - jnp.exp(jnp.sum(lp[2:3] * lp[3:4], axis=-1, keepdims=True)) + LAM_INIT)
    ot = acc_ref[0, :LANES, :] / l0 - lam * (acc_ref[1, :LANES, :] / l1)
    ms = jnp.mean(ot * ot, axis=0, keepdims=True)
    y = (ot * lax.rsqrt(ms + EPS)).T * g_ref[...] * (1.0 - LAM_INIT)
    o_ref[0] = y.astype(BF16)


def _diff_attn(nat, bias_tiles, far, lam_params, subln_g):
    b, s, _ = nat.shape
    tq, tk = min(TQ_DIFF, s), min(TK_DIFF, s)
    nt = bias_tiles.shape[1]
    ratio, nkb = tq // tk, s // tk
    assert tq % tk == 0 and tk >= MAX_DISTANCE and nkb >= ratio + 2
    assert DIFF_SLOTS >= 3
    kern = functools.partial(_diff_attn_kernel, tq=tq, tk=tk, nkb=s // tk)
    return pl.pallas_call(
        kern,
        grid=(b, H_DIFF, s // tq),
        in_specs=[
            _const_spec((4, HEAD_DIM)),
            pl.BlockSpec(memory_space=pltpu.SMEM),
            pl.BlockSpec((1, tq, LANES), lambda bi, h, qi: (bi, qi, h)),
            pl.BlockSpec((1, s, LANES), lambda bi, h, qi: (bi, 0, H_DIFF + h)),
            pl.BlockSpec((1, s, LANES), lambda bi, h, qi: (bi, 0, 2 * H_DIFF + h)),
            pl.BlockSpec((1, nt, tk, tq), lambda bi, h, qi: (h, 0, 0, 0)),
            _const_spec((1, 2 * HEAD_DIM)),
        ],
        out_specs=pl.BlockSpec((1, tq, LANES), lambda bi, h, qi: (bi, qi, h)),
        out_shape=jax.ShapeDtypeStruct((b, s, DIFF_V), BF16),
        scratch_shapes=[
            pltpu.VMEM((LANES + DEN_ROWS, s), BF16),
            pltpu.VMEM((DIFF_SLOTS, 2, tk, tq), F32),
            pltpu.VMEM((DIFF_SLOTS, 2, tk, tq), BF16),
            pltpu.VMEM((2, LANES + DEN_ROWS, tq), F32),
        ],
        compiler_params=_params(("parallel", "parallel", "arbitrary")),
        name="diff_attn",
    )(lam_params, far, nat, nat, nat, bias_tiles, subln_g)


def _dil_geometry(s, window, dilation):
    l = s // dilation
    w = window // (2 * dilation)
    qb = min(DIL_QB, l)
    kw = min(qb + 2 * LANES, l)
    assert w <= LANES and l % qb == 0 and l % LANES == 0
    return l, w, qb, kw, l // qb


def _dil_bias_tiles(rel_bias, g, s, window, dilation):
    l, w, qb, kw, nb = _dil_geometry(s, window, dilation)
    c0 = H_DIFF + g * H_DIL
    shifts = []
    for n in (0, min(1, nb - 1), nb - 1):
        l0 = n * qb
        shifts.append(min(max(l0 - LANES, 0), l - kw) - l0)
    j = jnp.arange(kw)[None, :, None]
    i = jnp.arange(qb)[None, None, :]
    off = jnp.asarray(shifts)[:, None, None] + j - i
    b = _bias_of_rel(rel_bias[:, c0:c0 + H_DIL], off * dilation) * LOG2E
    b = jnp.where((jnp.abs(off) <= w)[None], b, NEG_INF)
    return jnp.stack([jnp.concatenate([b[2 * p], b[2 * p + 1]], axis=-1) for p in range(H_DIL // 2)], axis=1)


def _dil_attn_kernel(q_ref, k_ref, v_ref, bias_ref, o_ref, lse_ref, qt_ref, vt_ref, st_ref,
                     *, l, qb, kw, nb, cps):
    chunk = min(512, l)
    first = lax.broadcasted_iota(jnp.int32, (LANES, qb), 0) < HEAD_DIM

    def transposes(c):
        for pair in range(H_DIL // 2):
            cs = slice(pair * LANES, (pair + 1) * LANES)
            for i in range(l // chunk):
                rs = slice(i * chunk, (i + 1) * chunk)
                qt_ref[c, pair, :, rs] = q_ref[0, c, rs, cs].astype(F32).T.astype(BF16)
                vt_ref[c, pair, :LANES, rs] = v_ref[0, c, rs, cs].astype(F32).T.astype(BF16)
            vt_ref[c, pair, LANES:, :] = jnp.ones((DEN_ROWS, l), BF16)

    def window(n):
        l0 = n * qb
        return l0, min(max(l0 - LANES, 0), l - kw)

    def scores(c, n, slot):
        l0, ws = window(n)
        for pair in range(H_DIL // 2):
            qt = qt_ref[c, pair, :, l0:l0 + qb]
            q2 = jnp.concatenate([jnp.where(first, qt, jnp.zeros_like(qt)),
                                  jnp.where(first, jnp.zeros_like(qt), qt)], axis=1)
            kp = k_ref[0, c, ws:ws + kw, pair * LANES:(pair + 1) * LANES]
            st_ref[c, slot, pair] = jnp.dot(kp, q2, preferred_element_type=F32)

    def finish(c, n, slot):
        l0, ws = window(n)
        t = 0 if n == 0 else (2 if n == nb - 1 else 1)
        for pair in range(H_DIL // 2):
            cs = slice(pair * LANES, (pair + 1) * LANES)
            st = st_ref[c, slot, pair] + bias_ref[t, pair]
            mx = jnp.max(st, axis=0, keepdims=True)
            p = jnp.exp2(st - mx).astype(BF16)
            acc = jnp.dot(vt_ref[c, pair, :, ws:ws + kw], p, preferred_element_type=F32)
            den = acc[LANES:LANES + 1, :]
            ot = acc[:LANES, :] / den
            lse = mx * LN2 + jnp.log(den)
            o_t = jnp.where(first, ot[:, :qb], ot[:, qb:])
            lse_t = jnp.where(first, lse[:, :qb], lse[:, qb:])
            o_ref[0, c, l0:l0 + qb, cs] = o_t.T.astype(BF16)
            lse_ref[0, c, l0:l0 + qb, cs] = lse_t.T

    for c in range(cps):
        transposes(c)
    for c in range(cps):
        scores(c, 0, 0)
    for c in range(cps):
        for n in range(nb):
            if n + 1 < nb:
                scores(c, n + 1, (n + 1) % 2)
            finish(c, n, n % 2)


def _dil_attn(arr, col0, bias_tiles, g, window, dilation):
    b, d, l, _ = arr.shape
    _, w, qb, kw, nb = _dil_geometry(l * d, window, dilation)
    cps = max(1, min(d, DIL_STEP_ROWS // l))
    kern = functools.partial(_dil_attn_kernel, l=l, qb=qb, kw=kw, nb=nb, cps=cps)
    return pl.pallas_call(
        kern,
        grid=(b, d // cps),
        in_specs=[
            pl.BlockSpec((1, cps, l, GRP_W), lambda bi, r: (bi, r, 0, col0)),
            pl.BlockSpec((1, cps, l, GRP_W), lambda bi, r: (bi, r, 0, col0 + 1)),
            pl.BlockSpec((1, cps, l, GRP_W), lambda bi, r: (bi, r, 0, col0 + 2)),
            _const_spec((3, H_DIL // 2, kw, 2 * qb)),
        ],
        out_specs=[
            pl.BlockSpec((1, cps, l, GRP_W), lambda bi, r: (bi, r, 0, 0)),
            pl.BlockSpec((1, cps, l, GRP_W), lambda bi, r: (bi, r, 0, 0)),
        ],
        out_shape=[
            jax.ShapeDtypeStruct((b, d, l, GRP_W), BF16),
            jax.ShapeDtypeStruct((b, d, l, GRP_W), F32),
        ],
        scratch_shapes=[
            pltpu.VMEM((cps, H_DIL // 2, LANES, l), BF16),
            pltpu.VMEM((cps, H_DIL // 2, LANES + DEN_ROWS, l), BF16),
            pltpu.VMEM((cps, 2, H_DIL // 2, kw, 2 * qb), F32),
        ],
        compiler_params=_params(("parallel", "parallel")),
        name=f"dil_attn_g{g}",
    )(arr, arr, arr, bias_tiles)


def _pack_words(a, b):
    ua = lax.bitcast_convert_type(a.astype(BF16).astype(F32), jnp.uint32)
    ub = lax.bitcast_convert_type(b.astype(BF16).astype(F32), jnp.uint32)
    return (ua >> 16) | (ub & jnp.uint32(0xFFFF0000))


def _unpack_words(w):
    return (lax.bitcast_convert_type(w << 16, F32),
            lax.bitcast_convert_type(w & jnp.uint32(0xFFFF0000), F32))


def _route(lg):
    lane = lax.broadcasted_iota(jnp.int32, lg.shape, 1)
    big = jnp.int32(ROUTE_LANES)
    gl = jnp.where(lane < N_GROUPS, lg, -jnp.inf)
    gmax = jnp.max(gl, axis=-1, keepdims=True)
    gi = jnp.min(jnp.where(gl == gmax, lane, big), axis=-1, keepdims=True)
    gw = 1.0 / jnp.sum(jnp.exp(gl - gmax), axis=-1, keepdims=True)
    e_lo = EXP_LANE0 + E_PER_GROUP * gi
    es = jnp.where((lane >= e_lo) & (lane < e_lo + E_PER_GROUP), lg, -jnp.inf)
    v1 = jnp.max(es, axis=-1, keepdims=True)
    i1 = jnp.min(jnp.where(es == v1, lane, big), axis=-1, keepdims=True)
    es2 = jnp.where(lane == i1, -jnp.inf, es)
    v2 = jnp.max(es2, axis=-1, keepdims=True)
    i2 = jnp.min(jnp.where(es2 == v2, lane, big), axis=-1, keepdims=True)
    t = jnp.exp(v2 - v1)
    w1 = gw / (1.0 + t)
    w2 = w1 * t
    return gi, jnp.where(lane == i1, w1, 0.0) + jnp.where(lane == i2, w2, 0.0)


def _merge_route_kernel(x_ref, oa_ref, o0_ref, l0_ref, o1_ref, l1_ref, o2_ref, l2_ref, gt_ref,
                        wa_ref, wb_ref, wo_ref, nf_ref, wr_ref, x1_ref, hw_ref, gid_ref,
                        so1_ref, sl1_ref, so2_ref, sl2_ref, *, tm):
    for src, dst, d in ((o1_ref, so1_ref, DIL_CONFIGS[1][1]), (l1_ref, sl1_ref, DIL_CONFIGS[1][1]),
                        (o2_ref, so2_ref, DIL_CONFIGS[2][1]), (l2_ref, sl2_ref, DIL_CONFIGS[2][1])):
        for r in range(d):
            for c in range(GRP_W // LANES):
                dst[c, pl.ds(r, tm // d, stride=d), :] = src[0, r, :, c * LANES:(c + 1) * LANES].astype(F32)
    subs = [slice(i * SUB_MERGE, (i + 1) * SUB_MERGE) for i in range(tm // SUB_MERGE)]
    unsplit = lambda ref, rs: jnp.concatenate([ref[c, rs, :] for c in range(GRP_W // LANES)], axis=1)

    def dil_mix(rs):
        l0, l1, l2 = l0_ref[0, 0, rs, :], unsplit(sl1_ref, rs), unsplit(sl2_ref, rs)
        mx = jnp.maximum(jnp.maximum(l0, l1), l2)
        e0, e1, e2 = jnp.exp(l0 - mx), jnp.exp(l1 - mx), jnp.exp(l2 - mx)
        ob = (e0 * o0_ref[0, 0, rs, :].astype(F32) + e1 * unsplit(so1_ref, rs)
              + e2 * unsplit(so2_ref, rs)) / (e0 + e1 + e2)
        return ob.astype(BF16)

    obs = [dil_mix(rs) for rs in subs]
    pas = [jnp.dot(oa_ref[0, rs, :], wa_ref[...], preferred_element_type=F32) for rs in subs]
    pbs = [jnp.dot(ob, wb_ref[...], preferred_element_type=F32) for ob in obs]

    def gate(rs, pa, pb):
        ga = jax.nn.sigmoid(gt_ref[0, rs, :D_MODEL].astype(F32))
        gb = jax.nn.sigmoid(gt_ref[0, rs, D_MODEL:].astype(F32))
        return (ga * pa + gb * pb).astype(BF16)

    mgs = [gate(rs, pa, pb) for rs, pa, pb in zip(subs, pas, pbs)]
    x1s = [x_ref[0, rs, :] + jnp.dot(mg, wo_ref[...], preferred_element_type=F32)
           for rs, mg in zip(subs, mgs)]

    def ffn_norm(rs, x1):
        x1_ref[0, rs, :] = x1
        ms = jnp.mean(x1 * x1, axis=-1, keepdims=True)
        h2f = x1 * lax.rsqrt(ms + EPS) * nf_ref[...]
        hw_ref[0, rs, :] = _pack_words(h2f[:, :D_MODEL // 2], h2f[:, D_MODEL // 2:])
        return h2f.astype(BF16)

    h2s = [ffn_norm(rs, x1) for rs, x1 in zip(subs, x1s)]
    lgs = [jnp.dot(h2, wr_ref[...], preferred_element_type=F32) for h2 in h2s]
    for rs, lg in zip(subs, lgs):
        gi, _ = _route(lg)
        git = jnp.broadcast_to(gi.astype(F32), lg.shape).T
        gid_ref[0, 0, :, rs] = git[:SUBLANES, :].astype(jnp.int32)


def _merge_route(x, oa, dil, gates, wa, wb, wo, nf, wr):
    b, s, _ = x.shape
    tm = TM_MERGE
    d1, d2 = DIL_CONFIGS[1][1], DIL_CONFIGS[2][1]
    tok = lambda bi, i: (bi, i, 0)
    cls = lambda bi, i: (bi, 0, i, 0)
    (o0, s0), (o1, s1), (o2, s2) = dil
    return pl.pallas_call(
        functools.partial(_merge_route_kernel, tm=tm),
        grid=(b, s // tm),
        in_specs=[
            pl.BlockSpec((1, tm, D_MODEL), tok),
            pl.BlockSpec((1, tm, DIFF_V), tok),
            pl.BlockSpec((1, 1, tm, GRP_W), cls), pl.BlockSpec((1, 1, tm, GRP_W), cls),
            pl.BlockSpec((1, d1, tm // d1, GRP_W), cls), pl.BlockSpec((1, d1, tm // d1, GRP_W), cls),
            pl.BlockSpec((1, d2, tm // d2, GRP_W), cls), pl.BlockSpec((1, d2, tm // d2, GRP_W), cls),
            pl.BlockSpec((1, tm, GATE_COLS), tok),
            _const_spec((DIFF_V, D_MODEL)),
            _const_spec((GRP_W, D_MODEL)),
            _const_spec((D_MODEL, D_MODEL)),
            _const_spec((1, D_MODEL)),
            _const_spec((D_MODEL, ROUTE_LANES)),
        ],
        out_specs=[
            pl.BlockSpec((1, tm, D_MODEL), tok),
            pl.BlockSpec((1, tm, D_MODEL // 2), tok),
            pl.BlockSpec((1, 1, SUBLANES, tm), lambda bi, i: (bi, i, 0, 0)),
        ],
        out_shape=[
            jax.ShapeDtypeStruct((b, s, D_MODEL), F32),
            jax.ShapeDtypeStruct((b, s, D_MODEL // 2), jnp.uint32),
            jax.ShapeDtypeStruct((b, s // tm, SUBLANES, tm), jnp.int32),
        ],
        scratch_shapes=[pltpu.VMEM((GRP_W // LANES, tm, LANES), F32) for _ in range(4)],
        compiler_params=_params(("parallel", "parallel")),
        name="merge_route",
    )(x, oa, o0, s0, o1, s1, o2, s2, gates, wa, wb, wo, nf, wr)


def _dispatch_plan_kernel(gid_ref, dest_ref, tgrp_ref, *, rows, cols):
    gid = gid_ref[...]
    ci = lax.broadcasted_iota(jnp.int32, (cols, cols), 0)
    cj = lax.broadcasted_iota(jnp.int32, (cols, cols), 1)
    upto = (ci <= cj).astype(BF16)
    ri = lax.broadcasted_iota(jnp.int32, (rows, rows), 0)
    rj = lax.broadcasted_iota(jnp.int32, (rows, rows), 1)
    above = (rj < ri).astype(F32)
    seg = jnp.zeros((1, 1), F32)
    dest = jnp.zeros((rows, cols), F32)
    ends = []
    for g in range(N_GROUPS):
        mem = (gid == g).astype(F32)
        pref = jnp.dot(mem.astype(BF16), upto, preferred_element_type=F32)
        tot = pref[:, cols - 1:cols]
        before = jnp.dot(above, jnp.broadcast_to(tot, (rows, LANES)), preferred_element_type=F32,
                         precision=lax.Precision.HIGHEST)[:, :1]
        dest = dest + mem * (seg + before + pref - 1.0)
        cnt = jnp.sum(tot, axis=0, keepdims=True)
        seg = seg + jnp.ceil(cnt / TM_MOE) * TM_MOE
        ends.append(seg)
    dest_ref[...] = dest.astype(jnp.int32)
    tile = (lax.broadcasted_iota(jnp.int32, (SUBLANES, LANES), 0) * LANES
            + lax.broadcasted_iota(jnp.int32, (SUBLANES, LANES), 1))
    start = (tile * TM_MOE).astype(F32)
    tgrp = jnp.zeros((SUBLANES, LANES), jnp.int32)
    for g in range(N_GROUPS - 1):
        tgrp = tgrp + (start >= ends[g]).astype(jnp.int32)
    tgrp_ref[...] = tgrp


def _dispatch_plan(gid):
    rows, cols = gid.shape
    assert rows % SUBLANES == 0 and rows * cols // TM_MOE + N_GROUPS <= SUBLANES * LANES
    return pl.pallas_call(
        functools.partial(_dispatch_plan_kernel, rows=rows, cols=cols),
        out_shape=[jax.ShapeDtypeStruct((rows, cols), jnp.int32),
                   jax.ShapeDtypeStruct((SUBLANES, LANES), jnp.int32)],
        compiler_params=pltpu.CompilerParams(vmem_limit_bytes=VMEM_LIMIT),
        name="dispatch_plan",
    )(gid)


def _moe_group_kernel(tgrp_ref, hw_ref, wr_ref, wg_ref, wu_ref, wd_ref, ow_ref, he_ref):
    g = tgrp_ref[pl.program_id(0)]
    lo, hi = _unpack_words(hw_ref[...])
    h2 = jnp.concatenate([lo, hi], axis=1).astype(BF16)
    _, dw = _route(jnp.dot(h2, wr_ref[...], preferred_element_type=F32))
    lane = lax.broadcasted_iota(jnp.int32, dw.shape, 1)
    for e in range(E_PER_GROUP):
        we = jnp.sum(jnp.where(lane == EXP_LANE0 + E_PER_GROUP * g + e, dw, 0.0), axis=-1, keepdims=True)
        gp = jnp.dot(h2, wg_ref[0, e], preferred_element_type=F32)
        up = jnp.dot(h2, wu_ref[0, e], preferred_element_type=F32)
        he_ref[:, e * D_EXPERT:(e + 1) * D_EXPERT] = (jax.nn.silu(gp) * up * we).astype(BF16)
    moe = jnp.dot(he_ref[...], wd_ref[0], preferred_element_type=F32)
    ow_ref[...] = _pack_words(moe[:, :D_MODEL // 2], moe[:, D_MODEL // 2:])


def _moe_groups(tgrp, hw_sorted, wr, wg, wu, wd):
    m = hw_sorted.shape[0]
    tm = TM_MOE
    row = lambda i, t: (i, 0)
    grp = lambda i, t: (t[i], 0, 0, 0)
    return pl.pallas_call(
        _moe_group_kernel,
        grid_spec=pltpu.PrefetchScalarGridSpec(
            num_scalar_prefetch=1,
            grid=(m // tm,),
            in_specs=[
                pl.BlockSpec((tm, D_MODEL // 2), row),
                pl.BlockSpec((D_MODEL, ROUTE_LANES), lambda i, t: (0, 0)),
                pl.BlockSpec((1, E_PER_GROUP, D_MODEL, D_EXPERT), grp),
                pl.BlockSpec((1, E_PER_GROUP, D_MODEL, D_EXPERT), grp),
                pl.BlockSpec((1, E_PER_GROUP * D_EXPERT, D_MODEL), lambda i, t: (t[i], 0, 0)),
            ],
            out_specs=pl.BlockSpec((tm, D_MODEL // 2), row),
            scratch_shapes=[pltpu.VMEM((tm, E_PER_GROUP * D_EXPERT), BF16)],
        ),
        out_shape=jax.ShapeDtypeStruct((m, D_MODEL // 2), jnp.uint32),
        compiler_params=_params(("arbitrary",)),
        name="moe_groups",
    )(tgrp, hw_sorted, wr, wg, wu, wd)


def _final_norm_kernel(x1_ref, yw_ref, nf_ref, y_ref):
    lo, hi = _unpack_words(yw_ref[...])
    x2 = x1_ref[...] + jnp.concatenate([lo, hi], axis=1)
    ms = jnp.mean(x2 * x2, axis=-1, keepdims=True)
    y_ref[...] = x2 * lax.rsqrt(ms + EPS) * nf_ref[...]


def _final_norm(x1, yw, nf):
    n = x1.shape[0]
    tm = TM_MOE
    row = lambda i: (i, 0)
    return pl.pallas_call(
        _final_norm_kernel,
        grid=(n // tm,),
        in_specs=[pl.BlockSpec((tm, D_MODEL), row), pl.BlockSpec((tm, D_MODEL // 2), row),
                  _const_spec((1, D_MODEL))],
        out_specs=pl.BlockSpec((tm, D_MODEL), row),
        out_shape=jax.ShapeDtypeStruct((n, D_MODEL), F32),
        compiler_params=_params(("parallel",)),
        name="final_norm",
    )(x1, yw, nf)


def _sc_mesh():
    return plsc.VectorSubcoreMesh(core_axis_name="core", subcore_axis_name="subcore")


def _sc_scatter_rows(x, idx, m):
    n, d = x.shape

    @pl.kernel(out_type=jax.ShapeDtypeStruct((m, d), x.dtype), mesh=_sc_mesh(), scratch_types=[])
    def scatter(x_hbm, i_hbm, o_hbm):
        for c in range(d // SC_COLS):
            cols = pl.ds(c * SC_COLS, SC_COLS)
            o_cols = o_hbm.at[:, cols]

            def body(x_vmem, i_vmem, o_cols=o_cols):
                pltpu.sync_copy(x_vmem, o_cols.at[i_vmem.at[0]])

            pltpu.emit_pipeline(
                body,
                grid=(n // SC_WINDOW,),
                in_specs=[pl.BlockSpec((SC_WINDOW, SC_COLS), lambda i, c=c: (i, c)),
                          pl.BlockSpec((1, SC_WINDOW), lambda i: (0, i))],
                out_specs=[],
                core_axis_name=("core", "subcore"),
                dimension_semantics=(pltpu.PARALLEL,),
            )(x_hbm, i_hbm)

    return scatter(x, idx.reshape(1, n))


def _sc_gather_rows(x, idx):
    n = idx.shape[0]
    d = x.shape[1]

    @pl.kernel(out_type=jax.ShapeDtypeStruct((n, d), x.dtype), mesh=_sc_mesh(), scratch_types=[])
    def gather(x_hbm, i_hbm, o_hbm):
        for c in range(d // SC_COLS):
            x_cols = x_hbm.at[:, pl.ds(c * SC_COLS, SC_COLS)]

            def body(i_vmem, o_vmem, x_cols=x_cols):
                pltpu.sync_copy(x_cols.at[i_vmem.at[0]], o_vmem)

            pltpu.emit_pipeline(
                body,
                grid=(n // SC_WINDOW,),
                in_specs=[pl.BlockSpec((1, SC_WINDOW), lambda i: (0, i))],
                out_specs=[pl.BlockSpec((SC_WINDOW, SC_COLS), lambda i, c=c: (i, c))],
                core_axis_name=("core", "subcore"),
                dimension_semantics=(pltpu.PARALLEL,),
            )(i_hbm, o_hbm)

    return gather(x, idx.reshape(1, n))


def _prepare_weights(rel_bias, norm_mix, w_in, lam_q1, lam_k1, lam_q2, lam_k2, subln_g,
                     w_branch_a, w_branch_b, w_out, norm_ffn, w_group, w_router,
                     w_exp_gate, w_exp_up, w_exp_down, norm_final):
    w = w_in[0]
    c = QK_SCALE * LOG2E
    o = 0
    qa, o = w[:, o:o + DIFF_QK] * c, o + DIFF_QK
    ka, o = w[:, o:o + DIFF_QK], o + DIFF_QK
    va, o = w[:, o:o + DIFF_V], o + DIFF_V
    qb, o = w[:, o:o + DIL_W] * c, o + DIL_W
    kb, o = w[:, o:o + DIL_W], o + DIL_W
    vb, o = w[:, o:o + DIL_W], o + DIL_W
    w_gate = w[:, o:].astype(BF16)
    grp = lambda g: [t[:, g * GRP_W:(g + 1) * GRP_W] for t in (qb, kb, vb)]
    w_nat = jnp.concatenate([qa, ka, va] + grp(0), axis=1).astype(BF16)
    w_d1 = jnp.concatenate(grp(1), axis=1).astype(BF16)
    w_d2 = jnp.concatenate(grp(2), axis=1).astype(BF16)
    wr = jnp.concatenate([w_group[0], w_router[0].transpose(1, 0, 2).reshape(D_MODEL, N_EXPERTS)], axis=1)
    wr = jnp.pad(wr, ((0, 0), (0, ROUTE_LANES - wr.shape[1]))).astype(BF16)
    return dict(
        rel_bias=rel_bias.astype(F32),
        norm_mix=norm_mix[0][None].astype(F32),
        w_nat=w_nat, w_gate=w_gate, w_d1=w_d1, w_d2=w_d2,
        lam=jnp.stack([lam_q1[0], lam_k1[0], lam_q2[0], lam_k2[0]]).astype(F32),
        subln_g=subln_g[0][None].astype(F32),
        wa=w_branch_a[0].astype(BF16), wb=w_branch_b[0].astype(BF16), wo=w_out[0].astype(BF16),
        norm_ffn=norm_ffn[0][None].astype(F32), wr=wr,
        wg=w_exp_gate[0].reshape(N_GROUPS, E_PER_GROUP, D_MODEL, D_EXPERT).astype(BF16),
        wu=w_exp_up[0].reshape(N_GROUPS, E_PER_GROUP, D_MODEL, D_EXPERT).astype(BF16),
        wd=w_exp_down[0].reshape(N_GROUPS, E_PER_GROUP * D_EXPERT, D_MODEL).astype(BF16),
        norm_final=norm_final[None].astype(F32),
    )


def _encoder(x, p):
    b, s, d = x.shape
    n = b * s
    nat, gates, dil1, dil2 = _norm_proj(x, p["norm_mix"], p["w_nat"], p["w_gate"], p["w_d1"], p["w_d2"])
    tiles, far = _diff_bias_tiles(p["rel_bias"], min(TQ_DIFF, s), min(TK_DIFF, s))
    oa = _diff_attn(nat, tiles, far, p["lam"], p["subln_g"])
    srcs = ((nat.reshape(b, 1, s, NAT_COLS), (2 * DIFF_QK + DIFF_V) // GRP_W), (dil1, 0), (dil2, 0))
    dil = []
    for g, (win, dilation) in enumerate(DIL_CONFIGS):
        tiles = _dil_bias_tiles(p["rel_bias"], g, s, win, dilation)
        dil.append(_dil_attn(srcs[g][0], srcs[g][1], tiles, g, win, dilation))
    x1, hw, gid = _merge_route(x, oa, dil, gates, p["wa"], p["wb"], p["wo"], p["norm_ffn"], p["wr"])
    dest, tgrp = _dispatch_plan(gid[:, :, 0, :].reshape(n // TM_MERGE, TM_MERGE))
    dest = dest.reshape(n)
    n_tiles = n // TM_MOE + N_GROUPS
    hw_sorted = _sc_scatter_rows(hw.reshape(n, d // 2), dest, n_tiles * TM_MOE)
    yw_sorted = _moe_groups(tgrp.reshape(-1)[:n_tiles], hw_sorted, p["wr"], p["wg"], p["wu"], p["wd"])
    y = _final_norm(x1.reshape(n, d), _sc_gather_rows(yw_sorted, dest), p["norm_final"])
    return y.reshape(b, s, d)


def kernel(x_prompt, x_sample, rel_bias, norm_mix, w_in, lam_q1, lam_k1, lam_q2, lam_k2, subln_g,
           w_branch_a, w_branch_b, w_out, norm_ffn, w_group, w_router, w_exp_gate, w_exp_up,
           w_exp_down, norm_final):
    p = _prepare_weights(rel_bias, norm_mix, w_in, lam_q1, lam_k1, lam_q2, lam_k2, subln_g,
                         w_branch_a, w_branch_b, w_out, norm_ffn, w_group, w_router,
                         w_exp_gate, w_exp_up, w_exp_down, norm_final)
    return (_encoder(x_prompt, p), _encoder(x_sample, p))
```

```python
import functools
import math

import jax
import jax.numpy as jnp
from jax import lax
from jax.experimental import pallas as pl
from jax.experimental.pallas import tpu as pltpu
from jax.experimental.pallas import tpu_sc as plsc

F32 = jnp.float32
BF16 = jnp.bfloat16

D_MODEL = 1024
HEAD_DIM = 64
H_DIFF = 4
H_DIL = 4
DIL_CONFIGS = ((128, 1), (512, 4), (2048, 16))
N_DIL = len(DIL_CONFIGS)
DIFF_QK = H_DIFF * 2 * HEAD_DIM
DIFF_V = H_DIFF * 2 * HEAD_DIM
DIL_W = N_DIL * H_DIL * HEAD_DIM
GRP_W = H_DIL * HEAD_DIM
GRP_COLS = 3 * GRP_W
NAT_COLS = 2 * DIFF_QK + DIFF_V + GRP_COLS
GATE_COLS = 2 * D_MODEL
Q_BLOCK = 128
N_BUCKETS = 32
MAX_DISTANCE = 128
N_GROUPS = 4
E_PER_GROUP = 4
N_EXPERTS = N_GROUPS * E_PER_GROUP
D_EXPERT = 256
EPS = 1e-6
NEG_INF = -1e30
LAM_INIT = 0.8 - 0.6 * math.exp(-0.3 * 0)
LOG2E = math.log2(math.e)
LN2 = math.log(2.0)
QK_SCALE = HEAD_DIM ** -0.5

LANES = 128
SUBLANES = 8
VMEM_LIMIT = 56 * 1024 * 1024

TM_PROJ = 512
TQ_DIFF = 1024
TK_DIFF = 256
DIFF_SLOTS = 4
DEN_ROWS = 16
DIL_QB = 256
DIL_STEP_ROWS = 1024
SC_WINDOW = 128
SC_COLS = 256
TM_MERGE = 512
SUB_MERGE = 256
TM_MOE = 512
ROUTE_LANES = 128
EXP_LANE0 = N_GROUPS


def _const_spec(shape):
    nd = len(shape)
    return pl.BlockSpec(shape, lambda *_: (0,) * nd, pipeline_mode=pl.Buffered(1))


def _params(sem):
    return pltpu.CompilerParams(dimension_semantics=sem, vmem_limit_bytes=VMEM_LIMIT)


def _rel_bucket(rel):
    half = N_BUCKETS // 2
    max_exact = half // 2
    ret = (rel > 0).astype(jnp.int32) * half
    n = jnp.abs(rel)
    nf = jnp.maximum(n, 1).astype(F32)
    large = max_exact + (jnp.log(nf / max_exact) / math.log(MAX_DISTANCE / max_exact)
                         * (half - max_exact)).astype(jnp.int32)
    large = jnp.minimum(large, half - 1)
    return ret + jnp.where(n < max_exact, n, large)


def _bias_of_rel(table, rel):
    bucket = _rel_bucket(rel)[None]
    t = table.astype(F32)
    col = (t.shape[1],) + (1,) * rel.ndim
    out = jnp.zeros((t.shape[1],) + rel.shape, F32)
    for k in range(N_BUCKETS):
        out = jnp.where(bucket == k, t[k].reshape(col), out)
    return out


def _chunks(total, width):
    out, c = [], 0
    while c < total:
        w = min(width, total - c)
        out.append((c, w))
        c += w
    return out


def _norm_proj_kernel(x_ref, g_ref, wn_ref, wg_ref, w1_ref, w2_ref,
                      on_ref, og_ref, o1_ref, o2_ref, hs_ref, *, tm):
    x = x_ref[0]
    ms = jnp.mean(x * x, axis=-1, keepdims=True)
    hf = x * lax.rsqrt(ms + EPS) * g_ref[...]
    for c in range(D_MODEL // LANES):
        hs_ref[c] = hf[:, c * LANES:(c + 1) * LANES]
    h = hf.astype(BF16)
    for c0, cw in _chunks(NAT_COLS, 512):
        on_ref[0, :, c0:c0 + cw] = jnp.dot(h, wn_ref[:, c0:c0 + cw],
                                           preferred_element_type=F32).astype(BF16)
    for c0, cw in _chunks(GATE_COLS, 512):
        og_ref[0, :, c0:c0 + cw] = jnp.dot(h, wg_ref[:, c0:c0 + cw],
                                           preferred_element_type=F32).astype(BF16)
    for w_ref, o_ref, d in ((w1_ref, o1_ref, DIL_CONFIGS[1][1]), (w2_ref, o2_ref, DIL_CONFIGS[2][1])):
        rows = tm // d
        hp = jnp.concatenate(
            [jnp.concatenate([hs_ref[c, pl.ds(r, rows, stride=d), :] for r in range(d)], axis=0)
             for c in range(D_MODEL // LANES)], axis=1).astype(BF16)
        for c0, cw in _chunks(GRP_COLS, 256):
            res = jnp.dot(hp, w_ref[:, c0:c0 + cw], preferred_element_type=F32).astype(BF16)
            o_ref[0, :, :, c0:c0 + cw] = res.reshape(d, rows, cw)


def _norm_proj(x, g, w_nat, w_gate, w_d1, w_d2):
    b, s, _ = x.shape
    tm = TM_PROJ
    d1, d2 = DIL_CONFIGS[1][1], DIL_CONFIGS[2][1]
    return pl.pallas_call(
        functools.partial(_norm_proj_kernel, tm=tm),
        grid=(b, s // tm),
        in_specs=[
            pl.BlockSpec((1, tm, D_MODEL), lambda bi, i: (bi, i, 0)),
            _const_spec((1, D_MODEL)),
            _const_spec((D_MODEL, NAT_COLS)),
            _const_spec((D_MODEL, GATE_COLS)),
            _const_spec((D_MODEL, GRP_COLS)),
            _const_spec((D_MODEL, GRP_COLS)),
        ],
        out_specs=[
            pl.BlockSpec((1, tm, NAT_COLS), lambda bi, i: (bi, i, 0)),
            pl.BlockSpec((1, tm, GATE_COLS), lambda bi, i: (bi, i, 0)),
            pl.BlockSpec((1, d1, tm // d1, GRP_COLS), lambda bi, i: (bi, 0, i, 0)),
            pl.BlockSpec((1, d2, tm // d2, GRP_COLS), lambda bi, i: (bi, 0, i, 0)),
        ],
        out_shape=[
            jax.ShapeDtypeStruct((b, s, NAT_COLS), BF16),
            jax.ShapeDtypeStruct((b, s, GATE_COLS), BF16),
            jax.ShapeDtypeStruct((b, d1, s // d1, GRP_COLS), BF16),
            jax.ShapeDtypeStruct((b, d2, s // d2, GRP_COLS), BF16),
        ],
        scratch_shapes=[pltpu.VMEM((D_MODEL // LANES, tm, LANES), F32)],
        compiler_params=_params(("parallel", "parallel")),
        name="norm_proj",
    )(x, g, w_nat, w_gate, w_d1, w_d2)


def _diff_bias_tiles(rel_bias, tq, tk):
    ratio, w = tq // tk, tq + tk
    t = jnp.arange(-2, ratio + 2)[:, None]
    m = jnp.arange(w)[None, :]
    rel = t * tk + jnp.where(m < tq, -m, w - m)
    tiles = _bias_of_rel(rel_bias[:, :H_DIFF], rel) * LOG2E
    far = _bias_of_rel(rel_bias[:, :H_DIFF], jnp.asarray([-MAX_DISTANCE, MAX_DISTANCE])) * LOG2E
    return tiles, far


def _diff_attn_kernel(lam_ref, far_ref, q_ref, k_ref, v_ref, brow_ref, g_ref, o_ref,
                      vt_ref, st_ref, p_ref, acc_ref, bias_ref, *, tq, tk, nkb):
    h = pl.program_id(0)
    qi = pl.program_id(2)
    ratio = tq // tk
    n_band = ratio + 2
    vchunk = min(512, nkb * tk)

    @pl.when((pl.program_id(1) == 0) & (qi == 0))
    def _():
        for t in range(ratio + 4):
            rows = jnp.broadcast_to(brow_ref[0, t:t + 1, :], (tk, tq + tk))
            bias_ref[t] = pltpu.roll(rows, 0, 1, stride=1, stride_axis=0)[:, :tq]

    @pl.when(qi == 0)
    def _():
        for c in range(nkb * tk // vchunk):
            cs = slice(c * vchunk, (c + 1) * vchunk)
            vt_ref[:LANES, cs] = v_ref[0, cs, :].astype(F32).T.astype(BF16)
        vt_ref[LANES:, :] = jnp.ones((DEN_ROWS, nkb * tk), BF16)

    qt = q_ref[0].astype(F32).T
    first = lax.broadcasted_iota(jnp.int32, (LANES, tq), 0) < HEAD_DIM
    qts = (jnp.where(first, qt, 0.0).astype(BF16), jnp.where(first, 0.0, qt).astype(BF16))
    kb0 = qi * ratio

    def key_block(j):
        kb = kb0 - 1 + j
        return jnp.where(kb < 0, kb + nkb, jnp.where(kb >= nkb, kb - nkb, kb))

    def scores(j, slot, band):
        kb = key_block(j)
        ks = pl.multiple_of(kb * tk, tk)
        k = k_ref[0, pl.ds(ks, tk), :]
        if band:
            bt = bias_ref[jnp.clip(kb - kb0, -2, ratio + 1) + 2]
        mx = []
        for mp in range(2):
            st = jnp.dot(k, qts[mp], preferred_element_type=F32)
            if band:
                st = st + bt
            st_ref[slot, mp] = st
            mx.append(jnp.max(st, axis=0, keepdims=True))
        return tuple(mx)

    def softmax(j, slot, stats, mx, band):
        if not band:
            kb = key_block(j)
            far_c = jnp.where(kb < kb0, far_ref[h, 0], far_ref[h, 1])
        out, alphas = [], []
        for mp in range(2):
            m = stats[mp]
            if band:
                m_new = jnp.maximum(m, mx[mp])
                shift = m_new
            else:
                m_new = jnp.maximum(m, mx[mp] + far_c)
                shift = m_new - far_c
            p_ref[slot, mp] = jnp.exp2((st_ref[slot, mp] - shift).astype(BF16))
            out.append(m_new)
            alphas.append(jnp.exp2(m - m_new))
        return tuple(out), tuple(alphas)

    def values(j, slot, alphas):
        ks = pl.multiple_of(key_block(j) * tk, tk)
        vt = vt_ref[:, pl.ds(ks, tk)]
        for mp in range(2):
            acc_ref[mp] = alphas[mp] * acc_ref[mp] + jnp.dot(vt, p_ref[slot, mp],
                                                             preferred_element_type=F32)

    acc_ref[...] = jnp.zeros_like(acc_ref)
    mx = scores(0, 0, True)
    stats = (jnp.full((1, tq), -jnp.inf, F32),) * 2
    alphas = None
    for j in range(nkb):
        if j + 1 < nkb:
            mx_next = scores(j + 1, (j + 1) % DIFF_SLOTS, j + 1 < n_band)
        if j >= 1:
            values(j - 1, (j - 1) % DIFF_SLOTS, alphas)
        stats, alphas = softmax(j, j % DIFF_SLOTS, stats, mx, j < n_band)
        mx = mx_next
    values(nkb - 1, (nkb - 1) % DIFF_SLOTS, alphas)
    l0, l1 = acc_ref[0, LANES:LANES + 1, :], acc_ref[1, LANES:LANES + 1, :]

    lp = lam_ref[...]
    lam = (jnp.exp(jnp.sum(lp[0:1] * lp[1:2], axis=-1, keepdims=True))
           - jnp.exp(jnp.sum(lp[2:3] * lp[3:4], axis=-1, keepdims=True)) + LAM_INIT)
    ot = acc_ref[0, :LANES, :] / l0 - lam * (acc_ref[1, :LANES, :] / l1)
    ms = jnp.mean(ot * ot, axis=0, keepdims=True)
    y = (ot * lax.rsqrt(ms + EPS)).T * g_ref[...] * (1.0 - LAM_INIT)
    o_ref[0] = y.astype(BF16)


def _diff_attn(nat, bias_tiles, far, lam_params, subln_g):
    b, s, _ = nat.shape
    tq, tk = min(TQ_DIFF, s), min(TK_DIFF, s)
    nt = bias_tiles.shape[1]
    ratio, nkb = tq // tk, s // tk
    assert tq % tk == 0 and tk >= MAX_DISTANCE and nkb >= ratio + 2
    assert DIFF_SLOTS >= 3
    kern = functools.partial(_diff_attn_kernel, tq=tq, tk=tk, nkb=s // tk)
    return pl.pallas_call(
        kern,
        grid=(H_DIFF, b, s // tq),
        in_specs=[
            _const_spec((4, HEAD_DIM)),
            pl.BlockSpec(memory_space=pltpu.SMEM),
            pl.BlockSpec((1, tq, LANES), lambda h, bi, qi: (bi, qi, h)),
            pl.BlockSpec((1, s, LANES), lambda h, bi, qi: (bi, 0, H_DIFF + h)),
            pl.BlockSpec((1, s, LANES), lambda h, bi, qi: (bi, 0, 2 * H_DIFF + h)),
            pl.BlockSpec((1, nt, tq + tk), lambda h, bi, qi: (h, 0, 0)),
            _const_spec((1, 2 * HEAD_DIM)),
        ],
        out_specs=pl.BlockSpec((1, tq, LANES), lambda h, bi, qi: (bi, qi, h)),
        out_shape=jax.ShapeDtypeStruct((b, s, DIFF_V), BF16),
        scratch_shapes=[
            pltpu.VMEM((LANES + DEN_ROWS, s), BF16),
            pltpu.VMEM((DIFF_SLOTS, 2, tk, tq), F32),
            pltpu.VMEM((DIFF_SLOTS, 2, tk, tq), BF16),
            pltpu.VMEM((2, LANES + DEN_ROWS, tq), F32),
            pltpu.VMEM((nt, tk, tq), F32),
        ],
        compiler_params=_params(("arbitrary", "arbitrary", "arbitrary")),
        name="diff_attn",
    )(lam_params, far, nat, nat, nat, bias_tiles, subln_g)


def _dil_geometry(s, window, dilation):
    l = s // dilation
    w = window // (2 * dilation)
    qb = min(DIL_QB, l)
    kw = min(qb + 2 * LANES, l)
    assert w <= LANES and l % qb == 0 and l % LANES == 0
    return l, w, qb, kw, l // qb


def _dil_bias_tiles(rel_bias, g, s, window, dilation):
    l, w, qb, kw, nb = _dil_geometry(s, window, dilation)
    c0 = H_DIFF + g * H_DIL
    shifts = []
    for n in (0, min(1, nb - 1), nb - 1):
        l0 = n * qb
        shifts.append(min(max(l0 - LANES, 0), l - kw) - l0)
    j = jnp.arange(kw)[None, :, None]
    i = jnp.arange(qb)[None, None, :]
    off = jnp.asarray(shifts)[:, None, None] + j - i
    b = _bias_of_rel(rel_bias[:, c0:c0 + H_DIL], off * dilation) * LOG2E
    b = jnp.where((jnp.abs(off) <= w)[None], b, NEG_INF)
    return jnp.stack([jnp.concatenate([b[2 * p], b[2 * p + 1]], axis=-1) for p in range(H_DIL // 2)], axis=1)


def _dil_attn_kernel(q_ref, k_ref, v_ref, bias_ref, o_ref, lse_ref, qt_ref, vt_ref, st_ref,
                     *, l, qb, kw, nb, cps):
    chunk = min(512, l)
    first = lax.broadcasted_iota(jnp.int32, (LANES, qb), 0) < HEAD_DIM

    def transposes(c):
        for pair in range(H_DIL // 2):
            cs = slice(pair * LANES, (pair + 1) * LANES)
            for i in range(l // chunk):
                rs = slice(i * chunk, (i + 1) * chunk)
                qt_ref[c, pair, :, rs] = q_ref[0, c, rs, cs].astype(F32).T.astype(BF16)
                vt_ref[c, pair, :LANES, rs] = v_ref[0, c, rs, cs].astype(F32).T.astype(BF16)
            vt_ref[c, pair, LANES:, :] = jnp.ones((DEN_ROWS, l), BF16)

    def window(n):
        l0 = n * qb
        return l0, min(max(l0 - LANES, 0), l - kw)

    def scores(c, n, slot):
        l0, ws = window(n)
        for pair in range(H_DIL // 2):
            qt = qt_ref[c, pair, :, l0:l0 + qb]
            q2 = jnp.concatenate([jnp.where(first, qt, jnp.zeros_like(qt)),
                                  jnp.where(first, jnp.zeros_like(qt), qt)], axis=1)
            kp = k_ref[0, c, ws:ws + kw, pair * LANES:(pair + 1) * LANES]
            st_ref[c, slot, pair] = jnp.dot(kp, q2, preferred_element_type=F32)

    def finish(c, n, slot):
        l0, ws = window(n)
        t = 0 if n == 0 else (2 if n == nb - 1 else 1)
        for pair in range(H_DIL // 2):
            cs = slice(pair * LANES, (pair + 1) * LANES)
            st = st_ref[c, slot, pair] + bias_ref[t, pair]
            mx = jnp.max(st, axis=0, keepdims=True)
            p = jnp.exp2(st - mx).astype(BF16)
            acc = jnp.dot(vt_ref[c, pair, :, ws:ws + kw], p, preferred_element_type=F32)
            den = acc[LANES:LANES + 1, :]
            ot = acc[:LANES, :] / den
            lse = mx * LN2 + jnp.log(den)
            o_t = jnp.where(first, ot[:, :qb], ot[:, qb:])
            lse_t = jnp.where(first, lse[:, :qb], lse[:, qb:])
            o_ref[0, c, l0:l0 + qb, cs] = o_t.T.astype(BF16)
            lse_ref[0, c, l0:l0 + qb, cs] = lse_t.T

    for c in range(cps):
        transposes(c)
    for c in range(cps):
        scores(c, 0, 0)
    for c in range(cps):
        for n in range(nb):
            if n + 1 < nb:
                scores(c, n + 1, (n + 1) % 2)
            finish(c, n, n % 2)


def _dil_attn(arr, col0, bias_tiles, g, window, dilation):
    b, d, l, _ = arr.shape
    _, w, qb, kw, nb = _dil_geometry(l * d, window, dilation)
    cps = max(1, min(d, DIL_STEP_ROWS // l))
    kern = functools.partial(_dil_attn_kernel, l=l, qb=qb, kw=kw, nb=nb, cps=cps)
    return pl.pallas_call(
        kern,
        grid=(b, d // cps),
        in_specs=[
            pl.BlockSpec((1, cps, l, GRP_W), lambda bi, r: (bi, r, 0, col0)),
            pl.BlockSpec((1, cps, l, GRP_W), lambda bi, r: (bi, r, 0, col0 + 1)),
            pl.BlockSpec((1, cps, l, GRP_W), lambda bi, r: (bi, r, 0, col0 + 2)),
            _const_spec((3, H_DIL // 2, kw, 2 * qb)),
        ],
        out_specs=[
            pl.BlockSpec((1, cps, l, GRP_W), lambda bi, r: (bi, r, 0, 0)),
            pl.BlockSpec((1, cps, l, GRP_W), lambda bi, r: (bi, r, 0, 0)),
        ],
        out_shape=[
            jax.ShapeDtypeStruct((b, d, l, GRP_W), BF16),
            jax.ShapeDtypeStruct((b, d, l, GRP_W), F32),
        ],
        scratch_shapes=[
            pltpu.VMEM((cps, H_DIL // 2, LANES, l), BF16),
            pltpu.VMEM((cps, H_DIL // 2, LANES + DEN_ROWS, l), BF16),
            pltpu.VMEM((cps, 2, H_DIL // 2, kw, 2 * qb), F32),
        ],
        compiler_params=_params(("parallel", "parallel")),
        name=f"dil_attn_g{g}",
    )(arr, arr, arr, bias_tiles)


def _pack_words(a, b):
    ua = lax.bitcast_convert_type(a.astype(BF16).astype(F32), jnp.uint32)
    ub = lax.bitcast_convert_type(b.astype(BF16).astype(F32), jnp.uint32)
    return (ua >> 16) | (ub & jnp.uint32(0xFFFF0000))


def _unpack_words(w):
    return (lax.bitcast_convert_type(w << 16, F32),
            lax.bitcast_convert_type(w & jnp.uint32(0xFFFF0000), F32))


def _route(lg):
    lane = lax.broadcasted_iota(jnp.int32, lg.shape, 1)
    big = jnp.int32(ROUTE_LANES)
    gl = jnp.where(lane < N_GROUPS, lg, -jnp.inf)
    gmax = jnp.max(gl, axis=-1, keepdims=True)
    gi = jnp.min(jnp.where(gl == gmax, lane, big), axis=-1, keepdims=True)
    gw = 1.0 / jnp.sum(jnp.exp(gl - gmax), axis=-1, keepdims=True)
    e_lo = EXP_LANE0 + E_PER_GROUP * gi
    es = jnp.where((lane >= e_lo) & (lane < e_lo + E_PER_GROUP), lg, -jnp.inf)
    v1 = jnp.max(es, axis=-1, keepdims=True)
    i1 = jnp.min(jnp.where(es == v1, lane, big), axis=-1, keepdims=True)
    es2 = jnp.where(lane == i1, -jnp.inf, es)
    v2 = jnp.max(es2, axis=-1, keepdims=True)
    i2 = jnp.min(jnp.where(es2 == v2, lane, big), axis=-1, keepdims=True)
    t = jnp.exp(v2 - v1)
    w1 = gw / (1.0 + t)
    w2 = w1 * t
    return gi, jnp.where(lane == i1, w1, 0.0) + jnp.where(lane == i2, w2, 0.0)


def _merge_route_kernel(x_ref, oa_ref, o0_ref, l0_ref, o1_ref, l1_ref, o2_ref, l2_ref, gt_ref,
                        wa_ref, wb_ref, wo_ref, nf_ref, wr_ref, x1_ref, hw_ref, gid_ref,
                        so1_ref, sl1_ref, so2_ref, sl2_ref, *, tm):
    for src, dst, d in ((o1_ref, so1_ref, DIL_CONFIGS[1][1]), (l1_ref, sl1_ref, DIL_CONFIGS[1][1]),
                        (o2_ref, so2_ref, DIL_CONFIGS[2][1]), (l2_ref, sl2_ref, DIL_CONFIGS[2][1])):
        for r in range(d):
            for c in range(GRP_W // LANES):
                dst[c, pl.ds(r, tm // d, stride=d), :] = src[0, r, :, c * LANES:(c + 1) * LANES].astype(F32)
    subs = [slice(i * SUB_MERGE, (i + 1) * SUB_MERGE) for i in range(tm // SUB_MERGE)]
    unsplit = lambda ref, rs: jnp.concatenate([ref[c, rs, :] for c in range(GRP_W // LANES)], axis=1)

    def dil_mix(rs):
        l0, l1, l2 = l0_ref[0, 0, rs, :], unsplit(sl1_ref, rs), unsplit(sl2_ref, rs)
        mx = jnp.maximum(jnp.maximum(l0, l1), l2)
        e0, e1, e2 = jnp.exp(l0 - mx), jnp.exp(l1 - mx), jnp.exp(l2 - mx)
        ob = (e0 * o0_ref[0, 0, rs, :].astype(F32) + e1 * unsplit(so1_ref, rs)
              + e2 * unsplit(so2_ref, rs)) / (e0 + e1 + e2)
        return ob.astype(BF16)

    obs = [dil_mix(rs) for rs in subs]
    pas = [jnp.dot(oa_ref[0, rs, :], wa_ref[...], preferred_element_type=F32) for rs in subs]
    pbs = [jnp.dot(ob, wb_ref[...], preferred_element_type=F32) for ob in obs]

    def gate(rs, pa, pb):
        ga = jax.nn.sigmoid(gt_ref[0, rs, :D_MODEL].astype(F32))
        gb = jax.nn.sigmoid(gt_ref[0, rs, D_MODEL:].astype(F32))
        return (ga * pa + gb * pb).astype(BF16)

    mgs = [gate(rs, pa, pb) for rs, pa, pb in zip(subs, pas, pbs)]
    x1s = [x_ref[0, rs, :] + jnp.dot(mg, wo_ref[...], preferred_element_type=F32)
           for rs, mg in zip(subs, mgs)]

    def ffn_norm(rs, x1):
        x1_ref[0, rs, :] = x1
        ms = jnp.mean(x1 * x1, axis=-1, keepdims=True)
        h2f = x1 * lax.rsqrt(ms + EPS) * nf_ref[...]
        hw_ref[0, rs, :] = _pack_words(h2f[:, :D_MODEL // 2], h2f[:, D_MODEL // 2:])
        return h2f.astype(BF16)

    h2s = [ffn_norm(rs, x1) for rs, x1 in zip(subs, x1s)]
    lgs = [jnp.dot(h2, wr_ref[...], preferred_element_type=F32) for h2 in h2s]
    for rs, lg in zip(subs, lgs):
        gi, _ = _route(lg)
        git = jnp.broadcast_to(gi.astype(F32), lg.shape).T
        gid_ref[0, 0, :, rs] = git[:SUBLANES, :].astype(jnp.int32)


def _merge_route(x, oa, dil, gates, wa, wb, wo, nf, wr):
    b, s, _ = x.shape
    tm = TM_MERGE
    d1, d2 = DIL_CONFIGS[1][1], DIL_CONFIGS[2][1]
    tok = lambda bi, i: (bi, i, 0)
    cls = lambda bi, i: (bi, 0, i, 0)
    (o0, s0), (o1, s1), (o2, s2) = dil
    return pl.pallas_call(
        functools.partial(_merge_route_kernel, tm=tm),
        grid=(b, s // tm),
        in_specs=[
            pl.BlockSpec((1, tm, D_MODEL), tok),
            pl.BlockSpec((1, tm, DIFF_V), tok),
            pl.BlockSpec((1, 1, tm, GRP_W), cls), pl.BlockSpec((1, 1, tm, GRP_W), cls),
            pl.BlockSpec((1, d1, tm // d1, GRP_W), cls), pl.BlockSpec((1, d1, tm // d1, GRP_W), cls),
            pl.BlockSpec((1, d2, tm // d2, GRP_W), cls), pl.BlockSpec((1, d2, tm // d2, GRP_W), cls),
            pl.BlockSpec((1, tm, GATE_COLS), tok),
            _const_spec((DIFF_V, D_MODEL)),
            _const_spec((GRP_W, D_MODEL)),
            _const_spec((D_MODEL, D_MODEL)),
            _const_spec((1, D_MODEL)),
            _const_spec((D_MODEL, ROUTE_LANES)),
        ],
        out_specs=[
            pl.BlockSpec((1, tm, D_MODEL), tok),
            pl.BlockSpec((1, tm, D_MODEL // 2), tok),
            pl.BlockSpec((1, 1, SUBLANES, tm), lambda bi, i: (bi, i, 0, 0)),
        ],
        out_shape=[
            jax.ShapeDtypeStruct((b, s, D_MODEL), F32),
            jax.ShapeDtypeStruct((b, s, D_MODEL // 2), jnp.uint32),
            jax.ShapeDtypeStruct((b, s // tm, SUBLANES, tm), jnp.int32),
        ],
        scratch_shapes=[pltpu.VMEM((GRP_W // LANES, tm, LANES), F32) for _ in range(4)],
        compiler_params=_params(("parallel", "parallel")),
        name="merge_route",
    )(x, oa, o0, s0, o1, s1, o2, s2, gates, wa, wb, wo, nf, wr)


def _dispatch_plan_kernel(gid_ref, dest_ref, tgrp_ref, *, rows, cols):
    gid = gid_ref[...]
    ci = lax.broadcasted_iota(jnp.int32, (cols, cols), 0)
    cj = lax.broadcasted_iota(jnp.int32, (cols, cols), 1)
    upto = (ci <= cj).astype(BF16)
    ri = lax.broadcasted_iota(jnp.int32, (rows, rows), 0)
    rj = lax.broadcasted_iota(jnp.int32, (rows, rows), 1)
    above = (rj < ri).astype(F32)
    seg = jnp.zeros((1, 1), F32)
    dest = jnp.zeros((rows, cols), F32)
    ends = []
    for g in range(N_GROUPS):
        mem = (gid == g).astype(F32)
        pref = jnp.dot(mem.astype(BF16), upto, preferred_element_type=F32)
        tot = pref[:, cols - 1:cols]
        before = jnp.dot(above, jnp.broadcast_to(tot, (rows, LANES)), preferred_element_type=F32,
                         precision=lax.Precision.HIGHEST)[:, :1]
        dest = dest + mem * (seg + before + pref - 1.0)
        cnt = jnp.sum(tot, axis=0, keepdims=True)
        seg = seg + jnp.ceil(cnt / TM_MOE) * TM_MOE
        ends.append(seg)
    dest_ref[...] = dest.astype(jnp.int32)
    tile = (lax.broadcasted_iota(jnp.int32, (SUBLANES, LANES), 0) * LANES
            + lax.broadcasted_iota(jnp.int32, (SUBLANES, LANES), 1))
    start = (tile * TM_MOE).astype(F32)
    tgrp = jnp.zeros((SUBLANES, LANES), jnp.int32)
    for g in range(N_GROUPS - 1):
        tgrp = tgrp + (start >= ends[g]).astype(jnp.int32)
    tgrp_ref[...] = tgrp


def _dispatch_plan(gid):
    rows, cols = gid.shape
    assert rows % SUBLANES == 0 and rows * cols // TM_MOE + N_GROUPS <= SUBLANES * LANES
    return pl.pallas_call(
        functools.partial(_dispatch_plan_kernel, rows=rows, cols=cols),
        out_shape=[jax.ShapeDtypeStruct((rows, cols), jnp.int32),
                   jax.ShapeDtypeStruct((SUBLANES, LANES), jnp.int32)],
        compiler_params=pltpu.CompilerParams(vmem_limit_bytes=VMEM_LIMIT),
        name="dispatch_plan",
    )(gid)


def _moe_group_kernel(tgrp_ref, hw_ref, wr_ref, wg_ref, wu_ref, wd_ref, ow_ref, he_ref):
    g = tgrp_ref[pl.program_id(0)]
    lo, hi = _unpack_words(hw_ref[...])
    h2 = jnp.concatenate([lo, hi], axis=1).astype(BF16)
    _, dw = _route(jnp.dot(h2, wr_ref[...], preferred_element_type=F32))
    lane = lax.broadcasted_iota(jnp.int32, dw.shape, 1)
    for e in range(E_PER_GROUP):
        we = jnp.sum(jnp.where(lane == EXP_LANE0 + E_PER_GROUP * g + e, dw, 0.0), axis=-1, keepdims=True)
        gp = jnp.dot(h2, wg_ref[0, e], preferred_element_type=F32)
        up = jnp.dot(h2, wu_ref[0, e], preferred_element_type=F32)
        he_ref[:, e * D_EXPERT:(e + 1) * D_EXPERT] = (jax.nn.silu(gp) * up * we).astype(BF16)
    moe = jnp.dot(he_ref[...], wd_ref[0], preferred_element_type=F32)
    ow_ref[...] = _pack_words(moe[:, :D_MODEL // 2], moe[:, D_MODEL // 2:])


def _moe_groups(tgrp, hw_sorted, wr, wg, wu, wd):
    m = hw_sorted.shape[0]
    tm = TM_MOE
    row = lambda i, t: (i, 0)
    grp = lambda i, t: (t[i], 0, 0, 0)
    return pl.pallas_call(
        _moe_group_kernel,
        grid_spec=pltpu.PrefetchScalarGridSpec(
            num_scalar_prefetch=1,
            grid=(m // tm,),
            in_specs=[
                pl.BlockSpec((tm, D_MODEL // 2), row),
                pl.BlockSpec((D_MODEL, ROUTE_LANES), lambda i, t: (0, 0)),
                pl.BlockSpec((1, E_PER_GROUP, D_MODEL, D_EXPERT), grp),
                pl.BlockSpec((1, E_PER_GROUP, D_MODEL, D_EXPERT), grp),
                pl.BlockSpec((1, E_PER_GROUP * D_EXPERT, D_MODEL), lambda i, t: (t[i], 0, 0)),
            ],
            out_specs=pl.BlockSpec((tm, D_MODEL // 2), row),
            scratch_shapes=[pltpu.VMEM((tm, E_PER_GROUP * D_EXPERT), BF16)],
        ),
        out_shape=jax.ShapeDtypeStruct((m, D_MODEL // 2), jnp.uint32),
        compiler_params=_params(("arbitrary",)),
        name="moe_groups",
    )(tgrp, hw_sorted, wr, wg, wu, wd)


def _final_norm_kernel(x1_ref, yw_ref, nf_ref, y_ref):
    lo, hi = _unpack_words(yw_ref[...])
    x2 = x1_ref[...] + jnp.concatenate([lo, hi], axis=1)
    ms = jnp.mean(x2 * x2, axis=-1, keepdims=True)
    y_ref[...] = x2 * lax.rsqrt(ms + EPS) * nf_ref[...]


def _final_norm(x1, yw, nf):
    n = x1.shape[0]
    tm = TM_MOE
    row = lambda i: (i, 0)
    return pl.pallas_call(
        _final_norm_kernel,
        grid=(n // tm,),
        in_specs=[pl.BlockSpec((tm, D_MODEL), row), pl.BlockSpec((tm, D_MODEL // 2), row),
                  _const_spec((1, D_MODEL))],
        out_specs=pl.BlockSpec((tm, D_MODEL), row),
        out_shape=jax.ShapeDtypeStruct((n, D_MODEL), F32),
        compiler_params=_params(("parallel",)),
        name="final_norm",
    )(x1, yw, nf)


def _sc_mesh():
    return plsc.VectorSubcoreMesh(core_axis_name="core", subcore_axis_name="subcore")


def _sc_scatter_rows(x, idx, m):
    n, d = x.shape

    @pl.kernel(out_type=jax.ShapeDtypeStruct((m, d), x.dtype), mesh=_sc_mesh(), scratch_types=[])
    def scatter(x_hbm, i_hbm, o_hbm):
        for c in range(d // SC_COLS):
            cols = pl.ds(c * SC_COLS, SC_COLS)
            o_cols = o_hbm.at[:, cols]

            def body(x_vmem, i_vmem, o_cols=o_cols):
                pltpu.sync_copy(x_vmem, o_cols.at[i_vmem.at[0]])

            pltpu.emit_pipeline(
                body,
                grid=(n // SC_WINDOW,),
                in_specs=[pl.BlockSpec((SC_WINDOW, SC_COLS), lambda i, c=c: (i, c)),
                          pl.BlockSpec((1, SC_WINDOW), lambda i: (0, i))],
                out_specs=[],
                core_axis_name=("core", "subcore"),
                dimension_semantics=(pltpu.PARALLEL,),
            )(x_hbm, i_hbm)

    return scatter(x, idx.reshape(1, n))


def _sc_gather_rows(x, idx):
    n = idx.shape[0]
    d = x.shape[1]

    @pl.kernel(out_type=jax.ShapeDtypeStruct((n, d), x.dtype), mesh=_sc_mesh(), scratch_types=[])
    def gather(x_hbm, i_hbm, o_hbm):
        for c in range(d // SC_COLS):
            x_cols = x_hbm.at[:, pl.ds(c * SC_COLS, SC_COLS)]

            def body(i_vmem, o_vmem, x_cols=x_cols):
                pltpu.sync_copy(x_cols.at[i_vmem.at[0]], o_vmem)

            pltpu.emit_pipeline(
                body,
                grid=(n // SC_WINDOW,),
                in_specs=[pl.BlockSpec((1, SC_WINDOW), lambda i: (0, i))],
                out_specs=[pl.BlockSpec((SC_WINDOW, SC_COLS), lambda i, c=c: (i, c))],
                core_axis_name=("core", "subcore"),
                dimension_semantics=(pltpu.PARALLEL,),
            )(i_hbm, o_hbm)

    return gather(x, idx.reshape(1, n))


def _prepare_weights(rel_bias, norm_mix, w_in, lam_q1, lam_k1, lam_q2, lam_k2, subln_g,
                     w_branch_a, w_branch_b, w_out, norm_ffn, w_group, w_router,
                     w_exp_gate, w_exp_up, w_exp_down, norm_final):
    w = w_in[0]
    c = QK_SCALE * LOG2E
    o = 0
    qa, o = w[:, o:o + DIFF_QK] * c, o + DIFF_QK
    ka, o = w[:, o:o + DIFF_QK], o + DIFF_QK
    va, o = w[:, o:o + DIFF_V], o + DIFF_V
    qb, o = w[:, o:o + DIL_W] * c, o + DIL_W
    kb, o = w[:, o:o + DIL_W], o + DIL_W
    vb, o = w[:, o:o + DIL_W], o + DIL_W
    w_gate = w[:, o:].astype(BF16)
    grp = lambda g: [t[:, g * GRP_W:(g + 1) * GRP_W] for t in (qb, kb, vb)]
    w_nat = jnp.concatenate([qa, ka, va] + grp(0), axis=1).astype(BF16)
    w_d1 = jnp.concatenate(grp(1), axis=1).astype(BF16)
    w_d2 = jnp.concatenate(grp(2), axis=1).astype(BF16)
    wr = jnp.concatenate([w_group[0], w_router[0].transpose(1, 0, 2).reshape(D_MODEL, N_EXPERTS)], axis=1)
    wr = jnp.pad(wr, ((0, 0), (0, ROUTE_LANES - wr.shape[1]))).astype(BF16)
    return dict(
        rel_bias=rel_bias.astype(F32),
        norm_mix=norm_mix[0][None].astype(F32),
        w_nat=w_nat, w_gate=w_gate, w_d1=w_d1, w_d2=w_d2,
        lam=jnp.stack([lam_q1[0], lam_k1[0], lam_q2[0], lam_k2[0]]).astype(F32),
        subln_g=subln_g[0][None].astype(F32),
        wa=w_branch_a[0].astype(BF16), wb=w_branch_b[0].astype(BF16), wo=w_out[0].astype(BF16),
        norm_ffn=norm_ffn[0][None].astype(F32), wr=wr,
        wg=w_exp_gate[0].reshape(N_GROUPS, E_PER_GROUP, D_MODEL, D_EXPERT).astype(BF16),
        wu=w_exp_up[0].reshape(N_GROUPS, E_PER_GROUP, D_MODEL, D_EXPERT).astype(BF16),
        wd=w_exp_down[0].reshape(N_GROUPS, E_PER_GROUP * D_EXPERT, D_MODEL).astype(BF16),
        norm_final=norm_final[None].astype(F32),
    )


def _encoder(x, p):
    b, s, d = x.shape
    n = b * s
    nat, gates, dil1, dil2 = _norm_proj(x, p["norm_mix"], p["w_nat"], p["w_gate"], p["w_d1"], p["w_d2"])
    tiles, far = _diff_bias_tiles(p["rel_bias"], min(TQ_DIFF, s), min(TK_DIFF, s))
    oa = _diff_attn(nat, tiles, far, p["lam"], p["subln_g"])
    srcs = ((nat.reshape(b, 1, s, NAT_COLS), (2 * DIFF_QK + DIFF_V) // GRP_W), (dil1, 0), (dil2, 0))
    dil = []
    for g, (win, dilation) in enumerate(DIL_CONFIGS):
        tiles = _dil_bias_tiles(p["rel_bias"], g, s, win, dilation)
        dil.append(_dil_attn(srcs[g][0], srcs[g][1], tiles, g, win, dilation))
    x1, hw, gid = _merge_route(x, oa, dil, gates, p["wa"], p["wb"], p["wo"], p["norm_ffn"], p["wr"])
    dest, tgrp = _dispatch_plan(gid[:, :, 0, :].reshape(n // TM_MERGE, TM_MERGE))
    dest = dest.reshape(n)
    n_tiles = n // TM_MOE + N_GROUPS
    hw_sorted = _sc_scatter_rows(hw.reshape(n, d // 2), dest, n_tiles * TM_MOE)
    yw_sorted = _moe_groups(tgrp.reshape(-1)[:n_tiles], hw_sorted, p["wr"], p["wg"], p["wu"], p["wd"])
    y = _final_norm(x1.reshape(n, d), _sc_gather_rows(yw_sorted, dest), p["norm_final"])
    return y.reshape(b, s, d)


def kernel(x_prompt, x_sample, rel_bias, norm_mix, w_in, lam_q1, lam_k1, lam_q2, lam_k2, subln_g,
           w_branch_a, w_branch_b, w_out, norm_ffn, w_group, w_router, w_exp_gate, w_exp_up,
           w_exp_down, norm_final):
    p = _prepare_weights(rel_bias, norm_mix, w_in, lam_q1, lam_k1, lam_q2, lam_k2, subln_g,
                         w_branch_a, w_branch_b, w_out, norm_ffn, w_group, w_router,
                         w_exp_gate, w_exp_up, w_exp_down, norm_final)
    return (_encoder(x_prompt, p), _encoder(x_sample, p))
```

```python
import functools
import math

import jax
import jax.numpy as jnp
from jax import lax
from jax.experimental import pallas as pl
from jax.experimental.pallas import tpu as pltpu
from jax.experimental.pallas import tpu_sc as plsc

F32 = jnp.float32
BF16 = jnp.bfloat16

D_MODEL = 1024
HEAD_DIM = 64
H_DIFF = 4
H_DIL = 4
DIL_CONFIGS = ((128, 1), (512, 4), (2048, 16))
N_DIL = len(DIL_CONFIGS)
DIFF_QK = H_DIFF * 2 * HEAD_DIM
DIFF_V = H_DIFF * 2 * HEAD_DIM
DIL_W = N_DIL * H_DIL * HEAD_DIM
GRP_W = H_DIL * HEAD_DIM
GRP_COLS = 3 * GRP_W
NAT_COLS = 2 * DIFF_QK + DIFF_V + GRP_COLS
GATE_COLS = 2 * D_MODEL
N_BUCKETS = 32
MAX_DISTANCE = 128
N_GROUPS = 4
E_PER_GROUP = 4
N_EXPERTS = N_GROUPS * E_PER_GROUP
D_EXPERT = 256
EPS = 1e-6
NEG_INF = -1e30
LAM_INIT = 0.8 - 0.6 * math.exp(-0.3 * 0)
LOG2E = math.log2(math.e)
LN2 = math.log(2.0)
QK_SCALE = HEAD_DIM ** -0.5

LANES = 128
SUBLANES = 8
VMEM_LIMIT = 56 * 1024 * 1024

TM_PROJ = 512
TQ_DIFF = 1024
TK_DIFF = 256
DIFF_SLOTS = 4
DEN_ROWS = 16
DIL_QB = 256
DIL_STEP_ROWS = 2048
SC_WINDOW = 128
SC_COLS = 256
TM_MERGE = 512
SUB_MERGE = 256
TM_MOE = 512
TM_FINAL = 1024
ROUTE_LANES = 128
EXP_LANE0 = N_GROUPS


def _const_spec(shape):
    nd = len(shape)
    return pl.BlockSpec(shape, lambda *_: (0,) * nd, pipeline_mode=pl.Buffered(1))


def _params(sem):
    return pltpu.CompilerParams(dimension_semantics=sem, vmem_limit_bytes=VMEM_LIMIT)


def _rel_bucket(rel):
    half = N_BUCKETS // 2
    max_exact = half // 2
    ret = (rel > 0).astype(jnp.int32) * half
    n = jnp.abs(rel)
    nf = jnp.maximum(n, 1).astype(F32)
    large = max_exact + (jnp.log(nf / max_exact) / math.log(MAX_DISTANCE / max_exact)
                         * (half - max_exact)).astype(jnp.int32)
    large = jnp.minimum(large, half - 1)
    return ret + jnp.where(n < max_exact, n, large)


def _bias_of_rel(table, rel):
    bucket = _rel_bucket(rel)[None]
    t = table.astype(F32)
    col = (t.shape[1],) + (1,) * rel.ndim
    out = jnp.zeros((t.shape[1],) + rel.shape, F32)
    for k in range(N_BUCKETS):
        out = jnp.where(bucket == k, t[k].reshape(col), out)
    return out


def _chunks(total, width):
    out, c = [], 0
    while c < total:
        w = min(width, total - c)
        out.append((c, w))
        c += w
    return out


def _norm_proj_kernel(x_ref, g_ref, wn_ref, wg_ref, w1_ref, w2_ref,
                      on_ref, og_ref, o1_ref, o2_ref, hs_ref, *, tm):
    x = x_ref[0]
    ms = jnp.mean(x * x, axis=-1, keepdims=True)
    hf = x * lax.rsqrt(ms + EPS) * g_ref[...]
    for c in range(D_MODEL // LANES):
        hs_ref[c] = hf[:, c * LANES:(c + 1) * LANES]
    h = hf.astype(BF16)
    for c0, cw in _chunks(NAT_COLS, 512):
        on_ref[0, :, c0:c0 + cw] = jnp.dot(h, wn_ref[:, c0:c0 + cw],
                                           preferred_element_type=F32).astype(BF16)
    for c0, cw in _chunks(GATE_COLS, 512):
        og_ref[0, :, c0:c0 + cw] = jnp.dot(h, wg_ref[:, c0:c0 + cw],
                                           preferred_element_type=F32).astype(BF16)
    for w_ref, o_ref, d in ((w1_ref, o1_ref, DIL_CONFIGS[1][1]), (w2_ref, o2_ref, DIL_CONFIGS[2][1])):
        rows = tm // d
        hp = jnp.concatenate(
            [jnp.concatenate([hs_ref[c, pl.ds(r, rows, stride=d), :] for r in range(d)], axis=0)
             for c in range(D_MODEL // LANES)], axis=1).astype(BF16)
        for c0, cw in _chunks(GRP_COLS, 256):
            res = jnp.dot(hp, w_ref[:, c0:c0 + cw], preferred_element_type=F32).astype(BF16)
            o_ref[0, :, :, c0:c0 + cw] = res.reshape(d, rows, cw)


def _norm_proj(x, g, w_nat, w_gate, w_d1, w_d2):
    b, s, _ = x.shape
    tm = TM_PROJ
    d1, d2 = DIL_CONFIGS[1][1], DIL_CONFIGS[2][1]
    return pl.pallas_call(
        functools.partial(_norm_proj_kernel, tm=tm),
        grid=(b, s // tm),
        in_specs=[
            pl.BlockSpec((1, tm, D_MODEL), lambda bi, i: (bi, i, 0)),
            _const_spec((1, D_MODEL)),
            _const_spec((D_MODEL, NAT_COLS)),
            _const_spec((D_MODEL, GATE_COLS)),
            _const_spec((D_MODEL, GRP_COLS)),
            _const_spec((D_MODEL, GRP_COLS)),
        ],
        out_specs=[
            pl.BlockSpec((1, tm, NAT_COLS), lambda bi, i: (bi, i, 0)),
            pl.BlockSpec((1, tm, GATE_COLS), lambda bi, i: (bi, i, 0)),
            pl.BlockSpec((1, d1, tm // d1, GRP_COLS), lambda bi, i: (bi, 0, i, 0)),
            pl.BlockSpec((1, d2, tm // d2, GRP_COLS), lambda bi, i: (bi, 0, i, 0)),
        ],
        out_shape=[
            jax.ShapeDtypeStruct((b, s, NAT_COLS), BF16),
            jax.ShapeDtypeStruct((b, s, GATE_COLS), BF16),
            jax.ShapeDtypeStruct((b, d1, s // d1, GRP_COLS), BF16),
            jax.ShapeDtypeStruct((b, d2, s // d2, GRP_COLS), BF16),
        ],
        scratch_shapes=[pltpu.VMEM((D_MODEL // LANES, tm, LANES), F32)],
        compiler_params=_params(("parallel", "parallel")),
        name="norm_proj",
    )(x, g, w_nat, w_gate, w_d1, w_d2)


def _diff_bias_tiles(rel_bias, tq, tk):
    ratio, w = tq // tk, tq + tk
    t = jnp.arange(-2, ratio + 2)[:, None]
    m = jnp.arange(w)[None, :]
    rel = t * tk + jnp.where(m < tq, -m, w - m)
    tiles = _bias_of_rel(rel_bias[:, :H_DIFF], rel) * LOG2E
    far = _bias_of_rel(rel_bias[:, :H_DIFF], jnp.asarray([-MAX_DISTANCE, MAX_DISTANCE])) * LOG2E
    return tiles, far


def _diff_attn_kernel(lam_ref, far_ref, q_ref, k_ref, v_ref, brow_ref, g_ref, o_ref,
                      vt_ref, st_ref, p_ref, acc_ref, bias_ref, *, tq, tk, nkb):
    h = pl.program_id(0)
    qi = pl.program_id(2)
    ratio = tq // tk
    n_band = ratio + 2
    vchunk = min(512, nkb * tk)

    @pl.when((pl.program_id(1) == 0) & (qi == 0))
    def _():
        for t in range(ratio + 4):
            rows = jnp.broadcast_to(brow_ref[0, t:t + 1, :], (tk, tq + tk))
            bias_ref[t] = pltpu.roll(rows, 0, 1, stride=1, stride_axis=0)[:, :tq]

    @pl.when(qi == 0)
    def _():
        for c in range(nkb * tk // vchunk):
            cs = slice(c * vchunk, (c + 1) * vchunk)
            vt_ref[:LANES, cs] = v_ref[0, cs, :].astype(F32).T.astype(BF16)
        vt_ref[LANES:, :] = jnp.ones((DEN_ROWS, nkb * tk), BF16)

    qt = q_ref[0].astype(F32).T
    first = lax.broadcasted_iota(jnp.int32, (LANES, tq), 0) < HEAD_DIM
    qts = (jnp.where(first, qt, 0.0).astype(BF16), jnp.where(first, 0.0, qt).astype(BF16))
    kb0 = qi * ratio

    def key_block(j):
        kb = kb0 - 1 + j
        return jnp.where(kb < 0, kb + nkb, jnp.where(kb >= nkb, kb - nkb, kb))

    def scores(j, slot, band):
        kb = key_block(j)
        ks = pl.multiple_of(kb * tk, tk)
        k = k_ref[0, pl.ds(ks, tk), :]
        if band:
            bt = bias_ref[jnp.clip(kb - kb0, -2, ratio + 1) + 2]
        mx = []
        for mp in range(2):
            st = jnp.dot(k, qts[mp], preferred_element_type=F32)
            if band:
                st = st + bt
            st_ref[slot, mp] = st
            mx.append(jnp.max(st, axis=0, keepdims=True))
        return tuple(mx)

    def softmax(j, slot, stats, mx, band):
        if not band:
            kb = key_block(j)
            far_c = jnp.where(kb < kb0, far_ref[h, 0], far_ref[h, 1])
        out, alphas = [], []
        for mp in range(2):
            m = stats[mp]
            if band:
                m_new = jnp.maximum(m, mx[mp])
                shift = m_new
            else:
                m_new = jnp.maximum(m, mx[mp] + far_c)
                shift = m_new - far_c
            p_ref[slot, mp] = jnp.exp2((st_ref[slot, mp] - shift).astype(BF16))
            out.append(m_new)
            alphas.append(jnp.exp2(m - m_new))
        return tuple(out), tuple(alphas)

    def values(j, slot, alphas):
        ks = pl.multiple_of(key_block(j) * tk, tk)
        vt = vt_ref[:, pl.ds(ks, tk)]
        for mp in range(2):
            acc_ref[mp] = alphas[mp] * acc_ref[mp] + jnp.dot(vt, p_ref[slot, mp],
                                                             preferred_element_type=F32)

    acc_ref[...] = jnp.zeros_like(acc_ref)
    mx = scores(0, 0, True)
    stats = (jnp.full((1, tq), -jnp.inf, F32),) * 2
    alphas = None
    for j in range(nkb):
        if j + 1 < nkb:
            mx_next = scores(j + 1, (j + 1) % DIFF_SLOTS, j + 1 < n_band)
        if j >= 1:
            values(j - 1, (j - 1) % DIFF_SLOTS, alphas)
        stats, alphas = softmax(j, j % DIFF_SLOTS, stats, mx, j < n_band)
        mx = mx_next
    values(nkb - 1, (nkb - 1) % DIFF_SLOTS, alphas)
    l0, l1 = acc_ref[0, LANES:LANES + 1, :], acc_ref[1, LANES:LANES + 1, :]

    lp = lam_ref[...]
    lam = (jnp.exp(jnp.sum(lp[0:1] * lp[1:2], axis=-1, keepdims=True))
           - jnp.exp(jnp.sum(lp[2:3] * lp[3:4], axis=-1, keepdims=True)) + LAM_INIT)
    ot = acc_ref[0, :LANES, :] / l0 - lam * (acc_ref[1, :LANES, :] / l1)
    ms = jnp.mean(ot * ot, axis=0, keepdims=True)
    y = (ot * lax.rsqrt(ms + EPS)).T * g_ref[...] * (1.0 - LAM_INIT)
    o_ref[0] = y.astype(BF16)


def _diff_attn(nat, bias_tiles, far, lam_params, subln_g):
    b, s, _ = nat.shape
    tq, tk = min(TQ_DIFF, s), min(TK_DIFF, s)
    nt = bias_tiles.shape[1]
    ratio, nkb = tq // tk, s // tk
    assert tq % tk == 0 and tk >= MAX_DISTANCE and nkb >= ratio + 2
    assert DIFF_SLOTS >= 3
    kern = functools.partial(_diff_attn_kernel, tq=tq, tk=tk, nkb=s // tk)
    return pl.pallas_call(
        kern,
        grid=(H_DIFF, b, s // tq),
        in_specs=[
            _const_spec((4, HEAD_DIM)),
            pl.BlockSpec(memory_space=pltpu.SMEM),
            pl.BlockSpec((1, tq, LANES), lambda h, bi, qi: (bi, qi, h)),
            pl.BlockSpec((1, s, LANES), lambda h, bi, qi: (bi, 0, H_DIFF + h)),
            pl.BlockSpec((1, s, LANES), lambda h, bi, qi: (bi, 0, 2 * H_DIFF + h)),
            pl.BlockSpec((1, nt, tq + tk), lambda h, bi, qi: (h, 0, 0)),
            _const_spec((1, 2 * HEAD_DIM)),
        ],
        out_specs=pl.BlockSpec((1, tq, LANES), lambda h, bi, qi: (bi, qi, h)),
        out_shape=jax.ShapeDtypeStruct((b, s, DIFF_V), BF16),
        scratch_shapes=[
            pltpu.VMEM((LANES + DEN_ROWS, s), BF16),
            pltpu.VMEM((DIFF_SLOTS, 2, tk, tq), F32),
            pltpu.VMEM((DIFF_SLOTS, 2, tk, tq), BF16),
            pltpu.VMEM((2, LANES + DEN_ROWS, tq), F32),
            pltpu.VMEM((nt, tk, tq), F32),
        ],
        compiler_params=_params(("arbitrary", "arbitrary", "arbitrary")),
        name="diff_attn",
    )(lam_params, far, nat, nat, nat, bias_tiles, subln_g)


def _dil_geometry(s, window, dilation):
    l = s // dilation
    w = window // (2 * dilation)
    qb = min(DIL_QB, l)
    kw = min(qb + 2 * LANES, l)
    assert w <= LANES and l % qb == 0 and l % LANES == 0
    return l, w, qb, kw, l // qb


def _dil_bias_tiles(rel_bias, g, s, window, dilation):
    l, w, qb, kw, nb = _dil_geometry(s, window, dilation)
    c0 = H_DIFF + g * H_DIL
    shifts = []
    for n in (0, min(1, nb - 1), nb - 1):
        l0 = n * qb
        shifts.append(min(max(l0 - LANES, 0), l - kw) - l0)
    j = jnp.arange(kw)[None, :, None]
    i = jnp.arange(qb)[None, None, :]
    off = jnp.asarray(shifts)[:, None, None] + j - i
    b = _bias_of_rel(rel_bias[:, c0:c0 + H_DIL], off * dilation) * LOG2E
    b = jnp.where((jnp.abs(off) <= w)[None], b, NEG_INF)
    return jnp.stack([jnp.concatenate([b[2 * p], b[2 * p + 1]], axis=-1) for p in range(H_DIL // 2)], axis=1)


def _dil_attn_kernel(q_ref, k_ref, v_ref, bias_ref, o_ref, lse_ref, qt_ref, vt_ref, st_ref,
                     *, l, qb, kw, nb, cps):
    chunk = min(512, l)
    first = lax.broadcasted_iota(jnp.int32, (LANES, qb), 0) < HEAD_DIM

    def transposes(c):
        for pair in range(H_DIL // 2):
            cs = slice(pair * LANES, (pair + 1) * LANES)
            for i in range(l // chunk):
                rs = slice(i * chunk, (i + 1) * chunk)
                qt_ref[c, pair, :, rs] = q_ref[0, c, rs, cs].astype(F32).T.astype(BF16)
                vt_ref[c, pair, :LANES, rs] = v_ref[0, c, rs, cs].astype(F32).T.astype(BF16)
            vt_ref[c, pair, LANES:, :] = jnp.ones((DEN_ROWS, l), BF16)

    def window(n):
        l0 = n * qb
        return l0, min(max(l0 - LANES, 0), l - kw)

    def scores(c, n, slot):
        l0, ws = window(n)
        for pair in range(H_DIL // 2):
            qt = qt_ref[c, pair, :, l0:l0 + qb]
            q2 = jnp.concatenate([jnp.where(first, qt, jnp.zeros_like(qt)),
                                  jnp.where(first, jnp.zeros_like(qt), qt)], axis=1)
            kp = k_ref[0, c, ws:ws + kw, pair * LANES:(pair + 1) * LANES]
            st_ref[c, slot, pair] = jnp.dot(kp, q2, preferred_element_type=F32)

    def finish(c, n, slot):
        l0, ws = window(n)
        t = 0 if n == 0 else (2 if n == nb - 1 else 1)
        for pair in range(H_DIL // 2):
            cs = slice(pair * LANES, (pair + 1) * LANES)
            st = st_ref[c, slot, pair] + bias_ref[t, pair]
            mx = jnp.max(st, axis=0, keepdims=True)
            p = jnp.exp2(st - mx).astype(BF16)
            acc = jnp.dot(vt_ref[c, pair, :, ws:ws + kw], p, preferred_element_type=F32)
            den = acc[LANES:LANES + 1, :]
            ot = acc[:LANES, :] / den
            lse = mx * LN2 + jnp.log(den)
            o_t = jnp.where(first, ot[:, :qb], ot[:, qb:])
            lse_t = jnp.where(first, lse[:, :qb], lse[:, qb:])
            o_ref[0, c, l0:l0 + qb, cs] = o_t.T.astype(BF16)
            lse_ref[0, c, l0:l0 + qb, cs] = lse_t.T

    for c in range(cps):
        transposes(c)
    for c in range(cps):
        scores(c, 0, 0)
    for c in range(cps):
        for n in range(nb):
            if n + 1 < nb:
                scores(c, n + 1, (n + 1) % 2)
            finish(c, n, n % 2)


def _dil_attn(arr, col0, bias_tiles, g, window, dilation):
    b, d, l, _ = arr.shape
    _, w, qb, kw, nb = _dil_geometry(l * d, window, dilation)
    cps = max(1, min(d, DIL_STEP_ROWS // l))
    kern = functools.partial(_dil_attn_kernel, l=l, qb=qb, kw=kw, nb=nb, cps=cps)
    return pl.pallas_call(
        kern,
        grid=(b, d // cps),
        in_specs=[
            pl.BlockSpec((1, cps, l, GRP_W), lambda bi, r: (bi, r, 0, col0)),
            pl.BlockSpec((1, cps, l, GRP_W), lambda bi, r: (bi, r, 0, col0 + 1)),
            pl.BlockSpec((1, cps, l, GRP_W), lambda bi, r: (bi, r, 0, col0 + 2)),
            _const_spec((3, H_DIL // 2, kw, 2 * qb)),
        ],
        out_specs=[
            pl.BlockSpec((1, cps, l, GRP_W), lambda bi, r: (bi, r, 0, 0)),
            pl.BlockSpec((1, cps, l, GRP_W), lambda bi, r: (bi, r, 0, 0)),
        ],
        out_shape=[
            jax.ShapeDtypeStruct((b, d, l, GRP_W), BF16),
            jax.ShapeDtypeStruct((b, d, l, GRP_W), F32),
        ],
        scratch_shapes=[
            pltpu.VMEM((cps, H_DIL // 2, LANES, l), BF16),
            pltpu.VMEM((cps, H_DIL // 2, LANES + DEN_ROWS, l), BF16),
            pltpu.VMEM((cps, 2, H_DIL // 2, kw, 2 * qb), F32),
        ],
        compiler_params=_params(("parallel", "parallel")),
        name=f"dil_attn_g{g}",
    )(arr, arr, arr, bias_tiles)


def _pack_words(a, b):
    ua = lax.bitcast_convert_type(a.astype(BF16).astype(F32), jnp.uint32)
    ub = lax.bitcast_convert_type(b.astype(BF16).astype(F32), jnp.uint32)
    return (ua >> 16) | (ub & jnp.uint32(0xFFFF0000))


def _unpack_words(w):
    return (lax.bitcast_convert_type(w << 16, F32),
            lax.bitcast_convert_type(w & jnp.uint32(0xFFFF0000), F32))


def _route(lg):
    lane = lax.broadcasted_iota(jnp.int32, lg.shape, 1)
    big = jnp.int32(ROUTE_LANES)
    gl = jnp.where(lane < N_GROUPS, lg, -jnp.inf)
    gmax = jnp.max(gl, axis=-1, keepdims=True)
    gi = jnp.min(jnp.where(gl == gmax, lane, big), axis=-1, keepdims=True)
    gw = 1.0 / jnp.sum(jnp.exp(gl - gmax), axis=-1, keepdims=True)
    e_lo = EXP_LANE0 + E_PER_GROUP * gi
    es = jnp.where((lane >= e_lo) & (lane < e_lo + E_PER_GROUP), lg, -jnp.inf)
    v1 = jnp.max(es, axis=-1, keepdims=True)
    i1 = jnp.min(jnp.where(es == v1, lane, big), axis=-1, keepdims=True)
    es2 = jnp.where(lane == i1, -jnp.inf, es)
    v2 = jnp.max(es2, axis=-1, keepdims=True)
    i2 = jnp.min(jnp.where(es2 == v2, lane, big), axis=-1, keepdims=True)
    t = jnp.exp(v2 - v1)
    w1 = gw / (1.0 + t)
    w2 = w1 * t
    return gi, jnp.where(lane == i1, w1, 0.0) + jnp.where(lane == i2, w2, 0.0)


def _merge_route_kernel(x_ref, oa_ref, o0_ref, l0_ref, o1_ref, l1_ref, o2_ref, l2_ref, gt_ref,
                        wa_ref, wb_ref, wo_ref, nf_ref, wr_ref, x1_ref, hw_ref, gid_ref,
                        so1_ref, sl1_ref, so2_ref, sl2_ref, *, tm):
    for src, dst, d in ((o1_ref, so1_ref, DIL_CONFIGS[1][1]), (l1_ref, sl1_ref, DIL_CONFIGS[1][1]),
                        (o2_ref, so2_ref, DIL_CONFIGS[2][1]), (l2_ref, sl2_ref, DIL_CONFIGS[2][1])):
        for r in range(d):
            for c in range(GRP_W // LANES):
                dst[c, pl.ds(r, tm // d, stride=d), :] = src[0, r, :, c * LANES:(c + 1) * LANES].astype(F32)
    subs = [slice(i * SUB_MERGE, (i + 1) * SUB_MERGE) for i in range(tm // SUB_MERGE)]
    unsplit = lambda ref, rs: jnp.concatenate([ref[c, rs, :] for c in range(GRP_W // LANES)], axis=1)

    def dil_mix(rs):
        l0, l1, l2 = l0_ref[0, 0, rs, :], unsplit(sl1_ref, rs), unsplit(sl2_ref, rs)
        mx = jnp.maximum(jnp.maximum(l0, l1), l2)
        e0, e1, e2 = jnp.exp(l0 - mx), jnp.exp(l1 - mx), jnp.exp(l2 - mx)
        ob = (e0 * o0_ref[0, 0, rs, :].astype(F32) + e1 * unsplit(so1_ref, rs)
              + e2 * unsplit(so2_ref, rs)) / (e0 + e1 + e2)
        return ob.astype(BF16)

    obs = [dil_mix(rs) for rs in subs]
    pas = [jnp.dot(oa_ref[0, rs, :], wa_ref[...], preferred_element_type=F32) for rs in subs]
    pbs = [jnp.dot(ob, wb_ref[...], preferred_element_type=F32) for ob in obs]

    def gate(rs, pa, pb):
        ga = jax.nn.sigmoid(gt_ref[0, rs, :D_MODEL].astype(F32))
        gb = jax.nn.sigmoid(gt_ref[0, rs, D_MODEL:].astype(F32))
        return (ga * pa + gb * pb).astype(BF16)

    mgs = [gate(rs, pa, pb) for rs, pa, pb in zip(subs, pas, pbs)]
    x1s = [x_ref[0, rs, :] + jnp.dot(mg, wo_ref[...], preferred_element_type=F32)
           for rs, mg in zip(subs, mgs)]

    def ffn_norm(rs, x1):
        x1_ref[0, rs, :] = x1
        ms = jnp.mean(x1 * x1, axis=-1, keepdims=True)
        h2f = x1 * lax.rsqrt(ms + EPS) * nf_ref[...]
        hw_ref[0, rs, :] = _pack_words(h2f[:, :D_MODEL // 2], h2f[:, D_MODEL // 2:])
        return h2f.astype(BF16)

    h2s = [ffn_norm(rs, x1) for rs, x1 in zip(subs, x1s)]
    lgs = [jnp.dot(h2, wr_ref[...], preferred_element_type=F32) for h2 in h2s]
    for rs, lg in zip(subs, lgs):
        gi, _ = _route(lg)
        git = jnp.broadcast_to(gi.astype(F32), lg.shape).T
        gid_ref[0, 0, :, rs] = git[:SUBLANES, :].astype(jnp.int32)


def _merge_route(x, oa, dil, gates, wa, wb, wo, nf, wr):
    b, s, _ = x.shape
    tm = TM_MERGE
    d1, d2 = DIL_CONFIGS[1][1], DIL_CONFIGS[2][1]
    tok = lambda bi, i: (bi, i, 0)
    cls = lambda bi, i: (bi, 0, i, 0)
    (o0, s0), (o1, s1), (o2, s2) = dil
    return pl.pallas_call(
        functools.partial(_merge_route_kernel, tm=tm),
        grid=(b, s // tm),
        in_specs=[
            pl.BlockSpec((1, tm, D_MODEL), tok),
            pl.BlockSpec((1, tm, DIFF_V), tok),
            pl.BlockSpec((1, 1, tm, GRP_W), cls), pl.BlockSpec((1, 1, tm, GRP_W), cls),
            pl.BlockSpec((1, d1, tm // d1, GRP_W), cls), pl.BlockSpec((1, d1, tm // d1, GRP_W), cls),
            pl.BlockSpec((1, d2, tm // d2, GRP_W), cls), pl.BlockSpec((1, d2, tm // d2, GRP_W), cls),
            pl.BlockSpec((1, tm, GATE_COLS), tok),
            _const_spec((DIFF_V, D_MODEL)),
            _const_spec((GRP_W, D_MODEL)),
            _const_spec((D_MODEL, D_MODEL)),
            _const_spec((1, D_MODEL)),
            _const_spec((D_MODEL, ROUTE_LANES)),
        ],
        out_specs=[
            pl.BlockSpec((1, tm, D_MODEL), tok),
            pl.BlockSpec((1, tm, D_MODEL // 2), tok),
            pl.BlockSpec((1, 1, SUBLANES, tm), lambda bi, i: (bi, i, 0, 0)),
        ],
        out_shape=[
            jax.ShapeDtypeStruct((b, s, D_MODEL), F32),
            jax.ShapeDtypeStruct((b, s, D_MODEL // 2), jnp.uint32),
            jax.ShapeDtypeStruct((b, s // tm, SUBLANES, tm), jnp.int32),
        ],
        scratch_shapes=[pltpu.VMEM((GRP_W // LANES, tm, LANES), F32) for _ in range(4)],
        compiler_params=_params(("parallel", "parallel")),
        name="merge_route",
    )(x, oa, o0, s0, o1, s1, o2, s2, gates, wa, wb, wo, nf, wr)


def _dispatch_plan_kernel(gid_ref, dest_ref, tgrp_ref, *, rows, cols):
    gid = gid_ref[...]
    ci = lax.broadcasted_iota(jnp.int32, (cols, cols), 0)
    cj = lax.broadcasted_iota(jnp.int32, (cols, cols), 1)
    upto = (ci <= cj).astype(BF16)
    ri = lax.broadcasted_iota(jnp.int32, (rows, rows), 0)
    rj = lax.broadcasted_iota(jnp.int32, (rows, rows), 1)
    above = (rj < ri).astype(F32)
    seg = jnp.zeros((1, 1), F32)
    dest = jnp.zeros((rows, cols), F32)
    ends = []
    for g in range(N_GROUPS):
        mem = (gid == g).astype(F32)
        pref = jnp.dot(mem.astype(BF16), upto, preferred_element_type=F32)
        tot = pref[:, cols - 1:cols]
        before = jnp.dot(above, jnp.broadcast_to(tot, (rows, LANES)), preferred_element_type=F32,
                         precision=lax.Precision.HIGHEST)[:, :1]
        dest = dest + mem * (seg + before + pref - 1.0)
        cnt = jnp.sum(tot, axis=0, keepdims=True)
        seg = seg + jnp.ceil(cnt / TM_MOE) * TM_MOE
        ends.append(seg)
    dest_ref[...] = dest.astype(jnp.int32)
    tile = (lax.broadcasted_iota(jnp.int32, (SUBLANES, LANES), 0) * LANES
            + lax.broadcasted_iota(jnp.int32, (SUBLANES, LANES), 1))
    start = (tile * TM_MOE).astype(F32)
    tgrp = jnp.zeros((SUBLANES, LANES), jnp.int32)
    for g in range(N_GROUPS - 1):
        tgrp = tgrp + (start >= ends[g]).astype(jnp.int32)
    tgrp_ref[...] = tgrp


def _dispatch_plan(gid):
    rows, cols = gid.shape
    assert rows % SUBLANES == 0 and rows * cols // TM_MOE + N_GROUPS <= SUBLANES * LANES
    return pl.pallas_call(
        functools.partial(_dispatch_plan_kernel, rows=rows, cols=cols),
        out_shape=[jax.ShapeDtypeStruct((rows, cols), jnp.int32),
                   jax.ShapeDtypeStruct((SUBLANES, LANES), jnp.int32)],
        compiler_params=pltpu.CompilerParams(vmem_limit_bytes=VMEM_LIMIT),
        name="dispatch_plan",
    )(gid)


def _moe_group_kernel(tgrp_ref, hw_ref, wr_ref, wg_ref, wu_ref, wd_ref, ow_ref, he_ref):
    g = tgrp_ref[pl.program_id(0)]
    lo, hi = _unpack_words(hw_ref[...])
    h2 = jnp.concatenate([lo, hi], axis=1).astype(BF16)
    _, dw = _route(jnp.dot(h2, wr_ref[...], preferred_element_type=F32))
    lane = lax.broadcasted_iota(jnp.int32, dw.shape, 1)
    for e in range(E_PER_GROUP):
        we = jnp.sum(jnp.where(lane == EXP_LANE0 + E_PER_GROUP * g + e, dw, 0.0), axis=-1, keepdims=True)
        gp = jnp.dot(h2, wg_ref[0, e], preferred_element_type=F32)
        up = jnp.dot(h2, wu_ref[0, e], preferred_element_type=F32)
        he_ref[:, e * D_EXPERT:(e + 1) * D_EXPERT] = (jax.nn.silu(gp) * up * we).astype(BF16)
    moe = jnp.dot(he_ref[...], wd_ref[0], preferred_element_type=F32)
    ow_ref[...] = _pack_words(moe[:, :D_MODEL // 2], moe[:, D_MODEL // 2:])


def _moe_groups(tgrp, hw_sorted, wr, wg, wu, wd):
    m = hw_sorted.shape[0]
    tm = TM_MOE
    row = lambda i, t: (i, 0)
    grp = lambda i, t: (t[i], 0, 0, 0)
    return pl.pallas_call(
        _moe_group_kernel,
        grid_spec=pltpu.PrefetchScalarGridSpec(
            num_scalar_prefetch=1,
            grid=(m // tm,),
            in_specs=[
                pl.BlockSpec((tm, D_MODEL // 2), row),
                pl.BlockSpec((D_MODEL, ROUTE_LANES), lambda i, t: (0, 0)),
                pl.BlockSpec((1, E_PER_GROUP, D_MODEL, D_EXPERT), grp),
                pl.BlockSpec((1, E_PER_GROUP, D_MODEL, D_EXPERT), grp),
                pl.BlockSpec((1, E_PER_GROUP * D_EXPERT, D_MODEL), lambda i, t: (t[i], 0, 0)),
            ],
            out_specs=pl.BlockSpec((tm, D_MODEL // 2), row),
            scratch_shapes=[pltpu.VMEM((tm, E_PER_GROUP * D_EXPERT), BF16)],
        ),
        out_shape=jax.ShapeDtypeStruct((m, D_MODEL // 2), jnp.uint32),
        compiler_params=_params(("arbitrary",)),
        name="moe_groups",
    )(tgrp, hw_sorted, wr, wg, wu, wd)


def _final_norm_kernel(x1_ref, yw_ref, nf_ref, y_ref):
    lo, hi = _unpack_words(yw_ref[...])
    x2 = x1_ref[...] + jnp.concatenate([lo, hi], axis=1)
    ms = jnp.mean(x2 * x2, axis=-1, keepdims=True)
    y_ref[...] = x2 * lax.rsqrt(ms + EPS) * nf_ref[...]


def _final_norm(x1, yw, nf):
    n = x1.shape[0]
    tm = TM_FINAL
    row = lambda i: (i, 0)
    return pl.pallas_call(
        _final_norm_kernel,
        grid=(n // tm,),
        in_specs=[pl.BlockSpec((tm, D_MODEL), row), pl.BlockSpec((tm, D_MODEL // 2), row),
                  _const_spec((1, D_MODEL))],
        out_specs=pl.BlockSpec((tm, D_MODEL), row),
        out_shape=jax.ShapeDtypeStruct((n, D_MODEL), F32),
        compiler_params=_params(("parallel",)),
        name="final_norm",
    )(x1, yw, nf)


def _sc_mesh():
    return plsc.VectorSubcoreMesh(core_axis_name="core", subcore_axis_name="subcore")


def _sc_scatter_rows(x, idx, m):
    n, d = x.shape

    @pl.kernel(out_type=jax.ShapeDtypeStruct((m, d), x.dtype), mesh=_sc_mesh(), scratch_types=[])
    def scatter(x_hbm, i_hbm, o_hbm):
        for c in range(d // SC_COLS):
            cols = pl.ds(c * SC_COLS, SC_COLS)
            o_cols = o_hbm.at[:, cols]

            def body(x_vmem, i_vmem, o_cols=o_cols):
                pltpu.sync_copy(x_vmem, o_cols.at[i_vmem.at[0]])

            pltpu.emit_pipeline(
                body,
                grid=(n // SC_WINDOW,),
                in_specs=[pl.BlockSpec((SC_WINDOW, SC_COLS), lambda i, c=c: (i, c)),
                          pl.BlockSpec((1, SC_WINDOW), lambda i: (0, i))],
                out_specs=[],
                core_axis_name=("core", "subcore"),
                dimension_semantics=(pltpu.PARALLEL,),
            )(x_hbm, i_hbm)

    return scatter(x, idx.reshape(1, n))


def _sc_gather_rows(x, idx):
    n = idx.shape[0]
    d = x.shape[1]

    @pl.kernel(out_type=jax.ShapeDtypeStruct((n, d), x.dtype), mesh=_sc_mesh(), scratch_types=[])
    def gather(x_hbm, i_hbm, o_hbm):
        for c in range(d // SC_COLS):
            x_cols = x_hbm.at[:, pl.ds(c * SC_COLS, SC_COLS)]

            def body(i_vmem, o_vmem, x_cols=x_cols):
                pltpu.sync_copy(x_cols.at[i_vmem.at[0]], o_vmem)

            pltpu.emit_pipeline(
                body,
                grid=(n // SC_WINDOW,),
                in_specs=[pl.BlockSpec((1, SC_WINDOW), lambda i: (0, i))],
                out_specs=[pl.BlockSpec((SC_WINDOW, SC_COLS), lambda i, c=c: (i, c))],
                core_axis_name=("core", "subcore"),
                dimension_semantics=(pltpu.PARALLEL,),
            )(i_hbm, o_hbm)

    return gather(x, idx.reshape(1, n))


def _prepare_weights(rel_bias, norm_mix, w_in, lam_q1, lam_k1, lam_q2, lam_k2, subln_g,
                     w_branch_a, w_branch_b, w_out, norm_ffn, w_group, w_router,
                     w_exp_gate, w_exp_up, w_exp_down, norm_final):
    w = w_in[0]
    c = QK_SCALE * LOG2E
    o = 0
    qa, o = w[:, o:o + DIFF_QK] * c, o + DIFF_QK
    ka, o = w[:, o:o + DIFF_QK], o + DIFF_QK
    va, o = w[:, o:o + DIFF_V], o + DIFF_V
    qb, o = w[:, o:o + DIL_W] * c, o + DIL_W
    kb, o = w[:, o:o + DIL_W], o + DIL_W
    vb, o = w[:, o:o + DIL_W], o + DIL_W
    w_gate = w[:, o:].astype(BF16)
    grp = lambda g: [t[:, g * GRP_W:(g + 1) * GRP_W] for t in (qb, kb, vb)]
    w_nat = jnp.concatenate([qa, ka, va] + grp(0), axis=1).astype(BF16)
    w_d1 = jnp.concatenate(grp(1), axis=1).astype(BF16)
    w_d2 = jnp.concatenate(grp(2), axis=1).astype(BF16)
    wr = jnp.concatenate([w_group[0], w_router[0].transpose(1, 0, 2).reshape(D_MODEL, N_EXPERTS)], axis=1)
    wr = jnp.pad(wr, ((0, 0), (0, ROUTE_LANES - wr.shape[1]))).astype(BF16)
    return dict(
        rel_bias=rel_bias.astype(F32),
        norm_mix=norm_mix[0][None].astype(F32),
        w_nat=w_nat, w_gate=w_gate, w_d1=w_d1, w_d2=w_d2,
        lam=jnp.stack([lam_q1[0], lam_k1[0], lam_q2[0], lam_k2[0]]).astype(F32),
        subln_g=subln_g[0][None].astype(F32),
        wa=w_branch_a[0].astype(BF16), wb=w_branch_b[0].astype(BF16), wo=w_out[0].astype(BF16),
        norm_ffn=norm_ffn[0][None].astype(F32), wr=wr,
        wg=w_exp_gate[0].reshape(N_GROUPS, E_PER_GROUP, D_MODEL, D_EXPERT).astype(BF16),
        wu=w_exp_up[0].reshape(N_GROUPS, E_PER_GROUP, D_MODEL, D_EXPERT).astype(BF16),
        wd=w_exp_down[0].reshape(N_GROUPS, E_PER_GROUP * D_EXPERT, D_MODEL).astype(BF16),
        norm_final=norm_final[None].astype(F32),
    )


def _encoder(x, p):
    b, s, d = x.shape
    n = b * s
    nat, gates, dil1, dil2 = _norm_proj(x, p["norm_mix"], p["w_nat"], p["w_gate"], p["w_d1"], p["w_d2"])
    tiles, far = _diff_bias_tiles(p["rel_bias"], min(TQ_DIFF, s), min(TK_DIFF, s))
    oa = _diff_attn(nat, tiles, far, p["lam"], p["subln_g"])
    srcs = ((nat.reshape(b, 1, s, NAT_COLS), (2 * DIFF_QK + DIFF_V) // GRP_W), (dil1, 0), (dil2, 0))
    dil = []
    for g, (win, dilation) in enumerate(DIL_CONFIGS):
        tiles = _dil_bias_tiles(p["rel_bias"], g, s, win, dilation)
        dil.append(_dil_attn(srcs[g][0], srcs[g][1], tiles, g, win, dilation))
    x1, hw, gid = _merge_route(x, oa, dil, gates, p["wa"], p["wb"], p["wo"], p["norm_ffn"], p["wr"])
    dest, tgrp = _dispatch_plan(gid[:, :, 0, :].reshape(n // TM_MERGE, TM_MERGE))
    dest = dest.reshape(n)
    n_tiles = n // TM_MOE + N_GROUPS
    hw_sorted = _sc_scatter_rows(hw.reshape(n, d // 2), dest, n_tiles * TM_MOE)
    yw_sorted = _moe_groups(tgrp.reshape(-1)[:n_tiles], hw_sorted, p["wr"], p["wg"], p["wu"], p["wd"])
    y = _final_norm(x1.reshape(n, d), _sc_gather_rows(yw_sorted, dest), p["norm_final"])
    return y.reshape(b, s, d)


def kernel(x_prompt, x_sample, rel_bias, norm_mix, w_in, lam_q1, lam_k1, lam_q2, lam_k2, subln_g,
           w_branch_a, w_branch_b, w_out, norm_ffn, w_group, w_router, w_exp_gate, w_exp_up,
           w_exp_down, norm_final):
    p = _prepare_weights(rel_bias, norm_mix, w_in, lam_q1, lam_k1, lam_q2, lam_k2, subln_g,
                         w_branch_a, w_branch_b, w_out, norm_ffn, w_group, w_router,
                         w_exp_gate, w_exp_up, w_exp_down, norm_final)
    return (_encoder(x_prompt, p), _encoder(x_sample, p))
```

```python
import functools
import math

import jax
import jax.numpy as jnp
from jax import lax
from jax.experimental import pallas as pl
from jax.experimental.pallas import tpu as pltpu
from jax.experimental.pallas import tpu_sc as plsc

F32 = jnp.float32
BF16 = jnp.bfloat16

D_MODEL = 1024
HEAD_DIM = 64
H_DIFF = 4
H_DIL = 4
DIL_CONFIGS = ((128, 1), (512, 4), (2048, 16))
N_DIL = len(DIL_CONFIGS)
DIFF_QK = H_DIFF * 2 * HEAD_DIM
DIFF_V = H_DIFF * 2 * HEAD_DIM
DIL_W = N_DIL * H_DIL * HEAD_DIM
GRP_W = H_DIL * HEAD_DIM
GRP_COLS = 3 * GRP_W
NAT_COLS = 2 * DIFF_QK + DIFF_V + GRP_COLS
GATE_COLS = 2 * D_MODEL
N_BUCKETS = 32
MAX_DISTANCE = 128
N_GROUPS = 4
E_PER_GROUP = 4
N_EXPERTS = N_GROUPS * E_PER_GROUP
D_EXPERT = 256
EPS = 1e-6
NEG_INF = -1e30
LAM_INIT = 0.8 - 0.6 * math.exp(-0.3 * 0)
LOG2E = math.log2(math.e)
LN2 = math.log(2.0)
QK_SCALE = HEAD_DIM ** -0.5

LANES = 128
SUBLANES = 8
VMEM_LIMIT = 56 * 1024 * 1024

TM_PROJ = 512
TQ_DIFF = 1024
TK_DIFF = 256
DIFF_SLOTS = 4
DEN_ROWS = 16
DIL_QB = 128
DIL_STEP_ROWS = 2048
SC_WINDOW = 128
SC_COLS = 256
TM_MERGE = 512
SUB_MERGE = 256
TM_MOE = 512
TM_FINAL = 1024
ROUTE_LANES = 128
EXP_LANE0 = N_GROUPS


def _const_spec(shape):
    nd = len(shape)
    return pl.BlockSpec(shape, lambda *_: (0,) * nd, pipeline_mode=pl.Buffered(1))


def _params(sem):
    return pltpu.CompilerParams(dimension_semantics=sem, vmem_limit_bytes=VMEM_LIMIT)


def _rel_bucket(rel):
    half = N_BUCKETS // 2
    max_exact = half // 2
    ret = (rel > 0).astype(jnp.int32) * half
    n = jnp.abs(rel)
    nf = jnp.maximum(n, 1).astype(F32)
    large = max_exact + (jnp.log(nf / max_exact) / math.log(MAX_DISTANCE / max_exact)
                         * (half - max_exact)).astype(jnp.int32)
    large = jnp.minimum(large, half - 1)
    return ret + jnp.where(n < max_exact, n, large)


def _bias_of_rel(table, rel):
    bucket = _rel_bucket(rel)[None]
    t = table.astype(F32)
    col = (t.shape[1],) + (1,) * rel.ndim
    out = jnp.zeros((t.shape[1],) + rel.shape, F32)
    for k in range(N_BUCKETS):
        out = jnp.where(bucket == k, t[k].reshape(col), out)
    return out


def _chunks(total, width):
    out, c = [], 0
    while c < total:
        w = min(width, total - c)
        out.append((c, w))
        c += w
    return out


def _norm_proj_kernel(x_ref, g_ref, wn_ref, wg_ref, w1_ref, w2_ref,
                      on_ref, og_ref, o1_ref, o2_ref, hs_ref, *, tm):
    x = x_ref[0]
    ms = jnp.mean(x * x, axis=-1, keepdims=True)
    hf = x * lax.rsqrt(ms + EPS) * g_ref[...]
    for c in range(D_MODEL // LANES):
        hs_ref[c] = hf[:, c * LANES:(c + 1) * LANES]
    h = hf.astype(BF16)
    for c0, cw in _chunks(NAT_COLS, 512):
        on_ref[0, :, c0:c0 + cw] = jnp.dot(h, wn_ref[:, c0:c0 + cw],
                                           preferred_element_type=F32).astype(BF16)
    for c0, cw in _chunks(GATE_COLS, 512):
        og_ref[0, :, c0:c0 + cw] = jnp.dot(h, wg_ref[:, c0:c0 + cw],
                                           preferred_element_type=F32).astype(BF16)
    for w_ref, o_ref, d in ((w1_ref, o1_ref, DIL_CONFIGS[1][1]), (w2_ref, o2_ref, DIL_CONFIGS[2][1])):
        rows = tm // d
        hp = jnp.concatenate(
            [jnp.concatenate([hs_ref[c, pl.ds(r, rows, stride=d), :] for r in range(d)], axis=0)
             for c in range(D_MODEL // LANES)], axis=1).astype(BF16)
        for c0, cw in _chunks(GRP_COLS, 256):
            res = jnp.dot(hp, w_ref[:, c0:c0 + cw], preferred_element_type=F32).astype(BF16)
            o_ref[0, :, :, c0:c0 + cw] = res.reshape(d, rows, cw)


def _norm_proj(x, g, w_nat, w_gate, w_d1, w_d2):
    b, s, _ = x.shape
    tm = TM_PROJ
    d1, d2 = DIL_CONFIGS[1][1], DIL_CONFIGS[2][1]
    return pl.pallas_call(
        functools.partial(_norm_proj_kernel, tm=tm),
        grid=(b, s // tm),
        in_specs=[
            pl.BlockSpec((1, tm, D_MODEL), lambda bi, i: (bi, i, 0)),
            _const_spec((1, D_MODEL)),
            _const_spec((D_MODEL, NAT_COLS)),
            _const_spec((D_MODEL, GATE_COLS)),
            _const_spec((D_MODEL, GRP_COLS)),
            _const_spec((D_MODEL, GRP_COLS)),
        ],
        out_specs=[
            pl.BlockSpec((1, tm, NAT_COLS), lambda bi, i: (bi, i, 0)),
            pl.BlockSpec((1, tm, GATE_COLS), lambda bi, i: (bi, i, 0)),
            pl.BlockSpec((1, d1, tm // d1, GRP_COLS), lambda bi, i: (bi, 0, i, 0)),
            pl.BlockSpec((1, d2, tm // d2, GRP_COLS), lambda bi, i: (bi, 0, i, 0)),
        ],
        out_shape=[
            jax.ShapeDtypeStruct((b, s, NAT_COLS), BF16),
            jax.ShapeDtypeStruct((b, s, GATE_COLS), BF16),
            jax.ShapeDtypeStruct((b, d1, s // d1, GRP_COLS), BF16),
            jax.ShapeDtypeStruct((b, d2, s // d2, GRP_COLS), BF16),
        ],
        scratch_shapes=[pltpu.VMEM((D_MODEL // LANES, tm, LANES), F32)],
        compiler_params=_params(("parallel", "parallel")),
        name="norm_proj",
    )(x, g, w_nat, w_gate, w_d1, w_d2)


def _diff_bias_tiles(rel_bias, tq, tk):
    ratio, w = tq // tk, tq + tk
    t = jnp.arange(-2, ratio + 2)[:, None]
    m = jnp.arange(w)[None, :]
    rel = t * tk + jnp.where(m < tq, -m, w - m)
    tiles = _bias_of_rel(rel_bias[:, :H_DIFF], rel) * LOG2E
    far = _bias_of_rel(rel_bias[:, :H_DIFF], jnp.asarray([-MAX_DISTANCE, MAX_DISTANCE])) * LOG2E
    return tiles, far


def _diff_attn_kernel(lam_ref, far_ref, q_ref, k_ref, v_ref, brow_ref, g_ref, o_ref,
                      vt_ref, st_ref, p_ref, acc_ref, bias_ref, *, tq, tk, nkb):
    h = pl.program_id(0)
    qi = pl.program_id(2)
    ratio = tq // tk
    n_band = ratio + 2
    vchunk = min(512, nkb * tk)

    @pl.when((pl.program_id(1) == 0) & (qi == 0))
    def _():
        for t in range(ratio + 4):
            rows = jnp.broadcast_to(brow_ref[0, t:t + 1, :], (tk, tq + tk))
            bias_ref[t] = pltpu.roll(rows, 0, 1, stride=1, stride_axis=0)[:, :tq]

    @pl.when(qi == 0)
    def _():
        for c in range(nkb * tk // vchunk):
            cs = slice(c * vchunk, (c + 1) * vchunk)
            vt_ref[:LANES, cs] = v_ref[0, cs, :].astype(F32).T.astype(BF16)
        vt_ref[LANES:, :] = jnp.ones((DEN_ROWS, nkb * tk), BF16)

    qt = q_ref[0].astype(F32).T
    first = lax.broadcasted_iota(jnp.int32, (LANES, tq), 0) < HEAD_DIM
    qts = (jnp.where(first, qt, 0.0).astype(BF16), jnp.where(first, 0.0, qt).astype(BF16))
    kb0 = qi * ratio

    def key_block(j):
        kb = kb0 - 1 + j
        return jnp.where(kb < 0, kb + nkb, jnp.where(kb >= nkb, kb - nkb, kb))

    def scores(j, slot, band):
        kb = key_block(j)
        ks = pl.multiple_of(kb * tk, tk)
        k = k_ref[0, pl.ds(ks, tk), :]
        if band:
            bt = bias_ref[jnp.clip(kb - kb0, -2, ratio + 1) + 2]
        mx = []
        for mp in range(2):
            st = jnp.dot(k, qts[mp], preferred_element_type=F32)
            if band:
                st = st + bt
            st_ref[slot, mp] = st
            mx.append(jnp.max(st, axis=0, keepdims=True))
        return tuple(mx)

    def softmax(j, slot, stats, mx, band):
        if not band:
            kb = key_block(j)
            far_c = jnp.where(kb < kb0, far_ref[h, 0], far_ref[h, 1])
        out, alphas = [], []
        for mp in range(2):
            m = stats[mp]
            if band:
                m_new = jnp.maximum(m, mx[mp])
                shift = m_new
            else:
                m_new = jnp.maximum(m, mx[mp] + far_c)
                shift = m_new - far_c
            p_ref[slot, mp] = jnp.exp2((st_ref[slot, mp] - shift).astype(BF16))
            out.append(m_new)
            alphas.append(jnp.exp2(m - m_new))
        return tuple(out), tuple(alphas)

    def values(j, slot, alphas):
        ks = pl.multiple_of(key_block(j) * tk, tk)
        vt = vt_ref[:, pl.ds(ks, tk)]
        for mp in range(2):
            acc_ref[mp] = alphas[mp] * acc_ref[mp] + jnp.dot(vt, p_ref[slot, mp],
                                                             preferred_element_type=F32)

    acc_ref[...] = jnp.zeros_like(acc_ref)
    mx = scores(0, 0, True)
    stats = (jnp.full((1, tq), -jnp.inf, F32),) * 2
    alphas = None
    for j in range(nkb):
        if j + 1 < nkb:
            mx_next = scores(j + 1, (j + 1) % DIFF_SLOTS, j + 1 < n_band)
        if j >= 1:
            values(j - 1, (j - 1) % DIFF_SLOTS, alphas)
        stats, alphas = softmax(j, j % DIFF_SLOTS, stats, mx, j < n_band)
        mx = mx_next
    values(nkb - 1, (nkb - 1) % DIFF_SLOTS, alphas)
    l0, l1 = acc_ref[0, LANES:LANES + 1, :], acc_ref[1, LANES:LANES + 1, :]

    lp = lam_ref[...]
    lam = (jnp.exp(jnp.sum(lp[0:1] * lp[1:2], axis=-1, keepdims=True))
           - jnp.exp(jnp.sum(lp[2:3] * lp[3:4], axis=-1, keepdims=True)) + LAM_INIT)
    ot = acc_ref[0, :LANES, :] / l0 - lam * (acc_ref[1, :LANES, :] / l1)
    ms = jnp.mean(ot * ot, axis=0, keepdims=True)
    y = (ot * lax.rsqrt(ms + EPS)).T * g_ref[...] * (1.0 - LAM_INIT)
    o_ref[0] = y.astype(BF16)


def _diff_attn(nat, bias_tiles, far, lam_params, subln_g):
    b, s, _ = nat.shape
    tq, tk = min(TQ_DIFF, s), min(TK_DIFF, s)
    nt = bias_tiles.shape[1]
    ratio, nkb = tq // tk, s // tk
    assert tq % tk == 0 and tk >= MAX_DISTANCE and nkb >= ratio + 2
    assert DIFF_SLOTS >= 3
    kern = functools.partial(_diff_attn_kernel, tq=tq, tk=tk, nkb=s // tk)
    return pl.pallas_call(
        kern,
        grid=(H_DIFF, b, s // tq),
        in_specs=[
            _const_spec((4, HEAD_DIM)),
            pl.BlockSpec(memory_space=pltpu.SMEM),
            pl.BlockSpec((1, tq, LANES), lambda h, bi, qi: (bi, qi, h)),
            pl.BlockSpec((1, s, LANES), lambda h, bi, qi: (bi, 0, H_DIFF + h)),
            pl.BlockSpec((1, s, LANES), lambda h, bi, qi: (bi, 0, 2 * H_DIFF + h)),
            pl.BlockSpec((1, nt, tq + tk), lambda h, bi, qi: (h, 0, 0)),
            _const_spec((1, 2 * HEAD_DIM)),
        ],
        out_specs=pl.BlockSpec((1, tq, LANES), lambda h, bi, qi: (bi, qi, h)),
        out_shape=jax.ShapeDtypeStruct((b, s, DIFF_V), BF16),
        scratch_shapes=[
            pltpu.VMEM((LANES + DEN_ROWS, s), BF16),
            pltpu.VMEM((DIFF_SLOTS, 2, tk, tq), F32),
            pltpu.VMEM((DIFF_SLOTS, 2, tk, tq), BF16),
            pltpu.VMEM((2, LANES + DEN_ROWS, tq), F32),
            pltpu.VMEM((nt, tk, tq), F32),
        ],
        compiler_params=_params(("arbitrary", "arbitrary", "arbitrary")),
        name="diff_attn",
    )(lam_params, far, nat, nat, nat, bias_tiles, subln_g)


def _dil_geometry(s, window, dilation):
    l = s // dilation
    w = window // (2 * dilation)
    qb = min(DIL_QB, l)
    kw = min(qb + 2 * LANES, l)
    assert w <= LANES and l % qb == 0 and l % LANES == 0
    return l, w, qb, kw, l // qb


def _dil_bias_tiles(rel_bias, g, s, window, dilation):
    l, w, qb, kw, nb = _dil_geometry(s, window, dilation)
    c0 = H_DIFF + g * H_DIL
    shifts = []
    for n in (0, min(1, nb - 1), nb - 1):
        l0 = n * qb
        shifts.append(min(max(l0 - LANES, 0), l - kw) - l0)
    j = jnp.arange(kw)[None, :, None]
    i = jnp.arange(qb)[None, None, :]
    off = jnp.asarray(shifts)[:, None, None] + j - i
    b = _bias_of_rel(rel_bias[:, c0:c0 + H_DIL], off * dilation) * LOG2E
    b = jnp.where((jnp.abs(off) <= w)[None], b, NEG_INF)
    return jnp.stack([jnp.concatenate([b[2 * p], b[2 * p + 1]], axis=-1) for p in range(H_DIL // 2)], axis=1)


def _dil_attn_kernel(q_ref, k_ref, v_ref, bias_ref, o_ref, lse_ref, qt_ref, vt_ref, st_ref,
                     *, l, qb, kw, nb, cps):
    chunk = min(512, l)
    first = lax.broadcasted_iota(jnp.int32, (LANES, qb), 0) < HEAD_DIM

    def transposes(c):
        for pair in range(H_DIL // 2):
            cs = slice(pair * LANES, (pair + 1) * LANES)
            for i in range(l // chunk):
                rs = slice(i * chunk, (i + 1) * chunk)
                qt_ref[c, pair, :, rs] = q_ref[0, c, rs, cs].astype(F32).T.astype(BF16)
                vt_ref[c, pair, :LANES, rs] = v_ref[0, c, rs, cs].astype(F32).T.astype(BF16)
            vt_ref[c, pair, LANES:, :] = jnp.ones((DEN_ROWS, l), BF16)

    def window(n):
        l0 = n * qb
        return l0, min(max(l0 - LANES, 0), l - kw)

    def scores(c, n, slot):
        l0, ws = window(n)
        for pair in range(H_DIL // 2):
            qt = qt_ref[c, pair, :, l0:l0 + qb]
            q2 = jnp.concatenate([jnp.where(first, qt, jnp.zeros_like(qt)),
                                  jnp.where(first, jnp.zeros_like(qt), qt)], axis=1)
            kp = k_ref[0, c, ws:ws + kw, pair * LANES:(pair + 1) * LANES]
            st_ref[c, slot, pair] = jnp.dot(kp, q2, preferred_element_type=F32)

    def finish(c, n, slot):
        l0, ws = window(n)
        t = 0 if n == 0 else (2 if n == nb - 1 else 1)
        for pair in range(H_DIL // 2):
            cs = slice(pair * LANES, (pair + 1) * LANES)
            st = st_ref[c, slot, pair] + bias_ref[t, pair]
            mx = jnp.max(st, axis=0, keepdims=True)
            p = jnp.exp2(st - mx).astype(BF16)
            acc = jnp.dot(vt_ref[c, pair, :, ws:ws + kw], p, preferred_element_type=F32)
            den = acc[LANES:LANES + 1, :]
            ot = acc[:LANES, :] / den
            lse = mx * LN2 + jnp.log(den)
            o_t = jnp.where(first, ot[:, :qb], ot[:, qb:])
            lse_t = jnp.where(first, lse[:, :qb], lse[:, qb:])
            o_ref[0, c, l0:l0 + qb, cs] = o_t.T.astype(BF16)
            lse_ref[0, c, l0:l0 + qb, cs] = lse_t.T

    for c in range(cps):
        transposes(c)
    for c in range(cps):
        scores(c, 0, 0)
    for c in range(cps):
        for n in range(nb):
            if n + 1 < nb:
                scores(c, n + 1, (n + 1) % 2)
            finish(c, n, n % 2)


def _dil_attn(arr, col0, bias_tiles, g, window, dilation):
    b, d, l, _ = arr.shape
    _, w, qb, kw, nb = _dil_geometry(l * d, window, dilation)
    cps = max(1, min(d, DIL_STEP_ROWS // l))
    kern = functools.partial(_dil_attn_kernel, l=l, qb=qb, kw=kw, nb=nb, cps=cps)
    return pl.pallas_call(
        kern,
        grid=(b, d // cps),
        in_specs=[
            pl.BlockSpec((1, cps, l, GRP_W), lambda bi, r: (bi, r, 0, col0)),
            pl.BlockSpec((1, cps, l, GRP_W), lambda bi, r: (bi, r, 0, col0 + 1)),
            pl.BlockSpec((1, cps, l, GRP_W), lambda bi, r: (bi, r, 0, col0 + 2)),
            _const_spec((3, H_DIL // 2, kw, 2 * qb)),
        ],
        out_specs=[
            pl.BlockSpec((1, cps, l, GRP_W), lambda bi, r: (bi, r, 0, 0)),
            pl.BlockSpec((1, cps, l, GRP_W), lambda bi, r: (bi, r, 0, 0)),
        ],
        out_shape=[
            jax.ShapeDtypeStruct((b, d, l, GRP_W), BF16),
            jax.ShapeDtypeStruct((b, d, l, GRP_W), F32),
        ],
        scratch_shapes=[
            pltpu.VMEM((cps, H_DIL // 2, LANES, l), BF16),
            pltpu.VMEM((cps, H_DIL // 2, LANES + DEN_ROWS, l), BF16),
            pltpu.VMEM((cps, 2, H_DIL // 2, kw, 2 * qb), F32),
        ],
        compiler_params=_params(("parallel", "parallel")),
        name=f"dil_attn_g{g}",
    )(arr, arr, arr, bias_tiles)


def _pack_words(a, b):
    ua = lax.bitcast_convert_type(a.astype(BF16).astype(F32), jnp.uint32)
    ub = lax.bitcast_convert_type(b.astype(BF16).astype(F32), jnp.uint32)
    return (ua >> 16) | (ub & jnp.uint32(0xFFFF0000))


def _unpack_words(w):
    return (lax.bitcast_convert_type(w << 16, F32),
            lax.bitcast_convert_type(w & jnp.uint32(0xFFFF0000), F32))


def _route(lg):
    lane = lax.broadcasted_iota(jnp.int32, lg.shape, 1)
    big = jnp.int32(ROUTE_LANES)
    gl = jnp.where(lane < N_GROUPS, lg, -jnp.inf)
    gmax = jnp.max(gl, axis=-1, keepdims=True)
    gi = jnp.min(jnp.where(gl == gmax, lane, big), axis=-1, keepdims=True)
    gw = 1.0 / jnp.sum(jnp.exp(gl - gmax), axis=-1, keepdims=True)
    e_lo = EXP_LANE0 + E_PER_GROUP * gi
    es = jnp.where((lane >= e_lo) & (lane < e_lo + E_PER_GROUP), lg, -jnp.inf)
    v1 = jnp.max(es, axis=-1, keepdims=True)
    i1 = jnp.min(jnp.where(es == v1, lane, big), axis=-1, keepdims=True)
    es2 = jnp.where(lane == i1, -jnp.inf, es)
    v2 = jnp.max(es2, axis=-1, keepdims=True)
    i2 = jnp.min(jnp.where(es2 == v2, lane, big), axis=-1, keepdims=True)
    t = jnp.exp(v2 - v1)
    w1 = gw / (1.0 + t)
    w2 = w1 * t
    return gi, jnp.where(lane == i1, w1, 0.0) + jnp.where(lane == i2, w2, 0.0)


def _merge_route_kernel(x_ref, oa_ref, o0_ref, l0_ref, o1_ref, l1_ref, o2_ref, l2_ref, gt_ref,
                        wa_ref, wb_ref, wo_ref, nf_ref, wr_ref, x1_ref, hw_ref, gid_ref,
                        so1_ref, sl1_ref, so2_ref, sl2_ref, *, tm):
    for src, dst, d in ((o1_ref, so1_ref, DIL_CONFIGS[1][1]), (l1_ref, sl1_ref, DIL_CONFIGS[1][1]),
                        (o2_ref, so2_ref, DIL_CONFIGS[2][1]), (l2_ref, sl2_ref, DIL_CONFIGS[2][1])):
        for r in range(d):
            for c in range(GRP_W // LANES):
                dst[c, pl.ds(r, tm // d, stride=d), :] = src[0, r, :, c * LANES:(c + 1) * LANES].astype(F32)
    subs = [slice(i * SUB_MERGE, (i + 1) * SUB_MERGE) for i in range(tm // SUB_MERGE)]
    unsplit = lambda ref, rs: jnp.concatenate([ref[c, rs, :] for c in range(GRP_W // LANES)], axis=1)

    def dil_mix(rs):
        l0, l1, l2 = l0_ref[0, 0, rs, :], unsplit(sl1_ref, rs), unsplit(sl2_ref, rs)
        mx = jnp.maximum(jnp.maximum(l0, l1), l2)
        e0, e1, e2 = jnp.exp(l0 - mx), jnp.exp(l1 - mx), jnp.exp(l2 - mx)
        ob = (e0 * o0_ref[0, 0, rs, :].astype(F32) + e1 * unsplit(so1_ref, rs)
              + e2 * unsplit(so2_ref, rs)) / (e0 + e1 + e2)
        return ob.astype(BF16)

    obs = [dil_mix(rs) for rs in subs]
    pas = [jnp.dot(oa_ref[0, rs, :], wa_ref[...], preferred_element_type=F32) for rs in subs]
    pbs = [jnp.dot(ob, wb_ref[...], preferred_element_type=F32) for ob in obs]

    def gate(rs, pa, pb):
        ga = jax.nn.sigmoid(gt_ref[0, rs, :D_MODEL].astype(F32))
        gb = jax.nn.sigmoid(gt_ref[0, rs, D_MODEL:].astype(F32))
        return (ga * pa + gb * pb).astype(BF16)

    mgs = [gate(rs, pa, pb) for rs, pa, pb in zip(subs, pas, pbs)]
    x1s = [x_ref[0, rs, :] + jnp.dot(mg, wo_ref[...], preferred_element_type=F32)
           for rs, mg in zip(subs, mgs)]

    def ffn_norm(rs, x1):
        x1_ref[0, rs, :] = x1
        ms = jnp.mean(x1 * x1, axis=-1, keepdims=True)
        h2f = x1 * lax.rsqrt(ms + EPS) * nf_ref[...]
        hw_ref[0, rs, :] = _pack_words(h2f[:, :D_MODEL // 2], h2f[:, D_MODEL // 2:])
        return h2f.astype(BF16)

    h2s = [ffn_norm(rs, x1) for rs, x1 in zip(subs, x1s)]
    lgs = [jnp.dot(h2, wr_ref[...], preferred_element_type=F32) for h2 in h2s]
    for rs, lg in zip(subs, lgs):
        gi, _ = _route(lg)
        git = jnp.broadcast_to(gi.astype(F32), lg.shape).T
        gid_ref[0, 0, :, rs] = git[:SUBLANES, :].astype(jnp.int32)


def _merge_route(x, oa, dil, gates, wa, wb, wo, nf, wr):
    b, s, _ = x.shape
    tm = TM_MERGE
    d1, d2 = DIL_CONFIGS[1][1], DIL_CONFIGS[2][1]
    tok = lambda bi, i: (bi, i, 0)
    cls = lambda bi, i: (bi, 0, i, 0)
    (o0, s0), (o1, s1), (o2, s2) = dil
    return pl.pallas_call(
        functools.partial(_merge_route_kernel, tm=tm),
        grid=(b, s // tm),
        in_specs=[
            pl.BlockSpec((1, tm, D_MODEL), tok),
            pl.BlockSpec((1, tm, DIFF_V), tok),
            pl.BlockSpec((1, 1, tm, GRP_W), cls), pl.BlockSpec((1, 1, tm, GRP_W), cls),
            pl.BlockSpec((1, d1, tm // d1, GRP_W), cls), pl.BlockSpec((1, d1, tm // d1, GRP_W), cls),
            pl.BlockSpec((1, d2, tm // d2, GRP_W), cls), pl.BlockSpec((1, d2, tm // d2, GRP_W), cls),
            pl.BlockSpec((1, tm, GATE_COLS), tok),
            _const_spec((DIFF_V, D_MODEL)),
            _const_spec((GRP_W, D_MODEL)),
            _const_spec((D_MODEL, D_MODEL)),
            _const_spec((1, D_MODEL)),
            _const_spec((D_MODEL, ROUTE_LANES)),
        ],
        out_specs=[
            pl.BlockSpec((1, tm, D_MODEL), tok),
            pl.BlockSpec((1, tm, D_MODEL // 2), tok),
            pl.BlockSpec((1, 1, SUBLANES, tm), lambda bi, i: (bi, i, 0, 0)),
        ],
        out_shape=[
            jax.ShapeDtypeStruct((b, s, D_MODEL), F32),
            jax.ShapeDtypeStruct((b, s, D_MODEL // 2), jnp.uint32),
            jax.ShapeDtypeStruct((b, s // tm, SUBLANES, tm), jnp.int32),
        ],
        scratch_shapes=[pltpu.VMEM((GRP_W // LANES, tm, LANES), F32) for _ in range(4)],
        compiler_params=_params(("parallel", "parallel")),
        name="merge_route",
    )(x, oa, o0, s0, o1, s1, o2, s2, gates, wa, wb, wo, nf, wr)


def _dispatch_plan_kernel(gid_ref, dest_ref, tgrp_ref, *, rows, cols):
    gid = gid_ref[...]
    ci = lax.broadcasted_iota(jnp.int32, (cols, cols), 0)
    cj = lax.broadcasted_iota(jnp.int32, (cols, cols), 1)
    upto = (ci <= cj).astype(BF16)
    ri = lax.broadcasted_iota(jnp.int32, (rows, rows), 0)
    rj = lax.broadcasted_iota(jnp.int32, (rows, rows), 1)
    above = (rj < ri).astype(F32)
    seg = jnp.zeros((1, 1), F32)
    dest = jnp.zeros((rows, cols), F32)
    ends = []
    for g in range(N_GROUPS):
        mem = (gid == g).astype(F32)
        pref = jnp.dot(mem.astype(BF16), upto, preferred_element_type=F32)
        tot = pref[:, cols - 1:cols]
        before = jnp.dot(above, jnp.broadcast_to(tot, (rows, LANES)), preferred_element_type=F32,
                         precision=lax.Precision.HIGHEST)[:, :1]
        dest = dest + mem * (seg + before + pref - 1.0)
        cnt = jnp.sum(tot, axis=0, keepdims=True)
        seg = seg + jnp.ceil(cnt / TM_MOE) * TM_MOE
        ends.append(seg)
    dest_ref[...] = dest.astype(jnp.int32)
    tile = (lax.broadcasted_iota(jnp.int32, (SUBLANES, LANES), 0) * LANES
            + lax.broadcasted_iota(jnp.int32, (SUBLANES, LANES), 1))
    start = (tile * TM_MOE).astype(F32)
    tgrp = jnp.zeros((SUBLANES, LANES), jnp.int32)
    for g in range(N_GROUPS - 1):
        tgrp = tgrp + (start >= ends[g]).astype(jnp.int32)
    tgrp_ref[...] = tgrp


def _dispatch_plan(gid):
    rows, cols = gid.shape
    assert rows % SUBLANES == 0 and rows * cols // TM_MOE + N_GROUPS <= SUBLANES * LANES
    return pl.pallas_call(
        functools.partial(_dispatch_plan_kernel, rows=rows, cols=cols),
        out_shape=[jax.ShapeDtypeStruct((rows, cols), jnp.int32),
                   jax.ShapeDtypeStruct((SUBLANES, LANES), jnp.int32)],
        compiler_params=pltpu.CompilerParams(vmem_limit_bytes=VMEM_LIMIT),
        name="dispatch_plan",
    )(gid)


def _moe_group_kernel(tgrp_ref, hw_ref, wr_ref, wg_ref, wu_ref, wd_ref, ow_ref, he_ref):
    g = tgrp_ref[pl.program_id(0)]
    lo, hi = _unpack_words(hw_ref[...])
    h2 = jnp.concatenate([lo, hi], axis=1).astype(BF16)
    _, dw = _route(jnp.dot(h2, wr_ref[...], preferred_element_type=F32))
    lane = lax.broadcasted_iota(jnp.int32, dw.shape, 1)
    for e in range(E_PER_GROUP):
        we = jnp.sum(jnp.where(lane == EXP_LANE0 + E_PER_GROUP * g + e, dw, 0.0), axis=-1, keepdims=True)
        gp = jnp.dot(h2, wg_ref[0, e], preferred_element_type=F32)
        up = jnp.dot(h2, wu_ref[0, e], preferred_element_type=F32)
        he_ref[:, e * D_EXPERT:(e + 1) * D_EXPERT] = (jax.nn.silu(gp) * up * we).astype(BF16)
    moe = jnp.dot(he_ref[...], wd_ref[0], preferred_element_type=F32)
    ow_ref[...] = _pack_words(moe[:, :D_MODEL // 2], moe[:, D_MODEL // 2:])


def _moe_groups(tgrp, hw_sorted, wr, wg, wu, wd):
    m = hw_sorted.shape[0]
    tm = TM_MOE
    row = lambda i, t: (i, 0)
    grp = lambda i, t: (t[i], 0, 0, 0)
    return pl.pallas_call(
        _moe_group_kernel,
        grid_spec=pltpu.PrefetchScalarGridSpec(
            num_scalar_prefetch=1,
            grid=(m // tm,),
            in_specs=[
                pl.BlockSpec((tm, D_MODEL // 2), row),
                pl.BlockSpec((D_MODEL, ROUTE_LANES), lambda i, t: (0, 0)),
                pl.BlockSpec((1, E_PER_GROUP, D_MODEL, D_EXPERT), grp),
                pl.BlockSpec((1, E_PER_GROUP, D_MODEL, D_EXPERT), grp),
                pl.BlockSpec((1, E_PER_GROUP * D_EXPERT, D_MODEL), lambda i, t: (t[i], 0, 0)),
            ],
            out_specs=pl.BlockSpec((tm, D_MODEL // 2), row),
            scratch_shapes=[pltpu.VMEM((tm, E_PER_GROUP * D_EXPERT), BF16)],
        ),
        out_shape=jax.ShapeDtypeStruct((m, D_MODEL // 2), jnp.uint32),
        compiler_params=_params(("arbitrary",)),
        name="moe_groups",
    )(tgrp, hw_sorted, wr, wg, wu, wd)


def _final_norm_kernel(x1_ref, yw_ref, nf_ref, y_ref):
    lo, hi = _unpack_words(yw_ref[...])
    x2 = x1_ref[...] + jnp.concatenate([lo, hi], axis=1)
    ms = jnp.mean(x2 * x2, axis=-1, keepdims=True)
    y_ref[...] = x2 * lax.rsqrt(ms + EPS) * nf_ref[...]


def _final_norm(x1, yw, nf):
    n = x1.shape[0]
    tm = TM_FINAL
    row = lambda i: (i, 0)
    return pl.pallas_call(
        _final_norm_kernel,
        grid=(n // tm,),
        in_specs=[pl.BlockSpec((tm, D_MODEL), row), pl.BlockSpec((tm, D_MODEL // 2), row),
                  _const_spec((1, D_MODEL))],
        out_specs=pl.BlockSpec((tm, D_MODEL), row),
        out_shape=jax.ShapeDtypeStruct((n, D_MODEL), F32),
        compiler_params=_params(("parallel",)),
        name="final_norm",
    )(x1, yw, nf)


def _sc_mesh():
    return plsc.VectorSubcoreMesh(core_axis_name="core", subcore_axis_name="subcore")


def _sc_scatter_rows(x, idx, m):
    n, d = x.shape

    @pl.kernel(out_type=jax.ShapeDtypeStruct((m, d), x.dtype), mesh=_sc_mesh(), scratch_types=[])
    def scatter(x_hbm, i_hbm, o_hbm):
        for c in range(d // SC_COLS):
            cols = pl.ds(c * SC_COLS, SC_COLS)
            o_cols = o_hbm.at[:, cols]

            def body(x_vmem, i_vmem, o_cols=o_cols):
                pltpu.sync_copy(x_vmem, o_cols.at[i_vmem.at[0]])

            pltpu.emit_pipeline(
                body,
                grid=(n // SC_WINDOW,),
                in_specs=[pl.BlockSpec((SC_WINDOW, SC_COLS), lambda i, c=c: (i, c)),
                          pl.BlockSpec((1, SC_WINDOW), lambda i: (0, i))],
                out_specs=[],
                core_axis_name=("core", "subcore"),
                dimension_semantics=(pltpu.PARALLEL,),
            )(x_hbm, i_hbm)

    return scatter(x, idx.reshape(1, n))


def _sc_gather_rows(x, idx):
    n = idx.shape[0]
    d = x.shape[1]

    @pl.kernel(out_type=jax.ShapeDtypeStruct((n, d), x.dtype), mesh=_sc_mesh(), scratch_types=[])
    def gather(x_hbm, i_hbm, o_hbm):
        for c in range(d // SC_COLS):
            x_cols = x_hbm.at[:, pl.ds(c * SC_COLS, SC_COLS)]

            def body(i_vmem, o_vmem, x_cols=x_cols):
                pltpu.sync_copy(x_cols.at[i_vmem.at[0]], o_vmem)

            pltpu.emit_pipeline(
                body,
                grid=(n // SC_WINDOW,),
                in_specs=[pl.BlockSpec((1, SC_WINDOW), lambda i: (0, i))],
                out_specs=[pl.BlockSpec((SC_WINDOW, SC_COLS), lambda i, c=c: (i, c))],
                core_axis_name=("core", "subcore"),
                dimension_semantics=(pltpu.PARALLEL,),
            )(i_hbm, o_hbm)

    return gather(x, idx.reshape(1, n))


def _prepare_weights(rel_bias, norm_mix, w_in, lam_q1, lam_k1, lam_q2, lam_k2, subln_g,
                     w_branch_a, w_branch_b, w_out, norm_ffn, w_group, w_router,
                     w_exp_gate, w_exp_up, w_exp_down, norm_final):
    w = w_in[0]
    c = QK_SCALE * LOG2E
    o = 0
    qa, o = w[:, o:o + DIFF_QK] * c, o + DIFF_QK
    ka, o = w[:, o:o + DIFF_QK], o + DIFF_QK
    va, o = w[:, o:o + DIFF_V], o + DIFF_V
    qb, o = w[:, o:o + DIL_W] * c, o + DIL_W
    kb, o = w[:, o:o + DIL_W], o + DIL_W
    vb, o = w[:, o:o + DIL_W], o + DIL_W
    w_gate = w[:, o:].astype(BF16)
    grp = lambda g: [t[:, g * GRP_W:(g + 1) * GRP_W] for t in (qb, kb, vb)]
    w_nat = jnp.concatenate([qa, ka, va] + grp(0), axis=1).astype(BF16)
    w_d1 = jnp.concatenate(grp(1), axis=1).astype(BF16)
    w_d2 = jnp.concatenate(grp(2), axis=1).astype(BF16)
    wr = jnp.concatenate([w_group[0], w_router[0].transpose(1, 0, 2).reshape(D_MODEL, N_EXPERTS)], axis=1)
    wr = jnp.pad(wr, ((0, 0), (0, ROUTE_LANES - wr.shape[1]))).astype(BF16)
    return dict(
        rel_bias=rel_bias.astype(F32),
        norm_mix=norm_mix[0][None].astype(F32),
        w_nat=w_nat, w_gate=w_gate, w_d1=w_d1, w_d2=w_d2,
        lam=jnp.stack([lam_q1[0], lam_k1[0], lam_q2[0], lam_k2[0]]).astype(F32),
        subln_g=subln_g[0][None].astype(F32),
        wa=w_branch_a[0].astype(BF16), wb=w_branch_b[0].astype(BF16), wo=w_out[0].astype(BF16),
        norm_ffn=norm_ffn[0][None].astype(F32), wr=wr,
        wg=w_exp_gate[0].reshape(N_GROUPS, E_PER_GROUP, D_MODEL, D_EXPERT).astype(BF16),
        wu=w_exp_up[0].reshape(N_GROUPS, E_PER_GROUP, D_MODEL, D_EXPERT).astype(BF16),
        wd=w_exp_down[0].reshape(N_GROUPS, E_PER_GROUP * D_EXPERT, D_MODEL).astype(BF16),
        norm_final=norm_final[None].astype(F32),
    )


def _encoder(x, p):
    b, s, d = x.shape
    n = b * s
    nat, gates, dil1, dil2 = _norm_proj(x, p["norm_mix"], p["w_nat"], p["w_gate"], p["w_d1"], p["w_d2"])
    tiles, far = _diff_bias_tiles(p["rel_bias"], min(TQ_DIFF, s), min(TK_DIFF, s))
    oa = _diff_attn(nat, tiles, far, p["lam"], p["subln_g"])
    srcs = ((nat.reshape(b, 1, s, NAT_COLS), (2 * DIFF_QK + DIFF_V) // GRP_W), (dil1, 0), (dil2, 0))
    dil = []
    for g, (win, dilation) in enumerate(DIL_CONFIGS):
        tiles = _dil_bias_tiles(p["rel_bias"], g, s, win, dilation)
        dil.append(_dil_attn(srcs[g][0], srcs[g][1], tiles, g, win, dilation))
    x1, hw, gid = _merge_route(x, oa, dil, gates, p["wa"], p["wb"], p["wo"], p["norm_ffn"], p["wr"])
    dest, tgrp = _dispatch_plan(gid[:, :, 0, :].reshape(n // TM_MERGE, TM_MERGE))
    dest = dest.reshape(n)
    n_tiles = n // TM_MOE + N_GROUPS
    hw_sorted = _sc_scatter_rows(hw.reshape(n, d // 2), dest, n_tiles * TM_MOE)
    yw_sorted = _moe_groups(tgrp.reshape(-1)[:n_tiles], hw_sorted, p["wr"], p["wg"], p["wu"], p["wd"])
    y = _final_norm(x1.reshape(n, d), _sc_gather_rows(yw_sorted, dest), p["norm_final"])
    return y.reshape(b, s, d)


def kernel(x_prompt, x_sample, rel_bias, norm_mix, w_in, lam_q1, lam_k1, lam_q2, lam_k2, subln_g,
           w_branch_a, w_branch_b, w_out, norm_ffn, w_group, w_router, w_exp_gate, w_exp_up,
           w_exp_down, norm_final):
    p = _prepare_weights(rel_bias, norm_mix, w_in, lam_q1, lam_k1, lam_q2, lam_k2, subln_g,
                         w_branch_a, w_branch_b, w_out, norm_ffn, w_group, w_router,
                         w_exp_gate, w_exp_up, w_exp_down, norm_final)
    return (_encoder(x_prompt, p), _encoder(x_sample, p))
```

```python
import functools
import math

import jax
import jax.numpy as jnp
from jax import lax
from jax.experimental import pallas as pl
from jax.experimental.pallas import tpu as pltpu
from jax.experimental.pallas import tpu_sc as plsc

F32 = jnp.float32
BF16 = jnp.bfloat16

D_MODEL = 1024
HEAD_DIM = 64
H_DIFF = 4
H_DIL = 4
DIL_CONFIGS = ((128, 1), (512, 4), (2048, 16))
N_DIL = len(DIL_CONFIGS)
DIFF_QK = H_DIFF * 2 * HEAD_DIM
DIFF_V = H_DIFF * 2 * HEAD_DIM
DIL_W = N_DIL * H_DIL * HEAD_DIM
GRP_W = H_DIL * HEAD_DIM
GRP_COLS = 3 * GRP_W
NAT_COLS = 2 * DIFF_QK + DIFF_V + GRP_COLS
GATE_COLS = 2 * D_MODEL
N_BUCKETS = 32
MAX_DISTANCE = 128
N_GROUPS = 4
E_PER_GROUP = 4
N_EXPERTS = N_GROUPS * E_PER_GROUP
D_EXPERT = 256
EPS = 1e-6
NEG_INF = -1e30
LAM_INIT = 0.8 - 0.6 * math.exp(-0.3 * 0)
LOG2E = math.log2(math.e)
LN2 = math.log(2.0)
QK_SCALE = HEAD_DIM ** -0.5

LANES = 128
SUBLANES = 8
VMEM_LIMIT = 56 * 1024 * 1024

TM_PROJ = 512
TQ_DIFF = 1024
TK_DIFF = 256
DIFF_SLOTS = 4
DEN_ROWS = 16
DIL_QB = 128
DIL_STEP_ROWS = 2048
SC_WINDOW = 128
SC_COLS = 256
TM_MERGE = 512
SUB_MERGE = 256
TM_MOE = 512
TM_FINAL = 1024
ROUTE_LANES = 128
EXP_LANE0 = N_GROUPS


def _const_spec(shape):
    nd = len(shape)
    return pl.BlockSpec(shape, lambda *_: (0,) * nd, pipeline_mode=pl.Buffered(1))


def _params(sem):
    return pltpu.CompilerParams(dimension_semantics=sem, vmem_limit_bytes=VMEM_LIMIT)


def _rel_bucket(rel):
    half = N_BUCKETS // 2
    max_exact = half // 2
    ret = (rel > 0).astype(jnp.int32) * half
    n = jnp.abs(rel)
    nf = jnp.maximum(n, 1).astype(F32)
    large = max_exact + (jnp.log(nf / max_exact) / math.log(MAX_DISTANCE / max_exact)
                         * (half - max_exact)).astype(jnp.int32)
    large = jnp.minimum(large, half - 1)
    return ret + jnp.where(n < max_exact, n, large)


def _bias_of_rel(table, rel):
    bucket = _rel_bucket(rel)[None]
    t = table.astype(F32)
    col = (t.shape[1],) + (1,) * rel.ndim
    out = jnp.zeros((t.shape[1],) + rel.shape, F32)
    for k in range(N_BUCKETS):
        out = jnp.where(bucket == k, t[k].reshape(col), out)
    return out


def _chunks(total, width):
    out, c = [], 0
    while c < total:
        w = min(width, total - c)
        out.append((c, w))
        c += w
    return out


def _norm_proj_kernel(x_ref, g_ref, wn_ref, wg_ref, w1_ref, w2_ref,
                      on_ref, og_ref, o1_ref, o2_ref, hs_ref, *, tm):
    x = x_ref[0]
    ms = jnp.mean(x * x, axis=-1, keepdims=True)
    hf = x * lax.rsqrt(ms + EPS) * g_ref[...]
    for c in range(D_MODEL // LANES):
        hs_ref[c] = hf[:, c * LANES:(c + 1) * LANES]
    h = hf.astype(BF16)
    for c0, cw in _chunks(NAT_COLS, 512):
        on_ref[0, :, c0:c0 + cw] = jnp.dot(h, wn_ref[:, c0:c0 + cw],
                                           preferred_element_type=F32).astype(BF16)
    for c0, cw in _chunks(GATE_COLS, 512):
        og_ref[0, :, c0:c0 + cw] = jnp.dot(h, wg_ref[:, c0:c0 + cw],
                                           preferred_element_type=F32).astype(BF16)
    for w_ref, o_ref, d in ((w1_ref, o1_ref, DIL_CONFIGS[1][1]), (w2_ref, o2_ref, DIL_CONFIGS[2][1])):
        rows = tm // d
        hp = jnp.concatenate(
            [jnp.concatenate([hs_ref[c, pl.ds(r, rows, stride=d), :] for r in range(d)], axis=0)
             for c in range(D_MODEL // LANES)], axis=1).astype(BF16)
        for c0, cw in _chunks(GRP_COLS, 256):
            res = jnp.dot(hp, w_ref[:, c0:c0 + cw], preferred_element_type=F32).astype(BF16)
            o_ref[0, :, :, c0:c0 + cw] = res.reshape(d, rows, cw)


def _norm_proj(x, g, w_nat, w_gate, w_d1, w_d2):
    b, s, _ = x.shape
    tm = TM_PROJ
    d1, d2 = DIL_CONFIGS[1][1], DIL_CONFIGS[2][1]
    return pl.pallas_call(
        functools.partial(_norm_proj_kernel, tm=tm),
        grid=(b, s // tm),
        in_specs=[
            pl.BlockSpec((1, tm, D_MODEL), lambda bi, i: (bi, i, 0)),
            _const_spec((1, D_MODEL)),
            _const_spec((D_MODEL, NAT_COLS)),
            _const_spec((D_MODEL, GATE_COLS)),
            _const_spec((D_MODEL, GRP_COLS)),
            _const_spec((D_MODEL, GRP_COLS)),
        ],
        out_specs=[
            pl.BlockSpec((1, tm, NAT_COLS), lambda bi, i: (bi, i, 0)),
            pl.BlockSpec((1, tm, GATE_COLS), lambda bi, i: (bi, i, 0)),
            pl.BlockSpec((1, d1, tm // d1, GRP_COLS), lambda bi, i: (bi, 0, i, 0)),
            pl.BlockSpec((1, d2, tm // d2, GRP_COLS), lambda bi, i: (bi, 0, i, 0)),
        ],
        out_shape=[
            jax.ShapeDtypeStruct((b, s, NAT_COLS), BF16),
            jax.ShapeDtypeStruct((b, s, GATE_COLS), BF16),
            jax.ShapeDtypeStruct((b, d1, s // d1, GRP_COLS), BF16),
            jax.ShapeDtypeStruct((b, d2, s // d2, GRP_COLS), BF16),
        ],
        scratch_shapes=[pltpu.VMEM((D_MODEL // LANES, tm, LANES), F32)],
        compiler_params=_params(("parallel", "parallel")),
        name="norm_proj",
    )(x, g, w_nat, w_gate, w_d1, w_d2)


def _diff_bias_tiles(rel_bias, tq, tk):
    ratio, w = tq // tk, tq + tk
    t = jnp.arange(-2, ratio + 2)[:, None]
    m = jnp.arange(w)[None, :]
    rel = t * tk + jnp.where(m < tq, -m, w - m)
    tiles = _bias_of_rel(rel_bias[:, :H_DIFF], rel) * LOG2E
    far = _bias_of_rel(rel_bias[:, :H_DIFF], jnp.asarray([-MAX_DISTANCE, MAX_DISTANCE])) * LOG2E
    return tiles, far


def _diff_attn_kernel(lam_ref, far_ref, q_ref, k_ref, v_ref, brow_ref, g_ref, o_ref,
                      vt_ref, st_ref, p_ref, acc_ref, bias_ref, *, tq, tk, nkb):
    h = pl.program_id(0)
    qi = pl.program_id(2)
    ratio = tq // tk
    n_band = ratio + 2
    vchunk = min(512, nkb * tk)

    @pl.when((pl.program_id(1) == 0) & (qi == 0))
    def _():
        for t in range(ratio + 4):
            rows = jnp.broadcast_to(brow_ref[0, t:t + 1, :], (tk, tq + tk))
            bias_ref[t] = pltpu.roll(rows, 0, 1, stride=1, stride_axis=0)[:, :tq]

    @pl.when(qi == 0)
    def _():
        for c in range(nkb * tk // vchunk):
            cs = slice(c * vchunk, (c + 1) * vchunk)
            vt_ref[:LANES, cs] = v_ref[0, cs, :].astype(F32).T.astype(BF16)
        vt_ref[LANES:, :] = jnp.ones((DEN_ROWS, nkb * tk), BF16)

    qt = q_ref[0].astype(F32).T
    first = lax.broadcasted_iota(jnp.int32, (LANES, tq), 0) < HEAD_DIM
    qts = (jnp.where(first, qt, 0.0).astype(BF16), jnp.where(first, 0.0, qt).astype(BF16))
    kb0 = qi * ratio

    def key_block(j):
        kb = kb0 - 1 + j
        return jnp.where(kb < 0, kb + nkb, jnp.where(kb >= nkb, kb - nkb, kb))

    def scores(j, slot, band):
        kb = key_block(j)
        ks = pl.multiple_of(kb * tk, tk)
        k = k_ref[0, pl.ds(ks, tk), :]
        if band:
            bt = bias_ref[jnp.clip(kb - kb0, -2, ratio + 1) + 2]
        mx = []
        for mp in range(2):
            st = jnp.dot(k, qts[mp], preferred_element_type=F32)
            if band:
                st = st + bt
            st_ref[slot, mp] = st
            mx.append(jnp.max(st, axis=0, keepdims=True))
        return tuple(mx)

    def softmax(j, slot, stats, mx, band):
        if not band:
            kb = key_block(j)
            far_c = jnp.where(kb < kb0, far_ref[h, 0], far_ref[h, 1])
        out, alphas = [], []
        for mp in range(2):
            m = stats[mp]
            if band:
                m_new = jnp.maximum(m, mx[mp])
                shift = m_new
            else:
                m_new = jnp.maximum(m, mx[mp] + far_c)
                shift = m_new - far_c
            p_ref[slot, mp] = jnp.exp2((st_ref[slot, mp] - shift).astype(BF16))
            out.append(m_new)
            alphas.append(jnp.exp2(m - m_new))
        return tuple(out), tuple(alphas)

    def values(j, slot, alphas):
        ks = pl.multiple_of(key_block(j) * tk, tk)
        vt = vt_ref[:, pl.ds(ks, tk)]
        for mp in range(2):
            acc_ref[mp] = alphas[mp] * acc_ref[mp] + jnp.dot(vt, p_ref[slot, mp],
                                                             preferred_element_type=F32)

    acc_ref[...] = jnp.zeros_like(acc_ref)
    mx = scores(0, 0, True)
    stats = (jnp.full((1, tq), -jnp.inf, F32),) * 2
    alphas = None
    for j in range(nkb):
        if j + 1 < nkb:
            mx_next = scores(j + 1, (j + 1) % DIFF_SLOTS, j + 1 < n_band)
        if j >= 1:
            values(j - 1, (j - 1) % DIFF_SLOTS, alphas)
        stats, alphas = softmax(j, j % DIFF_SLOTS, stats, mx, j < n_band)
        mx = mx_next
    values(nkb - 1, (nkb - 1) % DIFF_SLOTS, alphas)
    l0, l1 = acc_ref[0, LANES:LANES + 1, :], acc_ref[1, LANES:LANES + 1, :]

    lp = lam_ref[...]
    lam = (jnp.exp(jnp.sum(lp[0:1] * lp[1:2], axis=-1, keepdims=True))
           - jnp.exp(jnp.sum(lp[2:3] * lp[3:4], axis=-1, keepdims=True)) + LAM_INIT)
    ot = acc_ref[0, :LANES, :] / l0 - lam * (acc_ref[1, :LANES, :] / l1)
    ms = jnp.mean(ot * ot, axis=0, keepdims=True)
    y = (ot * lax.rsqrt(ms + EPS)).T * g_ref[...] * (1.0 - LAM_INIT)
    o_ref[0] = y.astype(BF16)


def _diff_attn(nat, bias_tiles, far, lam_params, subln_g):
    b, s, _ = nat.shape
    tq, tk = min(TQ_DIFF, s), min(TK_DIFF, s)
    nt = bias_tiles.shape[1]
    ratio, nkb = tq // tk, s // tk
    assert tq % tk == 0 and tk >= MAX_DISTANCE and nkb >= ratio + 2
    assert DIFF_SLOTS >= 3
    kern = functools.partial(_diff_attn_kernel, tq=tq, tk=tk, nkb=s // tk)
    return pl.pallas_call(
        kern,
        grid=(H_DIFF, b, s // tq),
        in_specs=[
            _const_spec((4, HEAD_DIM)),
            pl.BlockSpec(memory_space=pltpu.SMEM),
            pl.BlockSpec((1, tq, LANES), lambda h, bi, qi: (bi, qi, h)),
            pl.BlockSpec((1, s, LANES), lambda h, bi, qi: (bi, 0, H_DIFF + h)),
            pl.BlockSpec((1, s, LANES), lambda h, bi, qi: (bi, 0, 2 * H_DIFF + h)),
            pl.BlockSpec((1, nt, tq + tk), lambda h, bi, qi: (h, 0, 0)),
            _const_spec((1, 2 * HEAD_DIM)),
        ],
        out_specs=pl.BlockSpec((1, tq, LANES), lambda h, bi, qi: (bi, qi, h)),
        out_shape=jax.ShapeDtypeStruct((b, s, DIFF_V), BF16),
        scratch_shapes=[
            pltpu.VMEM((LANES + DEN_ROWS, s), BF16),
            pltpu.VMEM((DIFF_SLOTS, 2, tk, tq), F32),
            pltpu.VMEM((DIFF_SLOTS, 2, tk, tq), BF16),
            pltpu.VMEM((2, LANES + DEN_ROWS, tq), F32),
            pltpu.VMEM((nt, tk, tq), F32),
        ],
        compiler_params=_params(("arbitrary", "arbitrary", "arbitrary")),
        name="diff_attn",
    )(lam_params, far, nat, nat, nat, bias_tiles, subln_g)


def _dil_geometry(s, window, dilation):
    l = s // dilation
    w = window // (2 * dilation)
    qb = l if l <= 2 * DIL_QB else DIL_QB
    kw = min(qb + 2 * LANES, l)
    assert w <= LANES and l % qb == 0 and l % LANES == 0
    return l, w, qb, kw, l // qb


def _dil_bias_tiles(rel_bias, g, s, window, dilation):
    l, w, qb, kw, nb = _dil_geometry(s, window, dilation)
    c0 = H_DIFF + g * H_DIL
    shifts = []
    for n in (0, min(1, nb - 1), nb - 1):
        l0 = n * qb
        shifts.append(min(max(l0 - LANES, 0), l - kw) - l0)
    j = jnp.arange(kw)[None, :, None]
    i = jnp.arange(qb)[None, None, :]
    off = jnp.asarray(shifts)[:, None, None] + j - i
    b = _bias_of_rel(rel_bias[:, c0:c0 + H_DIL], off * dilation) * LOG2E
    b = jnp.where((jnp.abs(off) <= w)[None], b, NEG_INF)
    return jnp.stack([jnp.concatenate([b[2 * p], b[2 * p + 1]], axis=-1) for p in range(H_DIL // 2)], axis=1)


def _dil_attn_kernel(q_ref, k_ref, v_ref, bias_ref, o_ref, lse_ref, qt_ref, vt_ref, st_ref,
                     *, l, qb, kw, nb, cps):
    chunk = min(512, l)
    first = lax.broadcasted_iota(jnp.int32, (LANES, qb), 0) < HEAD_DIM

    def transposes(c):
        for pair in range(H_DIL // 2):
            cs = slice(pair * LANES, (pair + 1) * LANES)
            for i in range(l // chunk):
                rs = slice(i * chunk, (i + 1) * chunk)
                qt_ref[c, pair, :, rs] = q_ref[0, c, rs, cs].astype(F32).T.astype(BF16)
                vt_ref[c, pair, :LANES, rs] = v_ref[0, c, rs, cs].astype(F32).T.astype(BF16)
            vt_ref[c, pair, LANES:, :] = jnp.ones((DEN_ROWS, l), BF16)

    def window(n):
        l0 = n * qb
        return l0, min(max(l0 - LANES, 0), l - kw)

    def scores(c, n, slot):
        l0, ws = window(n)
        for pair in range(H_DIL // 2):
            qt = qt_ref[c, pair, :, l0:l0 + qb]
            q2 = jnp.concatenate([jnp.where(first, qt, jnp.zeros_like(qt)),
                                  jnp.where(first, jnp.zeros_like(qt), qt)], axis=1)
            kp = k_ref[0, c, ws:ws + kw, pair * LANES:(pair + 1) * LANES]
            st_ref[c, slot, pair] = jnp.dot(kp, q2, preferred_element_type=F32)

    def finish(c, n, slot):
        l0, ws = window(n)
        t = 0 if n == 0 else (2 if n == nb - 1 else 1)
        for pair in range(H_DIL // 2):
            cs = slice(pair * LANES, (pair + 1) * LANES)
            st = st_ref[c, slot, pair] + bias_ref[t, pair]
            mx = jnp.max(st, axis=0, keepdims=True)
            p = jnp.exp2(st - mx).astype(BF16)
            acc = jnp.dot(vt_ref[c, pair, :, ws:ws + kw], p, preferred_element_type=F32)
            den = acc[LANES:LANES + 1, :]
            ot = acc[:LANES, :] / den
            lse = mx * LN2 + jnp.log(den)
            o_t = jnp.where(first, ot[:, :qb], ot[:, qb:])
            lse_t = jnp.where(first, lse[:, :qb], lse[:, qb:])
            o_ref[0, c, l0:l0 + qb, cs] = o_t.T.astype(BF16)
            lse_ref[0, c, l0:l0 + qb, cs] = lse_t.T

    for c in range(cps):
        transposes(c)
    for c in range(cps):
        scores(c, 0, 0)
    for c in range(cps):
        for n in range(nb):
            if n + 1 < nb:
                scores(c, n + 1, (n + 1) % 2)
            finish(c, n, n % 2)


def _dil_attn(arr, col0, bias_tiles, g, window, dilation):
    b, d, l, _ = arr.shape
    _, w, qb, kw, nb = _dil_geometry(l * d, window, dilation)
    cps = max(1, min(d, DIL_STEP_ROWS // l))
    kern = functools.partial(_dil_attn_kernel, l=l, qb=qb, kw=kw, nb=nb, cps=cps)
    return pl.pallas_call(
        kern,
        grid=(b, d // cps),
        in_specs=[
            pl.BlockSpec((1, cps, l, GRP_W), lambda bi, r: (bi, r, 0, col0)),
            pl.BlockSpec((1, cps, l, GRP_W), lambda bi, r: (bi, r, 0, col0 + 1)),
            pl.BlockSpec((1, cps, l, GRP_W), lambda bi, r: (bi, r, 0, col0 + 2)),
            _const_spec((3, H_DIL // 2, kw, 2 * qb)),
        ],
        out_specs=[
            pl.BlockSpec((1, cps, l, GRP_W), lambda bi, r: (bi, r, 0, 0)),
            pl.BlockSpec((1, cps, l, GRP_W), lambda bi, r: (bi, r, 0, 0)),
        ],
        out_shape=[
            jax.ShapeDtypeStruct((b, d, l, GRP_W), BF16),
            jax.ShapeDtypeStruct((b, d, l, GRP_W), F32),
        ],
        scratch_shapes=[
            pltpu.VMEM((cps, H_DIL // 2, LANES, l), BF16),
            pltpu.VMEM((cps, H_DIL // 2, LANES + DEN_ROWS, l), BF16),
            pltpu.VMEM((cps, 2, H_DIL // 2, kw, 2 * qb), F32),
        ],
        compiler_params=_params(("parallel", "parallel")),
        name=f"dil_attn_g{g}",
    )(arr, arr, arr, bias_tiles)


def _pack_words(a, b):
    ua = lax.bitcast_convert_type(a.astype(BF16).astype(F32), jnp.uint32)
    ub = lax.bitcast_convert_type(b.astype(BF16).astype(F32), jnp.uint32)
    return (ua >> 16) | (ub & jnp.uint32(0xFFFF0000))


def _unpack_words(w):
    return (lax.bitcast_convert_type(w << 16, F32),
            lax.bitcast_convert_type(w & jnp.uint32(0xFFFF0000), F32))


def _route(lg):
    lane = lax.broadcasted_iota(jnp.int32, lg.shape, 1)
    big = jnp.int32(ROUTE_LANES)
    gl = jnp.where(lane < N_GROUPS, lg, -jnp.inf)
    gmax = jnp.max(gl, axis=-1, keepdims=True)
    gi = jnp.min(jnp.where(gl == gmax, lane, big), axis=-1, keepdims=True)
    gw = 1.0 / jnp.sum(jnp.exp(gl - gmax), axis=-1, keepdims=True)
    e_lo = EXP_LANE0 + E_PER_GROUP * gi
    es = jnp.where((lane >= e_lo) & (lane < e_lo + E_PER_GROUP), lg, -jnp.inf)
    v1 = jnp.max(es, axis=-1, keepdims=True)
    i1 = jnp.min(jnp.where(es == v1, lane, big), axis=-1, keepdims=True)
    es2 = jnp.where(lane == i1, -jnp.inf, es)
    v2 = jnp.max(es2, axis=-1, keepdims=True)
    i2 = jnp.min(jnp.where(es2 == v2, lane, big), axis=-1, keepdims=True)
    t = jnp.exp(v2 - v1)
    w1 = gw / (1.0 + t)
    w2 = w1 * t
    return gi, jnp.where(lane == i1, w1, 0.0) + jnp.where(lane == i2, w2, 0.0)


def _merge_route_kernel(x_ref, oa_ref, o0_ref, l0_ref, o1_ref, l1_ref, o2_ref, l2_ref, gt_ref,
                        wa_ref, wb_ref, wo_ref, nf_ref, wr_ref, x1_ref, hw_ref, gid_ref,
                        so1_ref, sl1_ref, so2_ref, sl2_ref, *, tm):
    for src, dst, d in ((o1_ref, so1_ref, DIL_CONFIGS[1][1]), (l1_ref, sl1_ref, DIL_CONFIGS[1][1]),
                        (o2_ref, so2_ref, DIL_CONFIGS[2][1]), (l2_ref, sl2_ref, DIL_CONFIGS[2][1])):
        for r in range(d):
            for c in range(GRP_W // LANES):
                dst[c, pl.ds(r, tm // d, stride=d), :] = src[0, r, :, c * LANES:(c + 1) * LANES].astype(F32)
    subs = [slice(i * SUB_MERGE, (i + 1) * SUB_MERGE) for i in range(tm // SUB_MERGE)]
    unsplit = lambda ref, rs: jnp.concatenate([ref[c, rs, :] for c in range(GRP_W // LANES)], axis=1)

    def dil_mix(rs):
        l0, l1, l2 = l0_ref[0, 0, rs, :], unsplit(sl1_ref, rs), unsplit(sl2_ref, rs)
        mx = jnp.maximum(jnp.maximum(l0, l1), l2)
        e0, e1, e2 = jnp.exp(l0 - mx), jnp.exp(l1 - mx), jnp.exp(l2 - mx)
        ob = (e0 * o0_ref[0, 0, rs, :].astype(F32) + e1 * unsplit(so1_ref, rs)
              + e2 * unsplit(so2_ref, rs)) / (e0 + e1 + e2)
        return ob.astype(BF16)

    obs = [dil_mix(rs) for rs in subs]
    pas = [jnp.dot(oa_ref[0, rs, :], wa_ref[...], preferred_element_type=F32) for rs in subs]
    pbs = [jnp.dot(ob, wb_ref[...], preferred_element_type=F32) for ob in obs]

    def gate(rs, pa, pb):
        ga = jax.nn.sigmoid(gt_ref[0, rs, :D_MODEL].astype(F32))
        gb = jax.nn.sigmoid(gt_ref[0, rs, D_MODEL:].astype(F32))
        return (ga * pa + gb * pb).astype(BF16)

    mgs = [gate(rs, pa, pb) for rs, pa, pb in zip(subs, pas, pbs)]
    x1s = [x_ref[0, rs, :] + jnp.dot(mg, wo_ref[...], preferred_element_type=F32)
           for rs, mg in zip(subs, mgs)]

    def ffn_norm(rs, x1):
        x1_ref[0, rs, :] = x1
        ms = jnp.mean(x1 * x1, axis=-1, keepdims=True)
        h2f = x1 * lax.rsqrt(ms + EPS) * nf_ref[...]
        hw_ref[0, rs, :] = _pack_words(h2f[:, :D_MODEL // 2], h2f[:, D_MODEL // 2:])
        return h2f.astype(BF16)

    h2s = [ffn_norm(rs, x1) for rs, x1 in zip(subs, x1s)]
    lgs = [jnp.dot(h2, wr_ref[...], preferred_element_type=F32) for h2 in h2s]
    for rs, lg in zip(subs, lgs):
        gi, _ = _route(lg)
        git = jnp.broadcast_to(gi.astype(F32), lg.shape).T
        gid_ref[0, 0, :, rs] = git[:SUBLANES, :].astype(jnp.int32)


def _merge_route(x, oa, dil, gates, wa, wb, wo, nf, wr):
    b, s, _ = x.shape
    tm = TM_MERGE
    d1, d2 = DIL_CONFIGS[1][1], DIL_CONFIGS[2][1]
    tok = lambda bi, i: (bi, i, 0)
    cls = lambda bi, i: (bi, 0, i, 0)
    (o0, s0), (o1, s1), (o2, s2) = dil
    return pl.pallas_call(
        functools.partial(_merge_route_kernel, tm=tm),
        grid=(b, s // tm),
        in_specs=[
            pl.BlockSpec((1, tm, D_MODEL), tok),
            pl.BlockSpec((1, tm, DIFF_V), tok),
            pl.BlockSpec((1, 1, tm, GRP_W), cls), pl.BlockSpec((1, 1, tm, GRP_W), cls),
            pl.BlockSpec((1, d1, tm // d1, GRP_W), cls), pl.BlockSpec((1, d1, tm // d1, GRP_W), cls),
            pl.BlockSpec((1, d2, tm // d2, GRP_W), cls), pl.BlockSpec((1, d2, tm // d2, GRP_W), cls),
            pl.BlockSpec((1, tm, GATE_COLS), tok),
            _const_spec((DIFF_V, D_MODEL)),
            _const_spec((GRP_W, D_MODEL)),
            _const_spec((D_MODEL, D_MODEL)),
            _const_spec((1, D_MODEL)),
            _const_spec((D_MODEL, ROUTE_LANES)),
        ],
        out_specs=[
            pl.BlockSpec((1, tm, D_MODEL), tok),
            pl.BlockSpec((1, tm, D_MODEL // 2), tok),
            pl.BlockSpec((1, 1, SUBLANES, tm), lambda bi, i: (bi, i, 0, 0)),
        ],
        out_shape=[
            jax.ShapeDtypeStruct((b, s, D_MODEL), F32),
            jax.ShapeDtypeStruct((b, s, D_MODEL // 2), jnp.uint32),
            jax.ShapeDtypeStruct((b, s // tm, SUBLANES, tm), jnp.int32),
        ],
        scratch_shapes=[pltpu.VMEM((GRP_W // LANES, tm, LANES), F32) for _ in range(4)],
        compiler_params=_params(("parallel", "parallel")),
        name="merge_route",
    )(x, oa, o0, s0, o1, s1, o2, s2, gates, wa, wb, wo, nf, wr)


def _dispatch_plan_kernel(gid_ref, dest_ref, tgrp_ref, *, rows, cols):
    gid = gid_ref[...]
    ci = lax.broadcasted_iota(jnp.int32, (cols, cols), 0)
    cj = lax.broadcasted_iota(jnp.int32, (cols, cols), 1)
    upto = (ci <= cj).astype(BF16)
    ri = lax.broadcasted_iota(jnp.int32, (rows, rows), 0)
    rj = lax.broadcasted_iota(jnp.int32, (rows, rows), 1)
    above = (rj < ri).astype(F32)
    seg = jnp.zeros((1, 1), F32)
    dest = jnp.zeros((rows, cols), F32)
    ends = []
    for g in range(N_GROUPS):
        mem = (gid == g).astype(F32)
        pref = jnp.dot(mem.astype(BF16), upto, preferred_element_type=F32)
        tot = pref[:, cols - 1:cols]
        before = jnp.dot(above, jnp.broadcast_to(tot, (rows, LANES)), preferred_element_type=F32,
                         precision=lax.Precision.HIGHEST)[:, :1]
        dest = dest + mem * (seg + before + pref - 1.0)
        cnt = jnp.sum(tot, axis=0, keepdims=True)
        seg = seg + jnp.ceil(cnt / TM_MOE) * TM_MOE
        ends.append(seg)
    dest_ref[...] = dest.astype(jnp.int32)
    tile = (lax.broadcasted_iota(jnp.int32, (SUBLANES, LANES), 0) * LANES
            + lax.broadcasted_iota(jnp.int32, (SUBLANES, LANES), 1))
    start = (tile * TM_MOE).astype(F32)
    tgrp = jnp.zeros((SUBLANES, LANES), jnp.int32)
    for g in range(N_GROUPS - 1):
        tgrp = tgrp + (start >= ends[g]).astype(jnp.int32)
    tgrp_ref[...] = tgrp


def _dispatch_plan(gid):
    rows, cols = gid.shape
    assert rows % SUBLANES == 0 and rows * cols // TM_MOE + N_GROUPS <= SUBLANES * LANES
    return pl.pallas_call(
        functools.partial(_dispatch_plan_kernel, rows=rows, cols=cols),
        out_shape=[jax.ShapeDtypeStruct((rows, cols), jnp.int32),
                   jax.ShapeDtypeStruct((SUBLANES, LANES), jnp.int32)],
        compiler_params=pltpu.CompilerParams(vmem_limit_bytes=VMEM_LIMIT),
        name="dispatch_plan",
    )(gid)


def _moe_group_kernel(tgrp_ref, hw_ref, wr_ref, wg_ref, wu_ref, wd_ref, ow_ref, he_ref):
    g = tgrp_ref[pl.program_id(0)]
    lo, hi = _unpack_words(hw_ref[...])
    h2 = jnp.concatenate([lo, hi], axis=1).astype(BF16)
    _, dw = _route(jnp.dot(h2, wr_ref[...], preferred_element_type=F32))
    lane = lax.broadcasted_iota(jnp.int32, dw.shape, 1)
    for e in range(E_PER_GROUP):
        we = jnp.sum(jnp.where(lane == EXP_LANE0 + E_PER_GROUP * g + e, dw, 0.0), axis=-1, keepdims=True)
        gp = jnp.dot(h2, wg_ref[0, e], preferred_element_type=F32)
        up = jnp.dot(h2, wu_ref[0, e], preferred_element_type=F32)
        he_ref[:, e * D_EXPERT:(e + 1) * D_EXPERT] = (jax.nn.silu(gp) * up * we).astype(BF16)
    moe = jnp.dot(he_ref[...], wd_ref[0], preferred_element_type=F32)
    ow_ref[...] = _pack_words(moe[:, :D_MODEL // 2], moe[:, D_MODEL // 2:])


def _moe_groups(tgrp, hw_sorted, wr, wg, wu, wd):
    m = hw_sorted.shape[0]
    tm = TM_MOE
    row = lambda i, t: (i, 0)
    grp = lambda i, t: (t[i], 0, 0, 0)
    return pl.pallas_call(
        _moe_group_kernel,
        grid_spec=pltpu.PrefetchScalarGridSpec(
            num_scalar_prefetch=1,
            grid=(m // tm,),
            in_specs=[
                pl.BlockSpec((tm, D_MODEL // 2), row),
                pl.BlockSpec((D_MODEL, ROUTE_LANES), lambda i, t: (0, 0)),
                pl.BlockSpec((1, E_PER_GROUP, D_MODEL, D_EXPERT), grp),
                pl.BlockSpec((1, E_PER_GROUP, D_MODEL, D_EXPERT), grp),
                pl.BlockSpec((1, E_PER_GROUP * D_EXPERT, D_MODEL), lambda i, t: (t[i], 0, 0)),
            ],
            out_specs=pl.BlockSpec((tm, D_MODEL // 2), row),
            scratch_shapes=[pltpu.VMEM((tm, E_PER_GROUP * D_EXPERT), BF16)],
        ),
        out_shape=jax.ShapeDtypeStruct((m, D_MODEL // 2), jnp.uint32),
        compiler_params=_params(("arbitrary",)),
        name="moe_groups",
    )(tgrp, hw_sorted, wr, wg, wu, wd)


def _final_norm_kernel(x1_ref, yw_ref, nf_ref, y_ref):
    lo, hi = _unpack_words(yw_ref[...])
    x2 = x1_ref[...] + jnp.concatenate([lo, hi], axis=1)
    ms = jnp.mean(x2 * x2, axis=-1, keepdims=True)
    y_ref[...] = x2 * lax.rsqrt(ms + EPS) * nf_ref[...]


def _final_norm(x1, yw, nf):
    n = x1.shape[0]
    tm = TM_FINAL
    row = lambda i: (i, 0)
    return pl.pallas_call(
        _final_norm_kernel,
        grid=(n // tm,),
        in_specs=[pl.BlockSpec((tm, D_MODEL), row), pl.BlockSpec((tm, D_MODEL // 2), row),
                  _const_spec((1, D_MODEL))],
        out_specs=pl.BlockSpec((tm, D_MODEL), row),
        out_shape=jax.ShapeDtypeStruct((n, D_MODEL), F32),
        compiler_params=_params(("parallel",)),
        name="final_norm",
    )(x1, yw, nf)


def _sc_mesh():
    return plsc.VectorSubcoreMesh(core_axis_name="core", subcore_axis_name="subcore")


def _sc_scatter_rows(x, idx, m):
    n, d = x.shape

    @pl.kernel(out_type=jax.ShapeDtypeStruct((m, d), x.dtype), mesh=_sc_mesh(), scratch_types=[])
    def scatter(x_hbm, i_hbm, o_hbm):
        for c in range(d // SC_COLS):
            cols = pl.ds(c * SC_COLS, SC_COLS)
            o_cols = o_hbm.at[:, cols]

            def body(x_vmem, i_vmem, o_cols=o_cols):
                pltpu.sync_copy(x_vmem, o_cols.at[i_vmem.at[0]])

            pltpu.emit_pipeline(
                body,
                grid=(n // SC_WINDOW,),
                in_specs=[pl.BlockSpec((SC_WINDOW, SC_COLS), lambda i, c=c: (i, c)),
                          pl.BlockSpec((1, SC_WINDOW), lambda i: (0, i))],
                out_specs=[],
                core_axis_name=("core", "subcore"),
                dimension_semantics=(pltpu.PARALLEL,),
            )(x_hbm, i_hbm)

    return scatter(x, idx.reshape(1, n))


def _sc_gather_rows(x, idx):
    n = idx.shape[0]
    d = x.shape[1]

    @pl.kernel(out_type=jax.ShapeDtypeStruct((n, d), x.dtype), mesh=_sc_mesh(), scratch_types=[])
    def gather(x_hbm, i_hbm, o_hbm):
        for c in range(d // SC_COLS):
            x_cols = x_hbm.at[:, pl.ds(c * SC_COLS, SC_COLS)]

            def body(i_vmem, o_vmem, x_cols=x_cols):
                pltpu.sync_copy(x_cols.at[i_vmem.at[0]], o_vmem)

            pltpu.emit_pipeline(
                body,
                grid=(n // SC_WINDOW,),
                in_specs=[pl.BlockSpec((1, SC_WINDOW), lambda i: (0, i))],
                out_specs=[pl.BlockSpec((SC_WINDOW, SC_COLS), lambda i, c=c: (i, c))],
                core_axis_name=("core", "subcore"),
                dimension_semantics=(pltpu.PARALLEL,),
            )(i_hbm, o_hbm)

    return gather(x, idx.reshape(1, n))


def _prepare_weights(rel_bias, norm_mix, w_in, lam_q1, lam_k1, lam_q2, lam_k2, subln_g,
                     w_branch_a, w_branch_b, w_out, norm_ffn, w_group, w_router,
                     w_exp_gate, w_exp_up, w_exp_down, norm_final):
    w = w_in[0]
    c = QK_SCALE * LOG2E
    o = 0
    qa, o = w[:, o:o + DIFF_QK] * c, o + DIFF_QK
    ka, o = w[:, o:o + DIFF_QK], o + DIFF_QK
    va, o = w[:, o:o + DIFF_V], o + DIFF_V
    qb, o = w[:, o:o + DIL_W] * c, o + DIL_W
    kb, o = w[:, o:o + DIL_W], o + DIL_W
    vb, o = w[:, o:o + DIL_W], o + DIL_W
    w_gate = w[:, o:].astype(BF16)
    grp = lambda g: [t[:, g * GRP_W:(g + 1) * GRP_W] for t in (qb, kb, vb)]
    w_nat = jnp.concatenate([qa, ka, va] + grp(0), axis=1).astype(BF16)
    w_d1 = jnp.concatenate(grp(1), axis=1).astype(BF16)
    w_d2 = jnp.concatenate(grp(2), axis=1).astype(BF16)
    wr = jnp.concatenate([w_group[0], w_router[0].transpose(1, 0, 2).reshape(D_MODEL, N_EXPERTS)], axis=1)
    wr = jnp.pad(wr, ((0, 0), (0, ROUTE_LANES - wr.shape[1]))).astype(BF16)
    return dict(
        rel_bias=rel_bias.astype(F32),
        norm_mix=norm_mix[0][None].astype(F32),
        w_nat=w_nat, w_gate=w_gate, w_d1=w_d1, w_d2=w_d2,
        lam=jnp.stack([lam_q1[0], lam_k1[0], lam_q2[0], lam_k2[0]]).astype(F32),
        subln_g=subln_g[0][None].astype(F32),
        wa=w_branch_a[0].astype(BF16), wb=w_branch_b[0].astype(BF16), wo=w_out[0].astype(BF16),
        norm_ffn=norm_ffn[0][None].astype(F32), wr=wr,
        wg=w_exp_gate[0].reshape(N_GROUPS, E_PER_GROUP, D_MODEL, D_EXPERT).astype(BF16),
        wu=w_exp_up[0].reshape(N_GROUPS, E_PER_GROUP, D_MODEL, D_EXPERT).astype(BF16),
        wd=w_exp_down[0].reshape(N_GROUPS, E_PER_GROUP * D_EXPERT, D_MODEL).astype(BF16),
        norm_final=norm_final[None].astype(F32),
    )


def _encoder(x, p):
    b, s, d = x.shape
    n = b * s
    nat, gates, dil1, dil2 = _norm_proj(x, p["norm_mix"], p["w_nat"], p["w_gate"], p["w_d1"], p["w_d2"])
    tiles, far = _diff_bias_tiles(p["rel_bias"], min(TQ_DIFF, s), min(TK_DIFF, s))
    oa = _diff_attn(nat, tiles, far, p["lam"], p["subln_g"])
    srcs = ((nat.reshape(b, 1, s, NAT_COLS), (2 * DIFF_QK + DIFF_V) // GRP_W), (dil1, 0), (dil2, 0))
    dil = []
    for g, (win, dilation) in enumerate(DIL_CONFIGS):
        tiles = _dil_bias_tiles(p["rel_bias"], g, s, win, dilation)
        dil.append(_dil_attn(srcs[g][0], srcs[g][1], tiles, g, win, dilation))
    x1, hw, gid = _merge_route(x, oa, dil, gates, p["wa"], p["wb"], p["wo"], p["norm_ffn"], p["wr"])
    dest, tgrp = _dispatch_plan(gid[:, :, 0, :].reshape(n // TM_MERGE, TM_MERGE))
    dest = dest.reshape(n)
    n_tiles = n // TM_MOE + N_GROUPS
    hw_sorted = _sc_scatter_rows(hw.reshape(n, d // 2), dest, n_tiles * TM_MOE)
    yw_sorted = _moe_groups(tgrp.reshape(-1)[:n_tiles], hw_sorted, p["wr"], p["wg"], p["wu"], p["wd"])
    y = _final_norm(x1.reshape(n, d), _sc_gather_rows(yw_sorted, dest), p["norm_final"])
    return y.reshape(b, s, d)


def kernel(x_prompt, x_sample, rel_bias, norm_mix, w_in, lam_q1, lam_k1, lam_q2, lam_k2, subln_g,
           w_branch_a, w_branch_b, w_out, norm_ffn, w_group, w_router, w_exp_gate, w_exp_up,
           w_exp_down, norm_final):
    p = _prepare_weights(rel_bias, norm_mix, w_in, lam_q1, lam_k1, lam_q2, lam_k2, subln_g,
                         w_branch_a, w_branch_b, w_out, norm_ffn, w_group, w_router,
                         w_exp_gate, w_exp_up, w_exp_down, norm_final)
    return (_encoder(x_prompt, p), _encoder(x_sample, p))
```

```python
import functools
import math

import jax
import jax.numpy as jnp
from jax import lax
from jax.experimental import pallas as pl
from jax.experimental.pallas import tpu as pltpu
from jax.experimental.pallas import tpu_sc as plsc

F32 = jnp.float32
BF16 = jnp.bfloat16

D_MODEL = 1024
HEAD_DIM = 64
H_DIFF = 4
H_DIL = 4
DIL_CONFIGS = ((128, 1), (512, 4), (2048, 16))
N_DIL = len(DIL_CONFIGS)
DIFF_QK = H_DIFF * 2 * HEAD_DIM
DIFF_V = H_DIFF * 2 * HEAD_DIM
DIL_W = N_DIL * H_DIL * HEAD_DIM
GRP_W = H_DIL * HEAD_DIM
GRP_COLS = 3 * GRP_W
NAT_COLS = 2 * DIFF_QK + DIFF_V + GRP_COLS
GATE_COLS = 2 * D_MODEL
N_BUCKETS = 32
MAX_DISTANCE = 128
N_GROUPS = 4
E_PER_GROUP = 4
N_EXPERTS = N_GROUPS * E_PER_GROUP
D_EXPERT = 256
EPS = 1e-6
NEG_INF = -1e30
LAM_INIT = 0.8 - 0.6 * math.exp(-0.3 * 0)
LOG2E = math.log2(math.e)
LN2 = math.log(2.0)
QK_SCALE = HEAD_DIM ** -0.5

LANES = 128
SUBLANES = 8
VMEM_LIMIT = 56 * 1024 * 1024

TM_PROJ = 512
TQ_DIFF = 1024
TK_DIFF = 256
DIFF_SLOTS = 4
DEN_ROWS = 16
DIL_QB = 128
DIL_STEP_ROWS = 2048
SC_WINDOW = 128
SC_COLS = 256
TM_MERGE = 512
SUB_MERGE = 256
TM_MOE = 512
TM_FINAL = 1024
ROUTE_LANES = 128
EXP_LANE0 = N_GROUPS


def _const_spec(shape):
    nd = len(shape)
    return pl.BlockSpec(shape, lambda *_: (0,) * nd, pipeline_mode=pl.Buffered(1))


def _params(sem):
    return pltpu.CompilerParams(dimension_semantics=sem, vmem_limit_bytes=VMEM_LIMIT)


def _rel_bucket(rel):
    half = N_BUCKETS // 2
    max_exact = half // 2
    ret = (rel > 0).astype(jnp.int32) * half
    n = jnp.abs(rel)
    nf = jnp.maximum(n, 1).astype(F32)
    large = max_exact + (jnp.log(nf / max_exact) / math.log(MAX_DISTANCE / max_exact)
                         * (half - max_exact)).astype(jnp.int32)
    large = jnp.minimum(large, half - 1)
    return ret + jnp.where(n < max_exact, n, large)


def _bias_of_rel(table, rel):
    bucket = _rel_bucket(rel)[None]
    t = table.astype(F32)
    col = (t.shape[1],) + (1,) * rel.ndim
    out = jnp.zeros((t.shape[1],) + rel.shape, F32)
    for k in range(N_BUCKETS):
        out = jnp.where(bucket == k, t[k].reshape(col), out)
    return out


def _chunks(total, width):
    out, c = [], 0
    while c < total:
        w = min(width, total - c)
        out.append((c, w))
        c += w
    return out


def _norm_proj_kernel(x_ref, g_ref, wn_ref, w1_ref, w2_ref,
                      on_ref, o1_ref, o2_ref, hs_ref, *, tm):
    x = x_ref[0]
    ms = jnp.mean(x * x, axis=-1, keepdims=True)
    hf = x * lax.rsqrt(ms + EPS) * g_ref[...]
    for c in range(D_MODEL // LANES):
        hs_ref[c] = hf[:, c * LANES:(c + 1) * LANES]
    h = hf.astype(BF16)
    for c0, cw in _chunks(NAT_COLS, 512):
        on_ref[0, :, c0:c0 + cw] = jnp.dot(h, wn_ref[:, c0:c0 + cw],
                                           preferred_element_type=F32).astype(BF16)
    for w_ref, o_ref, d in ((w1_ref, o1_ref, DIL_CONFIGS[1][1]), (w2_ref, o2_ref, DIL_CONFIGS[2][1])):
        rows = tm // d
        hp = jnp.concatenate(
            [jnp.concatenate([hs_ref[c, pl.ds(r, rows, stride=d), :] for r in range(d)], axis=0)
             for c in range(D_MODEL // LANES)], axis=1).astype(BF16)
        for c0, cw in _chunks(GRP_COLS, 256):
            res = jnp.dot(hp, w_ref[:, c0:c0 + cw], preferred_element_type=F32).astype(BF16)
            o_ref[0, :, :, c0:c0 + cw] = res.reshape(d, rows, cw)


def _norm_proj(x, g, w_nat, w_d1, w_d2):
    b, s, _ = x.shape
    tm = TM_PROJ
    d1, d2 = DIL_CONFIGS[1][1], DIL_CONFIGS[2][1]
    return pl.pallas_call(
        functools.partial(_norm_proj_kernel, tm=tm),
        grid=(b, s // tm),
        in_specs=[
            pl.BlockSpec((1, tm, D_MODEL), lambda bi, i: (bi, i, 0)),
            _const_spec((1, D_MODEL)),
            _const_spec((D_MODEL, NAT_COLS)),
            _const_spec((D_MODEL, GRP_COLS)),
            _const_spec((D_MODEL, GRP_COLS)),
        ],
        out_specs=[
            pl.BlockSpec((1, tm, NAT_COLS), lambda bi, i: (bi, i, 0)),
            pl.BlockSpec((1, d1, tm // d1, GRP_COLS), lambda bi, i: (bi, 0, i, 0)),
            pl.BlockSpec((1, d2, tm // d2, GRP_COLS), lambda bi, i: (bi, 0, i, 0)),
        ],
        out_shape=[
            jax.ShapeDtypeStruct((b, s, NAT_COLS), BF16),
            jax.ShapeDtypeStruct((b, d1, s // d1, GRP_COLS), BF16),
            jax.ShapeDtypeStruct((b, d2, s // d2, GRP_COLS), BF16),
        ],
        scratch_shapes=[pltpu.VMEM((D_MODEL // LANES, tm, LANES), F32)],
        compiler_params=_params(("parallel", "parallel")),
        name="norm_proj",
    )(x, g, w_nat, w_d1, w_d2)


def _diff_bias_tiles(rel_bias, tq, tk):
    ratio, w = tq // tk, tq + tk
    t = jnp.arange(-2, ratio + 2)[:, None]
    m = jnp.arange(w)[None, :]
    rel = t * tk + jnp.where(m < tq, -m, w - m)
    tiles = _bias_of_rel(rel_bias[:, :H_DIFF], rel) * LOG2E
    far = _bias_of_rel(rel_bias[:, :H_DIFF], jnp.asarray([-MAX_DISTANCE, MAX_DISTANCE])) * LOG2E
    return tiles, far


def _diff_attn_kernel(lam_ref, far_ref, q_ref, k_ref, v_ref, brow_ref, g_ref, o_ref,
                      vt_ref, st_ref, p_ref, acc_ref, bias_ref, *, tq, tk, nkb):
    h = pl.program_id(0)
    qi = pl.program_id(2)
    ratio = tq // tk
    n_band = ratio + 2
    vchunk = min(512, nkb * tk)

    @pl.when((pl.program_id(1) == 0) & (qi == 0))
    def _():
        for t in range(ratio + 4):
            rows = jnp.broadcast_to(brow_ref[0, t:t + 1, :], (tk, tq + tk))
            bias_ref[t] = pltpu.roll(rows, 0, 1, stride=1, stride_axis=0)[:, :tq]

    @pl.when(qi == 0)
    def _():
        for c in range(nkb * tk // vchunk):
            cs = slice(c * vchunk, (c + 1) * vchunk)
            vt_ref[:LANES, cs] = v_ref[0, cs, :].astype(F32).T.astype(BF16)
        vt_ref[LANES:, :] = jnp.ones((DEN_ROWS, nkb * tk), BF16)

    qt = q_ref[0].astype(F32).T
    first = lax.broadcasted_iota(jnp.int32, (LANES, tq), 0) < HEAD_DIM
    qts = (jnp.where(first, qt, 0.0).astype(BF16), jnp.where(first, 0.0, qt).astype(BF16))
    kb0 = qi * ratio

    def key_block(j):
        kb = kb0 - 1 + j
        return jnp.where(kb < 0, kb + nkb, jnp.where(kb >= nkb, kb - nkb, kb))

    def scores(j, slot, band):
        kb = key_block(j)
        ks = pl.multiple_of(kb * tk, tk)
        k = k_ref[0, pl.ds(ks, tk), :]
        if band:
            bt = bias_ref[jnp.clip(kb - kb0, -2, ratio + 1) + 2]
        mx = []
        for mp in range(2):
            st = jnp.dot(k, qts[mp], preferred_element_type=F32)
            if band:
                st = st + bt
            st_ref[slot, mp] = st
            mx.append(jnp.max(st, axis=0, keepdims=True))
        return tuple(mx)

    def softmax(j, slot, stats, mx, band):
        if not band:
            kb = key_block(j)
            far_c = jnp.where(kb < kb0, far_ref[h, 0], far_ref[h, 1])
        out, alphas = [], []
        for mp in range(2):
            m = stats[mp]
            if band:
                m_new = jnp.maximum(m, mx[mp])
                shift = m_new
            else:
                m_new = jnp.maximum(m, mx[mp] + far_c)
                shift = m_new - far_c
            p_ref[slot, mp] = jnp.exp2((st_ref[slot, mp] - shift).astype(BF16))
            out.append(m_new)
            alphas.append(jnp.exp2(m - m_new))
        return tuple(out), tuple(alphas)

    def values(j, slot, alphas):
        ks = pl.multiple_of(key_block(j) * tk, tk)
        vt = vt_ref[:, pl.ds(ks, tk)]
        for mp in range(2):
            acc_ref[mp] = alphas[mp] * acc_ref[mp] + jnp.dot(vt, p_ref[slot, mp],
                                                             preferred_element_type=F32)

    acc_ref[...] = jnp.zeros_like(acc_ref)
    mx = scores(0, 0, True)
    stats = (jnp.full((1, tq), -jnp.inf, F32),) * 2
    alphas = None
    for j in range(nkb):
        if j + 1 < nkb:
            mx_next = scores(j + 1, (j + 1) % DIFF_SLOTS, j + 1 < n_band)
        if j >= 1:
            values(j - 1, (j - 1) % DIFF_SLOTS, alphas)
        stats, alphas = softmax(j, j % DIFF_SLOTS, stats, mx, j < n_band)
        mx = mx_next
    values(nkb - 1, (nkb - 1) % DIFF_SLOTS, alphas)
    l0, l1 = acc_ref[0, LANES:LANES + 1, :], acc_ref[1, LANES:LANES + 1, :]

    lp = lam_ref[...]
    lam = (jnp.exp(jnp.sum(lp[0:1] * lp[1:2], axis=-1, keepdims=True))
           - jnp.exp(jnp.sum(lp[2:3] * lp[3:4], axis=-1, keepdims=True)) + LAM_INIT)
    ot = acc_ref[0, :LANES, :] / l0 - lam * (acc_ref[1, :LANES, :] / l1)
    ms = jnp.mean(ot * ot, axis=0, keepdims=True)
    y = (ot * lax.rsqrt(ms + EPS)).T * g_ref[...] * (1.0 - LAM_INIT)
    o_ref[0] = y.astype(BF16)


def _diff_attn(nat, bias_tiles, far, lam_params, subln_g):
    b, s, _ = nat.shape
    tq, tk = min(TQ_DIFF, s), min(TK_DIFF, s)
    nt = bias_tiles.shape[1]
    ratio, nkb = tq // tk, s // tk
    assert tq % tk == 0 and tk >= MAX_DISTANCE and nkb >= ratio + 2
    assert DIFF_SLOTS >= 3
    kern = functools.partial(_diff_attn_kernel, tq=tq, tk=tk, nkb=s // tk)
    return pl.pallas_call(
        kern,
        grid=(H_DIFF, b, s // tq),
        in_specs=[
            _const_spec((4, HEAD_DIM)),
            pl.BlockSpec(memory_space=pltpu.SMEM),
            pl.BlockSpec((1, tq, LANES), lambda h, bi, qi: (bi, qi, h)),
            pl.BlockSpec((1, s, LANES), lambda h, bi, qi: (bi, 0, H_DIFF + h)),
            pl.BlockSpec((1, s, LANES), lambda h, bi, qi: (bi, 0, 2 * H_DIFF + h)),
            pl.BlockSpec((1, nt, tq + tk), lambda h, bi, qi: (h, 0, 0)),
            _const_spec((1, 2 * HEAD_DIM)),
        ],
        out_specs=pl.BlockSpec((1, tq, LANES), lambda h, bi, qi: (bi, qi, h)),
        out_shape=jax.ShapeDtypeStruct((b, s, DIFF_V), BF16),
        scratch_shapes=[
            pltpu.VMEM((LANES + DEN_ROWS, s), BF16),
            pltpu.VMEM((DIFF_SLOTS, 2, tk, tq), F32),
            pltpu.VMEM((DIFF_SLOTS, 2, tk, tq), BF16),
            pltpu.VMEM((2, LANES + DEN_ROWS, tq), F32),
            pltpu.VMEM((nt, tk, tq), F32),
        ],
        compiler_params=_params(("arbitrary", "arbitrary", "arbitrary")),
        name="diff_attn",
    )(lam_params, far, nat, nat, nat, bias_tiles, subln_g)


def _dil_geometry(s, window, dilation):
    l = s // dilation
    w = window // (2 * dilation)
    qb = l if l <= 2 * DIL_QB else DIL_QB
    kw = min(qb + 2 * LANES, l)
    assert w <= LANES and l % qb == 0 and l % LANES == 0
    return l, w, qb, kw, l // qb


def _dil_bias_tiles(rel_bias, g, s, window, dilation):
    l, w, qb, kw, nb = _dil_geometry(s, window, dilation)
    c0 = H_DIFF + g * H_DIL
    shifts = []
    for n in (0, min(1, nb - 1), nb - 1):
        l0 = n * qb
        shifts.append(min(max(l0 - LANES, 0), l - kw) - l0)
    j = jnp.arange(kw)[None, :, None]
    i = jnp.arange(qb)[None, None, :]
    off = jnp.asarray(shifts)[:, None, None] + j - i
    b = _bias_of_rel(rel_bias[:, c0:c0 + H_DIL], off * dilation) * LOG2E
    b = jnp.where((jnp.abs(off) <= w)[None], b, NEG_INF)
    return jnp.stack([jnp.concatenate([b[2 * p], b[2 * p + 1]], axis=-1) for p in range(H_DIL // 2)], axis=1)


def _dil_attn_kernel(q_ref, k_ref, v_ref, bias_ref, o_ref, lse_ref, qt_ref, vt_ref, st_ref,
                     *, l, qb, kw, nb, cps):
    chunk = min(512, l)
    first = lax.broadcasted_iota(jnp.int32, (LANES, qb), 0) < HEAD_DIM

    def transposes(c):
        for pair in range(H_DIL // 2):
            cs = slice(pair * LANES, (pair + 1) * LANES)
            for i in range(l // chunk):
                rs = slice(i * chunk, (i + 1) * chunk)
                qt_ref[c, pair, :, rs] = q_ref[0, c, rs, cs].astype(F32).T.astype(BF16)
                vt_ref[c, pair, :LANES, rs] = v_ref[0, c, rs, cs].astype(F32).T.astype(BF16)
            vt_ref[c, pair, LANES:, :] = jnp.ones((DEN_ROWS, l), BF16)

    def window(n):
        l0 = n * qb
        return l0, min(max(l0 - LANES, 0), l - kw)

    def scores(c, n, slot):
        l0, ws = window(n)
        for pair in range(H_DIL // 2):
            qt = qt_ref[c, pair, :, l0:l0 + qb]
            q2 = jnp.concatenate([jnp.where(first, qt, jnp.zeros_like(qt)),
                                  jnp.where(first, jnp.zeros_like(qt), qt)], axis=1)
            kp = k_ref[0, c, ws:ws + kw, pair * LANES:(pair + 1) * LANES]
            st_ref[c, slot, pair] = jnp.dot(kp, q2, preferred_element_type=F32)

    def finish(c, n, slot):
        l0, ws = window(n)
        t = 0 if n == 0 else (2 if n == nb - 1 else 1)
        for pair in range(H_DIL // 2):
            cs = slice(pair * LANES, (pair + 1) * LANES)
            st = st_ref[c, slot, pair] + bias_ref[t, pair]
            mx = jnp.max(st, axis=0, keepdims=True)
            p = jnp.exp2(st - mx).astype(BF16)
            acc = jnp.dot(vt_ref[c, pair, :, ws:ws + kw], p, preferred_element_type=F32)
            den = acc[LANES:LANES + 1, :]
            ot = acc[:LANES, :] / den
            lse = mx * LN2 + jnp.log(den)
            o_t = jnp.where(first, ot[:, :qb], ot[:, qb:])
            lse_t = jnp.where(first, lse[:, :qb], lse[:, qb:])
            o_ref[0, c, l0:l0 + qb, cs] = o_t.T.astype(BF16)
            lse_ref[0, c, l0:l0 + qb, cs] = lse_t.T

    for c in range(cps):
        transposes(c)
    for c in range(cps):
        scores(c, 0, 0)
    for c in range(cps):
        for n in range(nb):
            if n + 1 < nb:
                scores(c, n + 1, (n + 1) % 2)
            finish(c, n, n % 2)


def _dil_attn(arr, col0, bias_tiles, g, window, dilation):
    b, d, l, _ = arr.shape
    _, w, qb, kw, nb = _dil_geometry(l * d, window, dilation)
    cps = max(1, min(d, DIL_STEP_ROWS // l))
    kern = functools.partial(_dil_attn_kernel, l=l, qb=qb, kw=kw, nb=nb, cps=cps)
    return pl.pallas_call(
        kern,
        grid=(b, d // cps),
        in_specs=[
            pl.BlockSpec((1, cps, l, GRP_W), lambda bi, r: (bi, r, 0, col0)),
            pl.BlockSpec((1, cps, l, GRP_W), lambda bi, r: (bi, r, 0, col0 + 1)),
            pl.BlockSpec((1, cps, l, GRP_W), lambda bi, r: (bi, r, 0, col0 + 2)),
            _const_spec((3, H_DIL // 2, kw, 2 * qb)),
        ],
        out_specs=[
            pl.BlockSpec((1, cps, l, GRP_W), lambda bi, r: (bi, r, 0, 0)),
            pl.BlockSpec((1, cps, l, GRP_W), lambda bi, r: (bi, r, 0, 0)),
        ],
        out_shape=[
            jax.ShapeDtypeStruct((b, d, l, GRP_W), BF16),
            jax.ShapeDtypeStruct((b, d, l, GRP_W), F32),
        ],
        scratch_shapes=[
            pltpu.VMEM((cps, H_DIL // 2, LANES, l), BF16),
            pltpu.VMEM((cps, H_DIL // 2, LANES + DEN_ROWS, l), BF16),
            pltpu.VMEM((cps, 2, H_DIL // 2, kw, 2 * qb), F32),
        ],
        compiler_params=_params(("parallel", "parallel")),
        name=f"dil_attn_g{g}",
    )(arr, arr, arr, bias_tiles)


def _pack_words(a, b):
    ua = lax.bitcast_convert_type(a.astype(BF16).astype(F32), jnp.uint32)
    ub = lax.bitcast_convert_type(b.astype(BF16).astype(F32), jnp.uint32)
    return (ua >> 16) | (ub & jnp.uint32(0xFFFF0000))


def _unpack_words(w):
    return (lax.bitcast_convert_type(w << 16, F32),
            lax.bitcast_convert_type(w & jnp.uint32(0xFFFF0000), F32))


def _route(lg):
    lane = lax.broadcasted_iota(jnp.int32, lg.shape, 1)
    big = jnp.int32(ROUTE_LANES)
    gl = jnp.where(lane < N_GROUPS, lg, -jnp.inf)
    gmax = jnp.max(gl, axis=-1, keepdims=True)
    gi = jnp.min(jnp.where(gl == gmax, lane, big), axis=-1, keepdims=True)
    gw = 1.0 / jnp.sum(jnp.exp(gl - gmax), axis=-1, keepdims=True)
    e_lo = EXP_LANE0 + E_PER_GROUP * gi
    es = jnp.where((lane >= e_lo) & (lane < e_lo + E_PER_GROUP), lg, -jnp.inf)
    v1 = jnp.max(es, axis=-1, keepdims=True)
    i1 = jnp.min(jnp.where(es == v1, lane, big), axis=-1, keepdims=True)
    es2 = jnp.where(lane == i1, -jnp.inf, es)
    v2 = jnp.max(es2, axis=-1, keepdims=True)
    i2 = jnp.min(jnp.where(es2 == v2, lane, big), axis=-1, keepdims=True)
    t = jnp.exp(v2 - v1)
    w1 = gw / (1.0 + t)
    w2 = w1 * t
    return gi, jnp.where(lane == i1, w1, 0.0) + jnp.where(lane == i2, w2, 0.0)


def _merge_route_kernel(x_ref, oa_ref, o0_ref, l0_ref, o1_ref, l1_ref, o2_ref, l2_ref, nm_ref, wgt_ref,
                        wa_ref, wb_ref, wo_ref, nf_ref, wr_ref, x1_ref, hw_ref, gid_ref,
                        so1_ref, sl1_ref, so2_ref, sl2_ref, *, tm):
    for src, dst, d in ((o1_ref, so1_ref, DIL_CONFIGS[1][1]), (l1_ref, sl1_ref, DIL_CONFIGS[1][1]),
                        (o2_ref, so2_ref, DIL_CONFIGS[2][1]), (l2_ref, sl2_ref, DIL_CONFIGS[2][1])):
        for r in range(d):
            for c in range(GRP_W // LANES):
                dst[c, pl.ds(r, tm // d, stride=d), :] = src[0, r, :, c * LANES:(c + 1) * LANES].astype(F32)
    subs = [slice(i * SUB_MERGE, (i + 1) * SUB_MERGE) for i in range(tm // SUB_MERGE)]
    unsplit = lambda ref, rs: jnp.concatenate([ref[c, rs, :] for c in range(GRP_W // LANES)], axis=1)

    def dil_mix(rs):
        l0, l1, l2 = l0_ref[0, 0, rs, :], unsplit(sl1_ref, rs), unsplit(sl2_ref, rs)
        mx = jnp.maximum(jnp.maximum(l0, l1), l2)
        e0, e1, e2 = jnp.exp(l0 - mx), jnp.exp(l1 - mx), jnp.exp(l2 - mx)
        ob = (e0 * o0_ref[0, 0, rs, :].astype(F32) + e1 * unsplit(so1_ref, rs)
              + e2 * unsplit(so2_ref, rs)) / (e0 + e1 + e2)
        return ob.astype(BF16)

    obs = [dil_mix(rs) for rs in subs]
    pas = [jnp.dot(oa_ref[0, rs, :], wa_ref[...], preferred_element_type=F32) for rs in subs]
    pbs = [jnp.dot(ob, wb_ref[...], preferred_element_type=F32) for ob in obs]

    def mixer_norm(rs):
        x = x_ref[0, rs, :]
        ms = jnp.mean(x * x, axis=-1, keepdims=True)
        return (x * lax.rsqrt(ms + EPS) * nm_ref[...]).astype(BF16)

    hms = [mixer_norm(rs) for rs in subs]
    gts = [jnp.dot(hm, wgt_ref[...], preferred_element_type=F32) for hm in hms]

    def gate(gt, pa, pb):
        return (jax.nn.sigmoid(gt[:, :D_MODEL]) * pa + jax.nn.sigmoid(gt[:, D_MODEL:]) * pb).astype(BF16)

    mgs = [gate(gt, pa, pb) for gt, pa, pb in zip(gts, pas, pbs)]
    x1s = [x_ref[0, rs, :] + jnp.dot(mg, wo_ref[...], preferred_element_type=F32)
           for rs, mg in zip(subs, mgs)]

    def ffn_norm(rs, x1):
        x1_ref[0, rs, :] = x1
        ms = jnp.mean(x1 * x1, axis=-1, keepdims=True)
        h2f = x1 * lax.rsqrt(ms + EPS) * nf_ref[...]
        hw_ref[0, rs, :] = _pack_words(h2f[:, :D_MODEL // 2], h2f[:, D_MODEL // 2:])
        return h2f.astype(BF16)

    h2s = [ffn_norm(rs, x1) for rs, x1 in zip(subs, x1s)]
    lgs = [jnp.dot(h2, wr_ref[...], preferred_element_type=F32) for h2 in h2s]
    for rs, lg in zip(subs, lgs):
        gi, _ = _route(lg)
        git = jnp.broadcast_to(gi.astype(F32), lg.shape).T
        gid_ref[0, 0, :, rs] = git[:SUBLANES, :].astype(jnp.int32)


def _merge_route(x, oa, dil, nm, w_gate, wa, wb, wo, nf, wr):
    b, s, _ = x.shape
    tm = TM_MERGE
    d1, d2 = DIL_CONFIGS[1][1], DIL_CONFIGS[2][1]
    tok = lambda bi, i: (bi, i, 0)
    cls = lambda bi, i: (bi, 0, i, 0)
    (o0, s0), (o1, s1), (o2, s2) = dil
    return pl.pallas_call(
        functools.partial(_merge_route_kernel, tm=tm),
        grid=(b, s // tm),
        in_specs=[
            pl.BlockSpec((1, tm, D_MODEL), tok),
            pl.BlockSpec((1, tm, DIFF_V), tok),
            pl.BlockSpec((1, 1, tm, GRP_W), cls), pl.BlockSpec((1, 1, tm, GRP_W), cls),
            pl.BlockSpec((1, d1, tm // d1, GRP_W), cls), pl.BlockSpec((1, d1, tm // d1, GRP_W), cls),
            pl.BlockSpec((1, d2, tm // d2, GRP_W), cls), pl.BlockSpec((1, d2, tm // d2, GRP_W), cls),
            _const_spec((1, D_MODEL)),
            _const_spec((D_MODEL, GATE_COLS)),
            _const_spec((DIFF_V, D_MODEL)),
            _const_spec((GRP_W, D_MODEL)),
            _const_spec((D_MODEL, D_MODEL)),
            _const_spec((1, D_MODEL)),
            _const_spec((D_MODEL, ROUTE_LANES)),
        ],
        out_specs=[
            pl.BlockSpec((1, tm, D_MODEL), tok),
            pl.BlockSpec((1, tm, D_MODEL // 2), tok),
            pl.BlockSpec((1, 1, SUBLANES, tm), lambda bi, i: (bi, i, 0, 0)),
        ],
        out_shape=[
            jax.ShapeDtypeStruct((b, s, D_MODEL), F32),
            jax.ShapeDtypeStruct((b, s, D_MODEL // 2), jnp.uint32),
            jax.ShapeDtypeStruct((b, s // tm, SUBLANES, tm), jnp.int32),
        ],
        scratch_shapes=[pltpu.VMEM((GRP_W // LANES, tm, LANES), F32) for _ in range(4)],
        compiler_params=_params(("parallel", "parallel")),
        name="merge_route",
    )(x, oa, o0, s0, o1, s1, o2, s2, nm, w_gate, wa, wb, wo, nf, wr)


def _dispatch_plan_kernel(gid_ref, dest_ref, tgrp_ref, *, rows, cols):
    gid = gid_ref[...]
    ci = lax.broadcasted_iota(jnp.int32, (cols, cols), 0)
    cj = lax.broadcasted_iota(jnp.int32, (cols, cols), 1)
    upto = (ci <= cj).astype(BF16)
    ri = lax.broadcasted_iota(jnp.int32, (rows, rows), 0)
    rj = lax.broadcasted_iota(jnp.int32, (rows, rows), 1)
    above = (rj < ri).astype(F32)
    seg = jnp.zeros((1, 1), F32)
    dest = jnp.zeros((rows, cols), F32)
    ends = []
    for g in range(N_GROUPS):
        mem = (gid == g).astype(F32)
        pref = jnp.dot(mem.astype(BF16), upto, preferred_element_type=F32)
        tot = pref[:, cols - 1:cols]
        before = jnp.dot(above, jnp.broadcast_to(tot, (rows, LANES)), preferred_element_type=F32,
                         precision=lax.Precision.HIGHEST)[:, :1]
        dest = dest + mem * (seg + before + pref - 1.0)
        cnt = jnp.sum(tot, axis=0, keepdims=True)
        seg = seg + jnp.ceil(cnt / TM_MOE) * TM_MOE
        ends.append(seg)
    dest_ref[...] = dest.astype(jnp.int32)
    tile = (lax.broadcasted_iota(jnp.int32, (SUBLANES, LANES), 0) * LANES
            + lax.broadcasted_iota(jnp.int32, (SUBLANES, LANES), 1))
    start = (tile * TM_MOE).astype(F32)
    tgrp = jnp.zeros((SUBLANES, LANES), jnp.int32)
    for g in range(N_GROUPS - 1):
        tgrp = tgrp + (start >= ends[g]).astype(jnp.int32)
    tgrp_ref[...] = tgrp


def _dispatch_plan(gid):
    rows, cols = gid.shape
    assert rows % SUBLANES == 0 and rows * cols // TM_MOE + N_GROUPS <= SUBLANES * LANES
    return pl.pallas_call(
        functools.partial(_dispatch_plan_kernel, rows=rows, cols=cols),
        out_shape=[jax.ShapeDtypeStruct((rows, cols), jnp.int32),
                   jax.ShapeDtypeStruct((SUBLANES, LANES), jnp.int32)],
        compiler_params=pltpu.CompilerParams(vmem_limit_bytes=VMEM_LIMIT),
        name="dispatch_plan",
    )(gid)


def _moe_group_kernel(tgrp_ref, hw_ref, wr_ref, wg_ref, wu_ref, wd_ref, ow_ref, he_ref):
    g = tgrp_ref[pl.program_id(0)]
    lo, hi = _unpack_words(hw_ref[...])
    h2 = jnp.concatenate([lo, hi], axis=1).astype(BF16)
    _, dw = _route(jnp.dot(h2, wr_ref[...], preferred_element_type=F32))
    lane = lax.broadcasted_iota(jnp.int32, dw.shape, 1)
    for e in range(E_PER_GROUP):
        we = jnp.sum(jnp.where(lane == EXP_LANE0 + E_PER_GROUP * g + e, dw, 0.0), axis=-1, keepdims=True)
        gp = jnp.dot(h2, wg_ref[0, e], preferred_element_type=F32)
        up = jnp.dot(h2, wu_ref[0, e], preferred_element_type=F32)
        he_ref[:, e * D_EXPERT:(e + 1) * D_EXPERT] = (jax.nn.silu(gp) * up * we).astype(BF16)
    moe = jnp.dot(he_ref[...], wd_ref[0], preferred_element_type=F32)
    ow_ref[...] = _pack_words(moe[:, :D_MODEL // 2], moe[:, D_MODEL // 2:])


def _moe_groups(tgrp, hw_sorted, wr, wg, wu, wd):
    m = hw_sorted.shape[0]
    tm = TM_MOE
    row = lambda i, t: (i, 0)
    grp = lambda i, t: (t[i], 0, 0, 0)
    return pl.pallas_call(
        _moe_group_kernel,
        grid_spec=pltpu.PrefetchScalarGridSpec(
            num_scalar_prefetch=1,
            grid=(m // tm,),
            in_specs=[
                pl.BlockSpec((tm, D_MODEL // 2), row),
                pl.BlockSpec((D_MODEL, ROUTE_LANES), lambda i, t: (0, 0)),
                pl.BlockSpec((1, E_PER_GROUP, D_MODEL, D_EXPERT), grp),
                pl.BlockSpec((1, E_PER_GROUP, D_MODEL, D_EXPERT), grp),
                pl.BlockSpec((1, E_PER_GROUP * D_EXPERT, D_MODEL), lambda i, t: (t[i], 0, 0)),
            ],
            out_specs=pl.BlockSpec((tm, D_MODEL // 2), row),
            scratch_shapes=[pltpu.VMEM((tm, E_PER_GROUP * D_EXPERT), BF16)],
        ),
        out_shape=jax.ShapeDtypeStruct((m, D_MODEL // 2), jnp.uint32),
        compiler_params=_params(("arbitrary",)),
        name="moe_groups",
    )(tgrp, hw_sorted, wr, wg, wu, wd)


def _final_norm_kernel(x1_ref, yw_ref, nf_ref, y_ref):
    lo, hi = _unpack_words(yw_ref[...])
    x2 = x1_ref[...] + jnp.concatenate([lo, hi], axis=1)
    ms = jnp.mean(x2 * x2, axis=-1, keepdims=True)
    y_ref[...] = x2 * lax.rsqrt(ms + EPS) * nf_ref[...]


def _final_norm(x1, yw, nf):
    n = x1.shape[0]
    tm = TM_FINAL
    row = lambda i: (i, 0)
    return pl.pallas_call(
        _final_norm_kernel,
        grid=(n // tm,),
        in_specs=[pl.BlockSpec((tm, D_MODEL), row), pl.BlockSpec((tm, D_MODEL // 2), row),
                  _const_spec((1, D_MODEL))],
        out_specs=pl.BlockSpec((tm, D_MODEL), row),
        out_shape=jax.ShapeDtypeStruct((n, D_MODEL), F32),
        compiler_params=_params(("parallel",)),
        name="final_norm",
    )(x1, yw, nf)


def _sc_mesh():
    return plsc.VectorSubcoreMesh(core_axis_name="core", subcore_axis_name="subcore")


def _sc_scatter_rows(x, idx, m):
    n, d = x.shape

    @pl.kernel(out_type=jax.ShapeDtypeStruct((m, d), x.dtype), mesh=_sc_mesh(), scratch_types=[])
    def scatter(x_hbm, i_hbm, o_hbm):
        for c in range(d // SC_COLS):
            cols = pl.ds(c * SC_COLS, SC_COLS)
            o_cols = o_hbm.at[:, cols]

            def body(x_vmem, i_vmem, o_cols=o_cols):
                pltpu.sync_copy(x_vmem, o_cols.at[i_vmem.at[0]])

            pltpu.emit_pipeline(
                body,
                grid=(n // SC_WINDOW,),
                in_specs=[pl.BlockSpec((SC_WINDOW, SC_COLS), lambda i, c=c: (i, c)),
                          pl.BlockSpec((1, SC_WINDOW), lambda i: (0, i))],
                out_specs=[],
                core_axis_name=("core", "subcore"),
                dimension_semantics=(pltpu.PARALLEL,),
            )(x_hbm, i_hbm)

    return scatter(x, idx.reshape(1, n))


def _sc_gather_rows(x, idx):
    n = idx.shape[0]
    d = x.shape[1]

    @pl.kernel(out_type=jax.ShapeDtypeStruct((n, d), x.dtype), mesh=_sc_mesh(), scratch_types=[])
    def gather(x_hbm, i_hbm, o_hbm):
        for c in range(d // SC_COLS):
            x_cols = x_hbm.at[:, pl.ds(c * SC_COLS, SC_COLS)]

            def body(i_vmem, o_vmem, x_cols=x_cols):
                pltpu.sync_copy(x_cols.at[i_vmem.at[0]], o_vmem)

            pltpu.emit_pipeline(
                body,
                grid=(n // SC_WINDOW,),
                in_specs=[pl.BlockSpec((1, SC_WINDOW), lambda i: (0, i))],
                out_specs=[pl.BlockSpec((SC_WINDOW, SC_COLS), lambda i, c=c: (i, c))],
                core_axis_name=("core", "subcore"),
                dimension_semantics=(pltpu.PARALLEL,),
            )(i_hbm, o_hbm)

    return gather(x, idx.reshape(1, n))


def _prepare_weights(rel_bias, norm_mix, w_in, lam_q1, lam_k1, lam_q2, lam_k2, subln_g,
                     w_branch_a, w_branch_b, w_out, norm_ffn, w_group, w_router,
                     w_exp_gate, w_exp_up, w_exp_down, norm_final):
    w = w_in[0]
    c = QK_SCALE * LOG2E
    o = 0
    qa, o = w[:, o:o + DIFF_QK] * c, o + DIFF_QK
    ka, o = w[:, o:o + DIFF_QK], o + DIFF_QK
    va, o = w[:, o:o + DIFF_V], o + DIFF_V
    qb, o = w[:, o:o + DIL_W] * c, o + DIL_W
    kb, o = w[:, o:o + DIL_W], o + DIL_W
    vb, o = w[:, o:o + DIL_W], o + DIL_W
    w_gate = w[:, o:].astype(BF16)
    grp = lambda g: [t[:, g * GRP_W:(g + 1) * GRP_W] for t in (qb, kb, vb)]
    w_nat = jnp.concatenate([qa, ka, va] + grp(0), axis=1).astype(BF16)
    w_d1 = jnp.concatenate(grp(1), axis=1).astype(BF16)
    w_d2 = jnp.concatenate(grp(2), axis=1).astype(BF16)
    wr = jnp.concatenate([w_group[0], w_router[0].transpose(1, 0, 2).reshape(D_MODEL, N_EXPERTS)], axis=1)
    wr = jnp.pad(wr, ((0, 0), (0, ROUTE_LANES - wr.shape[1]))).astype(BF16)
    return dict(
        rel_bias=rel_bias.astype(F32),
        norm_mix=norm_mix[0][None].astype(F32),
        w_nat=w_nat, w_gate=w_gate, w_d1=w_d1, w_d2=w_d2,
        lam=jnp.stack([lam_q1[0], lam_k1[0], lam_q2[0], lam_k2[0]]).astype(F32),
        subln_g=subln_g[0][None].astype(F32),
        wa=w_branch_a[0].astype(BF16), wb=w_branch_b[0].astype(BF16), wo=w_out[0].astype(BF16),
        norm_ffn=norm_ffn[0][None].astype(F32), wr=wr,
        wg=w_exp_gate[0].reshape(N_GROUPS, E_PER_GROUP, D_MODEL, D_EXPERT).astype(BF16),
        wu=w_exp_up[0].reshape(N_GROUPS, E_PER_GROUP, D_MODEL, D_EXPERT).astype(BF16),
        wd=w_exp_down[0].reshape(N_GROUPS, E_PER_GROUP * D_EXPERT, D_MODEL).astype(BF16),
        norm_final=norm_final[None].astype(F32),
    )


def _encoder(x, p):
    b, s, d = x.shape
    n = b * s
    nat, dil1, dil2 = _norm_proj(x, p["norm_mix"], p["w_nat"], p["w_d1"], p["w_d2"])
    tiles, far = _diff_bias_tiles(p["rel_bias"], min(TQ_DIFF, s), min(TK_DIFF, s))
    oa = _diff_attn(nat, tiles, far, p["lam"], p["subln_g"])
    srcs = ((nat.reshape(b, 1, s, NAT_COLS), (2 * DIFF_QK + DIFF_V) // GRP_W), (dil1, 0), (dil2, 0))
    dil = []
    for g, (win, dilation) in enumerate(DIL_CONFIGS):
        tiles = _dil_bias_tiles(p["rel_bias"], g, s, win, dilation)
        dil.append(_dil_attn(srcs[g][0], srcs[g][1], tiles, g, win, dilation))
    x1, hw, gid = _merge_route(x, oa, dil, p["norm_mix"], p["w_gate"], p["wa"], p["wb"], p["wo"],
                               p["norm_ffn"], p["wr"])
    dest, tgrp = _dispatch_plan(gid[:, :, 0, :].reshape(n // TM_MERGE, TM_MERGE))
    dest = dest.reshape(n)
    n_tiles = n // TM_MOE + N_GROUPS
    hw_sorted = _sc_scatter_rows(hw.reshape(n, d // 2), dest, n_tiles * TM_MOE)
    yw_sorted = _moe_groups(tgrp.reshape(-1)[:n_tiles], hw_sorted, p["wr"], p["wg"], p["wu"], p["wd"])
    y = _final_norm(x1.reshape(n, d), _sc_gather_rows(yw_sorted, dest), p["norm_final"])
    return y.reshape(b, s, d)


def kernel(x_prompt, x_sample, rel_bias, norm_mix, w_in, lam_q1, lam_k1, lam_q2, lam_k2, subln_g,
           w_branch_a, w_branch_b, w_out, norm_ffn, w_group, w_router, w_exp_gate, w_exp_up,
           w_exp_down, norm_final):
    p = _prepare_weights(rel_bias, norm_mix, w_in, lam_q1, lam_k1, lam_q2, lam_k2, subln_g,
                         w_branch_a, w_branch_b, w_out, norm_ffn, w_group, w_router,
                         w_exp_gate, w_exp_up, w_exp_down, norm_final)
    return (_encoder(x_prompt, p), _encoder(x_sample, p))
```
